```python
import math
import jax
import jax.numpy as jnp
from jax import lax
import numpy as np

D_MODEL = 4096
BATCH = 4
SEQ = 2048
DEPTH = 4
DEC_BATCH = 8
DEC_SEQ = 4
PAST_LEN = 8192
PAGE_SIZE = 128

A_WIDTH = D_MODEL // 2
A_HEADS = 16
A_DK = A_WIDTH // A_HEADS
A_DV = A_WIDTH // A_HEADS
A_CHUNK = 64
B_WIDTH = D_MODEL // 2
B_BLOCKS = 16
B_BW = B_WIDTH // B_BLOCKS
CONV_W = 4
RG_C = 8.0
C_HEADS = 32
C_HEAD_DIM = 128
C_KV_HEADS = 8
IDX_HEADS = 32
IDX_DIM = 128
TOPK_MAX = 256
Q_BLOCK = 128
ROPE_THETA = 500000.0
D_FF = 4 * D_MODEL
PE_DIM = 256
N_AB = (DEPTH + 1) // 2
N_C = DEPTH // 2
EPS = 1e-6
NEG_BIG = -1e30
AB_IN = 4 * A_WIDTH + 2 * B_WIDTH
C_IN = C_HEADS * C_HEAD_DIM + 2 * C_KV_HEADS * C_HEAD_DIM + IDX_HEADS * IDX_DIM + IDX_DIM + IDX_HEADS

kernel_name = 'hybrid_hgrn2_rglru_dsa_step'


def rmsnorm(x, g):
    xf = x.astype(jnp.float32)
    y = xf * lax.rsqrt(jnp.mean(xf * xf, axis=-1, keepdims=True) + EPS)
    return (y * g.astype(jnp.float32)).astype(x.dtype)


def layernorm(x, g, b):
    xf = x.astype(jnp.float32)
    mu = jnp.mean(xf, axis=-1, keepdims=True)
    xc = xf - mu
    y = xc * lax.rsqrt(jnp.mean(xc * xc, axis=-1, keepdims=True) + EPS)
    return (y * g.astype(jnp.float32) + b.astype(jnp.float32)).astype(x.dtype)


def partial_rope(x, pos):
    rd = x.shape[-1] // 4
    half = rd // 2
    inv = jnp.exp(-math.log(ROPE_THETA) * jnp.arange(half, dtype=jnp.float32) * (2.0 / rd))
    ang = pos.astype(jnp.float32)[:, None] * inv[None, :]
    cos = jnp.cos(ang)[None, :, None, :]
    sin = jnp.sin(ang)[None, :, None, :]
    xf = x.astype(jnp.float32)
    x1, x2 = xf[..., :half], xf[..., half:rd]
    out = jnp.concatenate([x1 * cos - x2 * sin, x2 * cos + x1 * sin, xf[..., rd:]], axis=-1)
    return out.astype(x.dtype)


def hgrn2_recurrence(q, k, v, logf, s0):
    bsz, t, h, _ = q.shape
    dv = v.shape[-1]
    c = min(A_CHUNK, t)
    n = -(-t // c)
    pad = n * c - t

    def prep(a):
        a = jnp.pad(a.astype(jnp.float32), ((0, 0), (0, pad), (0, 0), (0, 0)))
        return a.reshape(bsz, n, c, h, a.shape[-1]).transpose(1, 0, 3, 2, 4)

    tril = jnp.tril(jnp.ones((c, c), bool))

    def step(s, inp):
        qi, ki, vi, gi = inp
        g = jnp.cumsum(gi, axis=2)
        o_inter = jnp.einsum('bhck,bhkv->bhcv', qi * jnp.exp(g), s)
        diff = g[:, :, :, None, :] - g[:, :, None, :, :]
        decay = jnp.exp(jnp.where(tril[:, :, None], diff, NEG_BIG))
        att = jnp.einsum('bhtk,bhsk,bhtsk->bhts', qi, ki, decay)
        o = o_inter + jnp.einsum('bhts,bhsv->bhtv', att, vi)
        g_last = g[:, :, -1:, :]
        s_new = jnp.exp(g_last[:, :, 0, :])[..., None] * s + jnp.einsum('bhsk,bhsv->bhkv', ki * jnp.exp(g_last - g), vi)
        return s_new, o

    s_t, o = lax.scan(step, s0.astype(jnp.float32), (prep(q), prep(k), prep(v), prep(logf)))
    o = o.transpose(1, 0, 3, 2, 4).reshape(bsz, n * c, h, dv)[:, :t]
    return o, s_t


def causal_conv(x, buf, w, b):
    t = x.shape[1]
    xp = jnp.concatenate([buf.astype(x.dtype), x], axis=1)
    y = b + sum(w[j] * xp[:, j:j + t] for j in range(CONV_W))
    return y, xp[:, -(CONV_W - 1):]


def rglru(x, h0, pos, wa, ba, wx, bx, lam):
    bsz, t, w = x.shape
    xf = x.astype(jnp.float32)
    xb = xf.reshape(bsz, t, B_BLOCKS, B_BW)
    r = jax.nn.sigmoid(jnp.einsum('btni,nij->btnj', xb, wa.astype(jnp.float32)).reshape(bsz, t, w) + ba)
    gi = jax.nn.sigmoid(jnp.einsum('btni,nij->btnj', xb, wx.astype(jnp.float32)).reshape(bsz, t, w) + bx)
    log_a = -RG_C * r * jax.nn.softplus(-lam.astype(jnp.float32))
    a = jnp.exp(log_a)
    mult = jnp.sqrt(-jnp.expm1(2.0 * log_a))
    mult = jnp.where((pos == 0)[None, :, None], 1.0, mult)
    bterm = mult * gi * xf
    bterm = bterm.at[:, 0].add(a[:, 0] * h0.astype(jnp.float32))

    def combine(lhs, rhs):
        a1, b1 = lhs
        a2, b2 = rhs
        return a1 * a2, a2 * b1 + b2

    _, hseq = lax.associative_scan(combine, (a, bterm), axis=1)
    return hseq, hseq[:, -1]


def mix_ab(xn, pos, s0, h0, conv0, lb, w_in, w_out, a_norm, conv_w, conv_b, wa, ba, wx, bx, lam):
    bsz, t, _ = xn.shape
    proj = xn @ w_in
    qa, fa, ia, ga, gb, xb = jnp.split(proj, [A_WIDTH, 2 * A_WIDTH, 3 * A_WIDTH, 4 * A_WIDTH, 4 * A_WIDTH + B_WIDTH], axis=-1)
    heads = lambda a: a.reshape(bsz, t, A_HEADS, -1)
    z = heads(fa).astype(jnp.float32)
    lbh = lb.astype(jnp.float32).reshape(A_HEADS, A_DK)
    logf = jax.nn.log_sigmoid(z) + jnp.log1p(lbh * jnp.exp(-z))
    kk = (1.0 - lbh) * jax.nn.sigmoid(-z)
    qq = jax.nn.silu(heads(qa).astype(jnp.float32))
    o_a, s_new = hgrn2_recurrence(qq, kk, heads(ia), logf, s0)
    o_a = rmsnorm(o_a, a_norm.reshape(A_HEADS, A_DV)).reshape(bsz, t, A_WIDTH) * jax.nn.silu(ga.astype(jnp.float32))
    xc, conv_new = causal_conv(xb, conv0, conv_w, conv_b)
    hseq, h_last = rglru(xc, h0, pos, wa, ba, wx, bx, lam)
    o_b = jax.nn.gelu(gb.astype(jnp.float32)) * hseq
    out = jnp.concatenate([o_a, o_b], axis=-1).astype(xn.dtype) @ w_out
    return out, s_new, h_last, conv_new


def dsa_project(xn, pos, w_in, qn, kn, kg, kb):
    bsz, t, _ = xn.shape
    proj = xn @ w_in
    cuts = np.cumsum([C_HEADS * C_HEAD_DIM, C_KV_HEADS * C_HEAD_DIM, C_KV_HEADS * C_HEAD_DIM, IDX_HEADS * IDX_DIM, IDX_DIM]).tolist()
    q, k, v, qi, ki, wi = jnp.split(proj, cuts, axis=-1)
    q = partial_rope(rmsnorm(q.reshape(bsz, t, C_HEADS, C_HEAD_DIM), qn), pos)
    k = partial_rope(rmsnorm(k.reshape(bsz, t, C_KV_HEADS, C_HEAD_DIM), kn), pos)
    v = v.reshape(bsz, t, C_KV_HEADS, C_HEAD_DIM)
    qi = partial_rope(qi.reshape(bsz, t, IDX_HEADS, IDX_DIM), pos)
    ki = partial_rope(layernorm(ki, kg, kb)[:, :, None, :], pos)[:, :, 0, :]
    wi = wi.astype(jnp.float32) * (IDX_HEADS * IDX_DIM) ** -0.5
    return q, k, v, qi, ki, wi


def indexer_scores(qi, ki, wi):
    dots = jnp.einsum('bthd,bsd->bths', qi.astype(jnp.float32), ki.astype(jnp.float32))
    return jnp.einsum('bths,bth->bts', jax.nn.relu(dots), wi)


def sparse_attend(q, ksel, vsel, valid):
    bsz, t, _, _ = q.shape
    qg = q.astype(jnp.float32).reshape(bsz, t, C_KV_HEADS, C_HEADS // C_KV_HEADS, C_HEAD_DIM)
    s = jnp.einsum('btngd,btknd->btngk', qg, ksel.astype(jnp.float32)) * C_HEAD_DIM ** -0.5
    s = jnp.where(valid[:, :, None, None, :], s, NEG_BIG)
    p = jax.nn.softmax(s, axis=-1)
    o = jnp.einsum('btngk,btknd->btngd', p, vsel.astype(jnp.float32))
    return o.reshape(bsz, t, C_HEADS * C_HEAD_DIM)


def dsa_prompt(xn, pos, w_in, w_out, qn, kn, kg, kb):
    bsz, t, _ = xn.shape
    q, k, v, qi, ki, wi = dsa_project(xn, pos, w_in, qn, kn, kg, kb)
    topk = min(TOPK_MAX, t // 4)
    qb = min(Q_BLOCK, t)
    nblk = t // qb
    key_pos = jnp.arange(t)
    gather = jax.vmap(lambda a, ix: a[ix])

    def block(i):
        t0 = i * qb
        sl = lambda a: lax.dynamic_slice_in_dim(a, t0, qb, axis=1)
        qpos = t0 + jnp.arange(qb)
        sc = indexer_scores(sl(qi), ki, sl(wi))
        sc = jnp.where(key_pos[None, None, :] <= qpos[None, :, None], sc, NEG_BIG)
        _, idx = lax.top_k(sc, topk)
        valid = idx <= qpos[None, :, None]
        return sparse_attend(sl(q), gather(k, idx), gather(v, idx), valid)

    o = lax.map(block, jnp.arange(nblk))
    o = o.transpose(1, 0, 2, 3).reshape(bsz, t, C_HEADS * C_HEAD_DIM)
    return o.astype(xn.dtype) @ w_out, k, v, ki


def dsa_sample(xn, pos, ck, cv, cki, page_table, w_in, w_out, qn, kn, kg, kb):
    bsz, t, _ = xn.shape
    q, k, v, qi, ki, wi = dsa_project(xn, pos, w_in, qn, kn, kg, kb)
    n_pool, psz = ck.shape[0], ck.shape[1]
    past = page_table.shape[1] * psz
    total = past + t
    topk = min(TOPK_MAX, total // 4)
    ki_past = cki[page_table].reshape(bsz, past, IDX_DIM)
    ki_all = jnp.concatenate([ki_past.astype(ki.dtype), ki], axis=1)
    sc = indexer_scores(qi, ki_all, wi)
    key_pos = jnp.arange(total)
    sc = jnp.where(key_pos[None, None, :] <= pos[None, :, None], sc, NEG_BIG)
    _, idx = lax.top_k(sc, topk)
    valid = idx <= pos[None, :, None]
    in_past = idx < past
    ip = jnp.minimum(idx, past - 1)
    phys = jax.vmap(lambda pt, ix: pt[ix])(page_table, ip // psz) * psz + ip % psz
    inew = jnp.clip(idx - past, 0, t - 1)
    gather = jax.vmap(lambda a, ix: a[ix])

    def select(pool, new):
        flat = pool.reshape(n_pool * psz, C_KV_HEADS, C_HEAD_DIM)
        return jnp.where(in_past[..., None, None], flat[phys].astype(new.dtype), gather(new, inew))

    o = sparse_attend(q, select(ck, k), select(cv, v), valid)
    return o.astype(xn.dtype) @ w_out, k, v, ki


def ffn_pe(h, p_l, g_mlp, w_up, w_down, g_pe, w_pe, w_pg):
    hn = rmsnorm(h, g_mlp)
    h = h + (jnp.square(jax.nn.relu(hn @ w_up)) @ w_down).astype(h.dtype)
    gate = jax.nn.sigmoid((rmsnorm(h, g_pe) @ w_pg).astype(jnp.float32))
    return h + ((p_l @ w_pe).astype(jnp.float32) * gate).astype(h.dtype)


def setup_inputs(seed: int = 0) -> dict:
    key = jax.random.key(seed)
    keys = jax.random.split(key, 64)
    counter = [0]

    def nk():
        counter[0] += 1
        return keys[counter[0] - 1]

    def nrm(shape, scale=1.0):
        return jax.random.normal(nk(), shape, jnp.float32) * scale

    def gain(shape):
        return 1.0 + nrm(shape, 0.02)

    n_pages = PAST_LEN // PAGE_SIZE
    n_used = DEC_BATCH * n_pages
    n_pool = n_used + max(1, n_used // 4)
    x_prompt = nrm((BATCH, SEQ, D_MODEL))
    x_sample = nrm((DEC_BATCH, DEC_SEQ, D_MODEL))
    state_hgrn = nrm((N_AB, DEC_BATCH, A_HEADS, A_DK, A_DV), 0.5)
    state_rglru_h = nrm((N_AB, DEC_BATCH, B_WIDTH), 0.5)
    state_rglru_conv = nrm((N_AB, DEC_BATCH, CONV_W - 1, B_WIDTH))
    cache_k = nrm((N_C, n_pool, PAGE_SIZE, C_KV_HEADS, C_HEAD_DIM))
    cache_v = nrm((N_C, n_pool, PAGE_SIZE, C_KV_HEADS, C_HEAD_DIM))
    cache_kidx = nrm((N_C, n_pool, PAGE_SIZE, IDX_DIM))
    page_table = jax.random.permutation(nk(), n_pool)[:n_used].reshape(DEC_BATCH, n_pages).astype(jnp.int32)
    p_prompt = nrm((DEPTH, BATCH, SEQ, PE_DIM))
    p_sample = nrm((DEPTH, DEC_BATCH, DEC_SEQ, PE_DIM))
    u = jax.random.uniform(nk(), (N_AB, B_WIDTH), jnp.float32, 0.9, 0.999)
    a_base = u ** (1.0 / RG_C)
    rg_lambda = jnp.log(a_base) - jnp.log1p(-a_base)
    return {
        'x_prompt': x_prompt, 'x_sample': x_sample,
        'state_hgrn': state_hgrn, 'state_rglru_h': state_rglru_h, 'state_rglru_conv': state_rglru_conv,
        'cache_k': cache_k, 'cache_v': cache_v, 'cache_kidx': cache_kidx, 'page_table': page_table,
        'p_prompt': p_prompt, 'p_sample': p_sample,
        'norm_mix': gain((DEPTH, D_MODEL)), 'norm_mlp': gain((DEPTH, D_MODEL)), 'norm_pe': gain((DEPTH, D_MODEL)),
        'w_in_ab': nrm((N_AB, D_MODEL, AB_IN), D_MODEL ** -0.5),
        'w_out_ab': nrm((N_AB, A_WIDTH + B_WIDTH, D_MODEL), (A_WIDTH + B_WIDTH) ** -0.5),
        'hgrn_lb_logits': nrm((N_AB, A_WIDTH), 0.1),
        'hgrn_out_norm': gain((N_AB, A_WIDTH)),
        'rg_conv_w': nrm((N_AB, CONV_W, B_WIDTH), CONV_W ** -0.5),
        'rg_conv_b': nrm((N_AB, B_WIDTH), 0.01),
        'rg_wa': nrm((N_AB, B_BLOCKS, B_BW, B_BW), B_BW ** -0.5),
        'rg_ba': nrm((N_AB, B_WIDTH), 0.01),
        'rg_wx': nrm((N_AB, B_BLOCKS, B_BW, B_BW), B_BW ** -0.5),
        'rg_bx': nrm((N_AB, B_WIDTH), 0.01),
        'rg_lambda': rg_lambda,
        'w_in_c': nrm((N_C, D_MODEL, C_IN), D_MODEL ** -0.5),
        'w_out_c': nrm((N_C, C_HEADS * C_HEAD_DIM, D_MODEL), (C_HEADS * C_HEAD_DIM) ** -0.5),
        'c_q_norm': gain((N_C, C_HEAD_DIM)), 'c_k_norm': gain((N_C, C_HEAD_DIM)),
        'idx_k_ln_g': gain((N_C, IDX_DIM)), 'idx_k_ln_b': nrm((N_C, IDX_DIM), 0.01),
        'w_up': nrm((DEPTH, D_MODEL, D_FF), D_MODEL ** -0.5),
        'w_down': nrm((DEPTH, D_FF, D_MODEL), D_FF ** -0.5),
        'w_pe': nrm((DEPTH, PE_DIM, D_MODEL), PE_DIM ** -0.5),
        'w_pg': nrm((DEPTH, D_MODEL, D_MODEL), D_MODEL ** -0.5),
    }


def reference(x_prompt, x_sample, state_hgrn, state_rglru_h, state_rglru_conv, cache_k, cache_v, cache_kidx,
              page_table, p_prompt, p_sample, norm_mix, norm_mlp, norm_pe, w_in_ab, w_out_ab, hgrn_lb_logits,
              hgrn_out_norm, rg_conv_w, rg_conv_b, rg_wa, rg_ba, rg_wx, rg_bx, rg_lambda, w_in_c, w_out_c,
              c_q_norm, c_k_norm, idx_k_ln_g, idx_k_ln_b, w_up, w_down, w_pe, w_pg):
    f32 = jnp.float32
    bp, tp, _ = x_prompt.shape
    _, ts, _ = x_sample.shape
    past = page_table.shape[1] * cache_k.shape[2]
    pos_p = jnp.arange(tp, dtype=jnp.int32)
    pos_s = past + jnp.arange(ts, dtype=jnp.int32)
    lb_soft = jax.nn.softmax(hgrn_lb_logits.astype(f32), axis=0)
    lb_all = jnp.cumsum(lb_soft, axis=0) - lb_soft[0]
    s0_p = jnp.zeros((bp, A_HEADS, A_DK, A_DV), f32)
    h0_p = jnp.zeros((bp, B_WIDTH), f32)
    c0_p = jnp.zeros((bp, CONV_W - 1, B_WIDTH), x_prompt.dtype)
    hp, hs = x_prompt, x_sample
    hg_p, hg_s, rh_p, rh_s, rc_p, rc_s = [], [], [], [], [], []
    k_p, k_s, v_p, v_s, ki_p, ki_s = [], [], [], [], [], []
    for layer in range(DEPTH):
        j = layer // 2
        xpn = rmsnorm(hp, norm_mix[layer])
        xsn = rmsnorm(hs, norm_mix[layer])
        if layer % 2 == 0:
            wab = (lb_all[j], w_in_ab[j], w_out_ab[j], hgrn_out_norm[j], rg_conv_w[j], rg_conv_b[j],
                   rg_wa[j], rg_ba[j], rg_wx[j], rg_bx[j], rg_lambda[j])
            mp, sa, sb, sc = mix_ab(xpn, pos_p, s0_p, h0_p, c0_p, *wab)
            hg_p.append(sa)
            rh_p.append(sb)
            rc_p.append(sc)
            ms, sa, sb, sc = mix_ab(xsn, pos_s, state_hgrn[j], state_rglru_h[j], state_rglru_conv[j], *wab)
            hg_s.append(sa)
            rh_s.append(sb)
            rc_s.append(sc)
        else:
            wc = (w_in_c[j], w_out_c[j], c_q_norm[j], c_k_norm[j], idx_k_ln_g[j], idx_k_ln_b[j])
            mp, kk, vv, kx = dsa_prompt(xpn, pos_p, *wc)
            k_p.append(kk)
            v_p.append(vv)
            ki_p.append(kx)
            ms, kk, vv, kx = dsa_sample(xsn, pos_s, cache_k[j], cache_v[j], cache_kidx[j], page_table, *wc)
            k_s.append(kk)
            v_s.append(vv)
            ki_s.append(kx)
        hp = hp + mp.astype(hp.dtype)
        hs = hs + ms.astype(hs.dtype)
        wf = (norm_mlp[layer], w_up[layer], w_down[layer], norm_pe[layer], w_pe[layer], w_pg[layer])
        hp = ffn_pe(hp, p_prompt[layer], *wf)
        hs = ffn_pe(hs, p_sample[layer], *wf)
    return (hp, hs,
            jnp.stack(hg_p), jnp.stack(rh_p), jnp.stack(rc_p), jnp.stack(k_p), jnp.stack(v_p), jnp.stack(ki_p),
            jnp.stack(hg_s), jnp.stack(rh_s), jnp.stack(rc_s), jnp.stack(k_s), jnp.stack(v_s), jnp.stack(ki_s))
```

```python
import functools
import math

import jax
import jax.numpy as jnp
import numpy as np
from jax import lax
from jax.experimental import pallas as pl
from jax.experimental.pallas import tpu as pltpu

F32 = jnp.float32
BF16 = jnp.bfloat16

V7X_LANES = 128
V7X_SUBLANES = 8
V7X_VMEM_BYTES = 64 * 1024 * 1024
VMEM_LIMIT = 56 * 1024 * 1024

EPS = 1e-6
NEG_BIG = -1e30
RG_C = 8.0
ROPE_THETA = 500000.0
CONV_W = 4
TOPK_MAX = 256

HEAD = 128
ROPE_HALF = HEAD // 8

_NT = (((1,), (1,)), ((), ()))
_TN = (((0,), (0,)), ((), ()))


def _cparams(sem):
    return pltpu.CompilerParams(dimension_semantics=sem, vmem_limit_bytes=VMEM_LIMIT)


def _dot(a, b):
    return jnp.dot(a, b, preferred_element_type=F32)


def _dot_nt(a, b):
    return lax.dot_general(a, b, _NT, preferred_element_type=F32)


def _dot_tn(a, b):
    return lax.dot_general(a, b, _TN, preferred_element_type=F32)


def _rmsnorm_kernel(x_ref, g_ref, o_ref):
    x = x_ref[...]
    y = x * lax.rsqrt(jnp.mean(x * x, axis=-1, keepdims=True) + EPS)
    o_ref[...] = (y * g_ref[...]).astype(o_ref.dtype)


def rmsnorm_bf16(x, g):
    m, d = x.shape
    tm = min(m, 256)
    return pl.pallas_call(
        _rmsnorm_kernel,
        grid=(m // tm,),
        in_specs=[pl.BlockSpec((tm, d), lambda i: (i, 0)),
                  pl.BlockSpec((1, d), lambda i: (0, 0))],
        out_specs=pl.BlockSpec((tm, d), lambda i: (i, 0)),
        out_shape=jax.ShapeDtypeStruct((m, d), BF16),
        compiler_params=_cparams(("parallel",)),
        name="rmsnorm",
    )(x, g.reshape(1, d))


def _mm_kernel(*refs, nk, epilogue):
    x_ref, w_ref = refs[0], refs[1]
    pos = 2
    res_ref = p_ref = wpe_ref = None
    if epilogue in ("residual", "pe_gate"):
        res_ref = refs[pos]
        pos += 1
    if epilogue == "pe_gate":
        p_ref, wpe_ref = refs[pos], refs[pos + 1]
        pos += 2
    o_ref = refs[pos]
    acc_ref = refs[pos + 1] if nk > 1 else None

    def finish(acc):
        if epilogue == "none":
            out = acc
        elif epilogue == "relu2":
            r = jnp.maximum(acc, 0.0)
            out = r * r
        elif epilogue == "residual":
            out = res_ref[...] + acc
        else:
            pe = _dot(p_ref[...], wpe_ref[...])
            out = res_ref[...] + pe * jax.nn.sigmoid(acc)
        o_ref[...] = out.astype(o_ref.dtype)

    if nk == 1:
        finish(_dot(x_ref[...], w_ref[...]))
    else:
        k = pl.program_id(2)

        @pl.when(k == 0)
        def _():
            acc_ref[...] = jnp.zeros_like(acc_ref)

        acc_ref[...] += _dot(x_ref[...], w_ref[...])

        @pl.when(k == nk - 1)
        def _():
            finish(acc_ref[...])


def _mm_tiles(m, n, k):
    tm = min(m, 1024)
    if m <= 64:
        tn = min(n, 2048)
    else:
        tn = min(n, 512)
    while n % tn:
        tn //= 2
    tk = min(k, 4096)
    return tm, tn, tk


def matmul(x, w, *, epilogue="none", res=None, p=None, w_pe=None, out_dtype=F32):
    m, k = x.shape
    _, n = w.shape
    tm, tn, tk = _mm_tiles(m, n, k)
    nk = k // tk
    if nk == 1:
        grid = (m // tm, n // tn)
        xmap = lambda i, j: (i, 0)
        wmap = lambda i, j: (0, j)
        omap = lambda i, j: (i, j)
        pmap = lambda i, j: (i, 0)
        sem = ("parallel", "parallel")
    else:
        grid = (m // tm, n // tn, nk)
        xmap = lambda i, j, kk: (i, kk)
        wmap = lambda i, j, kk: (kk, j)
        omap = lambda i, j, kk: (i, j)
        pmap = lambda i, j, kk: (i, 0)
        sem = ("parallel", "parallel", "arbitrary")
    in_specs = [pl.BlockSpec((tm, tk), xmap), pl.BlockSpec((tk, tn), wmap)]
    args = [x, w]
    if epilogue in ("residual", "pe_gate"):
        in_specs.append(pl.BlockSpec((tm, tn), omap))
        args.append(res)
    if epilogue == "pe_gate":
        pe_dim = p.shape[1]
        in_specs.append(pl.BlockSpec((tm, pe_dim), pmap))
        in_specs.append(pl.BlockSpec((pe_dim, tn), wmap if nk == 1 else (lambda i, j, kk: (0, j))))
        args += [p, w_pe]
    scratch = [pltpu.VMEM((tm, tn), F32)] if nk > 1 else []
    return pl.pallas_call(
        functools.partial(_mm_kernel, nk=nk, epilogue=epilogue),
        grid=grid,
        in_specs=in_specs,
        out_specs=pl.BlockSpec((tm, tn), omap),
        out_shape=jax.ShapeDtypeStruct((m, n), out_dtype),
        scratch_shapes=scratch,
        compiler_params=_cparams(sem),
        name="mm_" + epilogue,
    )(*args)


def _hgrn_level_matrices(r):
    idx = np.arange(r)
    t = idx[:, None]
    u = idx[None, :]
    mats = []
    lvl = 2
    while lvl <= r:
        h = lvl // 2
        pos = t % lvl
        mid = t - pos + h
        upper = pos >= h
        m_up = (u >= mid) & (u <= t)
        m_lo = (u > t) & (u <= mid - 1)
        mats.append(np.where(upper, m_up, m_lo))
        lvl *= 2
    mats.append(u <= t)
    mats.append(u > t)
    return np.concatenate(mats, axis=0).astype(np.float32)


def _hgrn_kernel(q_ref, f_ref, i_ref, ga_ref, lbl_ref, an_ref, s0_ref, mall_ref,
                 o_ref, s_ref, st_ref, *, r, layer_j, t_valid):
    c = pl.program_id(2)
    nc = pl.num_programs(2)
    nlev = int(math.log2(r))

    @pl.when(c == 0)
    def _():
        st_ref[...] = s0_ref[0, 0].T

    lbl = lbl_ref[...]
    e = jnp.exp(lbl - jnp.max(lbl, axis=0, keepdims=True))
    soft = e / jnp.sum(e, axis=0, keepdims=True)
    lb = jnp.zeros((1, HEAD), F32)
    for jj in range(1, layer_j + 1):
        lb = lb + soft[jj:jj + 1, :]

    z = f_ref[...]
    logf = jnp.log1p(lb * jnp.exp(-z)) - _softplus(-z)
    kk = (1.0 - lb) * jax.nn.sigmoid(-z)
    qq = jax.nn.silu(q_ref[...])
    vv = i_ref[...]
    if t_valid is not None:
        live = lax.broadcasted_iota(jnp.int32, (r, 1), 0) < t_valid
        logf = jnp.where(live, logf, 0.0)
        kk = jnp.where(live, kk, 0.0)
        qq = jnp.where(live, qq, 0.0)
        vv = jnp.where(live, vv, 0.0)

    hi = logf.astype(BF16)
    mid = (logf - hi.astype(F32)).astype(BF16)
    xs = _dot(mall_ref[...], jnp.concatenate([hi, mid], axis=1))
    xs = xs[:, :HEAD] + xs[:, HEAD:]

    ti = lax.broadcasted_iota(jnp.int32, (r, r), 0)
    si = lax.broadcasted_iota(jnp.int32, (r, r), 1)
    rowi = lax.broadcasted_iota(jnp.int32, (r, 1), 0)
    qb = qq.astype(BF16)
    kb = kk.astype(BF16)
    att = jnp.where(ti == si, _dot_nt(qb, kb), 0.0)
    for lv in range(nlev):
        blk = 2 << lv
        h = blk // 2
        w = jnp.exp(xs[lv * r:(lv + 1) * r, :])
        up = (rowi & (blk - 1)) >= h
        qt = jnp.where(up, qq * w, 0.0).astype(BF16)
        kt = jnp.where(up, 0.0, kk * w).astype(BF16)
        pair = ((ti >> (lv + 1)) == (si >> (lv + 1))) & ((ti & (blk - 1)) >= h) & ((si & (blk - 1)) < h)
        att = att + jnp.where(pair, _dot_nt(qt, kt), 0.0)

    xc = xs[nlev * r:(nlev + 1) * r, :]
    xe = xs[(nlev + 1) * r:(nlev + 2) * r, :]
    st = st_ref[...]
    o = _dot(att.astype(BF16), vv.astype(BF16))
    o = o + _dot_nt((qq * jnp.exp(xc)).astype(BF16), st.astype(BF16))
    kt_end = (kk * jnp.exp(xe)).astype(BF16)
    g_end = jnp.exp(xc[r - 1:r, :])
    st_new = g_end * st + _dot_tn(vv.astype(BF16), kt_end)
    st_ref[...] = st_new

    on = o * lax.rsqrt(jnp.mean(o * o, axis=-1, keepdims=True) + EPS) * an_ref[...]
    o_ref[...] = (on * jax.nn.silu(ga_ref[...])).astype(o_ref.dtype)

    @pl.when(c == nc - 1)
    def _():
        s_ref[0, 0] = st_new.T


def hgrn(proj, lb_logits, a_norm, s0, *, bsz, t, layer_j, r, t_valid=None):
    width = a_norm.shape[0]
    nh = width // HEAD
    n_ab = lb_logits.shape[0]
    nchunk = t // r
    mall = jnp.asarray(_hgrn_level_matrices(r), BF16)
    col = lambda off: (lambda b, h, c: (b * nchunk + c, off * nh + h))
    kern = functools.partial(_hgrn_kernel, r=r, layer_j=layer_j, t_valid=t_valid)
    return pl.pallas_call(
        kern,
        grid=(bsz, nh, nchunk),
        in_specs=[pl.BlockSpec((r, HEAD), col(0)),
                  pl.BlockSpec((r, HEAD), col(1)),
                  pl.BlockSpec((r, HEAD), col(2)),
                  pl.BlockSpec((r, HEAD), col(3)),
                  pl.BlockSpec((n_ab, HEAD), lambda b, h, c: (0, h)),
                  pl.BlockSpec((1, HEAD), lambda b, h, c: (0, h)),
                  pl.BlockSpec((1, 1, HEAD, HEAD), lambda b, h, c: (b, h, 0, 0)),
                  pl.BlockSpec(mall.shape, lambda b, h, c: (0, 0))],
        out_specs=[pl.BlockSpec((r, HEAD), lambda b, h, c: (b * nchunk + c, h)),
                   pl.BlockSpec((1, 1, HEAD, HEAD), lambda b, h, c: (b, h, 0, 0))],
        out_shape=[jax.ShapeDtypeStruct((bsz * t, width), BF16),
                   jax.ShapeDtypeStruct((bsz, nh, HEAD, HEAD), F32)],
        scratch_shapes=[pltpu.VMEM((HEAD, HEAD), F32)],
        compiler_params=_cparams(("parallel", "parallel", "arbitrary")),
        name="hgrn",
    )(proj, proj, proj, proj, lb_logits, a_norm.reshape(1, width), s0, mall)


def _softplus(x):
    return jnp.maximum(x, 0.0) + jnp.log1p(jnp.exp(-jnp.abs(x)))


def _rglru_kernel(x_ref, gb_ref, c0_ref, h0_ref, cw_ref, cb_ref, wa_ref, ba_ref, wx_ref, bx_ref,
                  lam_ref, o_ref, hl_ref, cn_ref, xp_ref, a_ref, b_ref, hc_ref,
                  *, r, nblk, first_pos_zero, t_valid):
    i = pl.program_id(1)
    ni = pl.num_programs(1)
    pad = V7X_SUBLANES

    @pl.when(i == 0)
    def _():
        xp_ref[0:pad, :] = c0_ref[0]
        hc_ref[...] = h0_ref[0]

    x = x_ref[...]
    xp_ref[pad:pad + r, :] = x
    cw = cw_ref[...]
    y = cw[0:1, :] * xp_ref[pad - 3:pad - 3 + r, :]
    y = y + cw[1:2, :] * xp_ref[pad - 2:pad - 2 + r, :]
    y = y + cw[2:3, :] * xp_ref[pad - 1:pad - 1 + r, :]
    y = cb_ref[...] + (y + cw[3:4, :] * x)

    last = r if t_valid is None else t_valid
    cn_ref[0] = xp_ref[pad + last - 3:pad + last, :]
    xp_ref[0:pad, :] = xp_ref[r:r + pad, :]

    ra = []
    rx = []
    for n in range(nblk):
        ys = y[:, n * HEAD:(n + 1) * HEAD].astype(BF16)
        ra.append(_dot(ys, wa_ref[n]))
        rx.append(_dot(ys, wx_ref[n]))
    rg = jax.nn.sigmoid(jnp.concatenate(ra, axis=1) + ba_ref[...])
    gi = jax.nn.sigmoid(jnp.concatenate(rx, axis=1) + bx_ref[...])
    log_a = (-RG_C * rg) * _softplus(-lam_ref[...])
    a = jnp.exp(log_a)
    th = jnp.tanh(log_a)
    mult = jnp.sqrt((-2.0 * th) / (1.0 - th))
    rowi = lax.broadcasted_iota(jnp.int32, (r, 1), 0)
    if first_pos_zero:
        mult = jnp.where((rowi == 0) & (i == 0), 1.0, mult)
    bterm = mult * gi * y
    if t_valid is not None:
        live = rowi < t_valid
        a = jnp.where(live, a, 1.0)
        bterm = jnp.where(live, bterm, 0.0)
    a_ref[...] = a
    b_ref[...] = bterm

    def step(tt, h):
        h = a_ref[pl.ds(tt, 1), :] * h + b_ref[pl.ds(tt, 1), :]
        b_ref[pl.ds(tt, 1), :] = h
        return h

    h_last = lax.fori_loop(0, r, step, hc_ref[...], unroll=8)
    hc_ref[...] = h_last
    o_ref[...] = (jax.nn.gelu(gb_ref[...]) * b_ref[...]).astype(o_ref.dtype)

    @pl.when(i == ni - 1)
    def _():
        hl_ref[0] = h_last


def rglru(proj, conv0, h0, conv_w, conv_b, wa, ba, wx, bx, lam, *, bsz, t, r, gb_blk, x_blk,
          first_pos_zero, t_valid=None):
    w = lam.shape[0]
    nblk = wa.shape[0]
    nrb = t // r
    pad = V7X_SUBLANES
    c0 = jnp.pad(conv0, ((0, 0), (pad - (CONV_W - 1), 0), (0, 0)))
    row = lambda v: v.reshape(1, w)
    kern = functools.partial(_rglru_kernel, r=r, nblk=nblk, first_pos_zero=first_pos_zero, t_valid=t_valid)
    full = lambda shape: pl.BlockSpec(shape, lambda b, i: (0,) * len(shape))
    return pl.pallas_call(
        kern,
        grid=(bsz, nrb),
        in_specs=[pl.BlockSpec((r, w), lambda b, i: (b * nrb + i, x_blk)),
                  pl.BlockSpec((r, w), lambda b, i: (b * nrb + i, gb_blk)),
                  pl.BlockSpec((1, pad, w), lambda b, i: (b, 0, 0)),
                  pl.BlockSpec((1, 1, w), lambda b, i: (b, 0, 0)),
                  full((CONV_W, w)), full((1, w)),
                  full(wa.shape), full((1, w)), full(wx.shape), full((1, w)), full((1, w))],
        out_specs=[pl.BlockSpec((r, w), lambda b, i: (b * nrb + i, 0)),
                   pl.BlockSpec((1, 1, w), lambda b, i: (b, 0, 0)),
                   pl.BlockSpec((1, CONV_W - 1, w), lambda b, i: (b, 0, 0))],
        out_shape=[jax.ShapeDtypeStruct((bsz * t, w), BF16),
                   jax.ShapeDtypeStruct((bsz, 1, w), F32),
                   jax.ShapeDtypeStruct((bsz, CONV_W - 1, w), F32)],
        scratch_shapes=[pltpu.VMEM((r + pad, w), F32), pltpu.VMEM((r, w), F32),
                        pltpu.VMEM((r, w), F32), pltpu.VMEM((1, w), F32)],
        compiler_params=_cparams(("parallel", "arbitrary")),
        name="rglru",
    )(proj, proj, c0, h0.reshape(bsz, 1, w), conv_w, row(conv_b), wa.astype(BF16), row(ba),
      wx.astype(BF16), row(bx), row(lam))


def _rope_tables(pos):
    rd = HEAD // 4
    half = rd // 2
    inv = jnp.exp(-math.log(ROPE_THETA) * jnp.arange(half, dtype=F32) * (2.0 / rd))
    ang = pos.astype(F32)[:, None] * inv[None, :]
    cos, sin = jnp.cos(ang), jnp.sin(ang)
    n = pos.shape[0]
    ones = jnp.ones((n, HEAD - rd), F32)
    zeros = jnp.zeros((n, HEAD - rd), F32)
    zh = jnp.zeros((n, half), F32)
    c = jnp.concatenate([cos, cos, ones], axis=1)
    a = jnp.concatenate([-sin, zh, zeros], axis=1)
    b = jnp.concatenate([zh, sin, zeros], axis=1)
    return c, a, b


def _rope(x, c, a, b):
    return x * c + pltpu.roll(x, HEAD - ROPE_HALF, 1) * a + pltpu.roll(x, ROPE_HALF, 1) * b


def _dsa_post_kernel(pm_ref, pt_ref, c_ref, a_ref, b_ref, qn_ref, kn_ref, kg_ref, kb_ref,
                     q_ref, qi_ref, k_ref, v_ref, ki_ref, wi_ref, kbf_ref, vbf_ref, kibf_ref,
                     *, nq, nkv, nidx, wi_scale):
    c, a, b = c_ref[...], a_ref[...], b_ref[...]
    qn, kn = qn_ref[...], kn_ref[...]
    off = 0
    for h in range(nq):
        x = pm_ref[:, off + h * HEAD:off + (h + 1) * HEAD]
        y = x * lax.rsqrt(jnp.mean(x * x, axis=-1, keepdims=True) + EPS) * qn
        q_ref[:, h * HEAD:(h + 1) * HEAD] = _rope(y, c, a, b).astype(q_ref.dtype)
    off += nq * HEAD
    for h in range(nkv):
        x = pm_ref[:, off + h * HEAD:off + (h + 1) * HEAD]
        y = x * lax.rsqrt(jnp.mean(x * x, axis=-1, keepdims=True) + EPS) * kn
        kr = _rope(y, c, a, b)
        k_ref[:, h * HEAD:(h + 1) * HEAD] = kr
        kbf_ref[:, h * HEAD:(h + 1) * HEAD] = kr.astype(BF16)
    off += nkv * HEAD
    vv = pm_ref[:, off:off + nkv * HEAD]
    v_ref[...] = vv
    vbf_ref[...] = vv.astype(BF16)
    off += nkv * HEAD
    for h in range(nidx):
        x = pm_ref[:, off + h * HEAD:off + (h + 1) * HEAD]
        qi_ref[:, h * HEAD:(h + 1) * HEAD] = _rope(x, c, a, b).astype(qi_ref.dtype)
    x = pt_ref[:, 0:HEAD]
    mu = jnp.mean(x, axis=-1, keepdims=True)
    xc = x - mu
    y = xc * lax.rsqrt(jnp.mean(xc * xc, axis=-1, keepdims=True) + EPS)
    kir = _rope(y * kg_ref[...] + kb_ref[...], c, a, b)
    ki_ref[...] = kir
    kibf_ref[...] = kir.astype(BF16)
    wi_ref[...] = pt_ref[:, HEAD:HEAD + nidx] * wi_scale


def dsa_post(proj_main, proj_tail, tables, qn, kn, kg, kb, *, nq, nkv, nidx, t_tab):
    m = proj_main.shape[0]
    tm = min(m, 128, t_tab)
    ntab = t_tab // tm
    wm = proj_main.shape[1]
    wt = proj_tail.shape[1]
    rowspec = lambda wdt: pl.BlockSpec((tm, wdt), lambda i: (i, 0))
    tabspec = pl.BlockSpec((tm, HEAD), lambda i: (i % ntab, 0))
    vec = pl.BlockSpec((1, HEAD), lambda i: (0, 0))
    kern = functools.partial(_dsa_post_kernel, nq=nq, nkv=nkv, nidx=nidx,
                             wi_scale=float((nidx * HEAD) ** -0.5))
    outs = [(nq * HEAD, BF16), (nidx * HEAD, BF16), (nkv * HEAD, F32), (nkv * HEAD, F32),
            (HEAD, F32), (nidx, F32), (nkv * HEAD, BF16), (nkv * HEAD, BF16), (HEAD, BF16)]
    return pl.pallas_call(
        kern,
        grid=(m // tm,),
        in_specs=[rowspec(wm), rowspec(wt), tabspec, tabspec, tabspec, vec, vec, vec, vec],
        out_specs=[rowspec(wdt) for wdt, _ in outs],
        out_shape=[jax.ShapeDtypeStruct((m, wdt), dt) for wdt, dt in outs],
        compiler_params=_cparams(("parallel",)),
        name="dsa_post",
    )(proj_main, proj_tail, *tables, qn.reshape(1, HEAD), kn.reshape(1, HEAD),
      kg.reshape(1, HEAD), kb.reshape(1, HEAD))


def _sort_key(x):
    bits = pltpu.bitcast(x, jnp.int32)
    return jnp.where(bits < 0, bits ^ jnp.int32(0x7FFFFFFF), bits)


def _kth_largest_key(count_ge, rows, k):
    def body(it, thr):
        bit = lax.shift_left(jnp.int32(1), jnp.int32(31) - it)
        trial = thr + bit
        return jnp.where(count_ge(trial) >= k, trial, thr)

    init = jnp.full((rows, 1), jnp.iinfo(jnp.int32).min, jnp.int32)
    return lax.fori_loop(0, 32, body, init)


def _dsa_prompt_kernel(qi_ref, wi_ref, kib_ref, q_ref, kb_ref, vb_ref, o_ref, key_ref, sel_ref,
                       *, tq, nidx, nq, nkv, topk, scale):
    i = pl.program_id(1)
    ntile = i + 1
    rowi = lax.broadcasted_iota(jnp.int32, (tq, tq), 0)
    coli = lax.broadcasted_iota(jnp.int32, (tq, tq), 1)
    tri = coli <= rowi

    def idx_tile(j, carry):
        kt = kib_ref[pl.ds(pl.multiple_of(j * tq, tq), tq), :]
        acc = jnp.zeros((tq, tq), F32)
        for h in range(nidx):
            d = _dot_nt(qi_ref[:, h * HEAD:(h + 1) * HEAD], kt)
            acc = acc + jnp.maximum(d, 0.0) * wi_ref[:, h:h + 1]
        acc = jnp.where(tri | (j < i), acc, NEG_BIG)
        key_ref[j] = _sort_key(acc)
        return carry

    lax.fori_loop(0, ntile, idx_tile, 0)

    def count_ge(trial):
        def tile(j, cnt):
            return cnt + jnp.where(key_ref[j] >= trial, 1.0, 0.0)

        cnt = lax.fori_loop(0, ntile, tile, jnp.zeros((tq, tq), F32))
        return jnp.sum(cnt, axis=1, keepdims=True)

    thr = _kth_largest_key(count_ge, tq, float(topk))

    def sel_tile(j, carry):
        keep = (key_ref[j] >= thr) & (tri | (j < i))
        sel_ref[j] = jnp.where(keep, 1.0, 0.0)
        return carry

    lax.fori_loop(0, ntile, sel_tile, 0)

    grp = nq // nkv
    for n in range(nkv):
        for g in range(grp):
            h = n * grp + g
            qh = q_ref[:, h * HEAD:(h + 1) * HEAD]

            def att_tile(j, carry, n=n, qh=qh):
                m_i, l_i, acc = carry
                r0 = pl.multiple_of(j * tq, tq)
                kt = kb_ref[pl.ds(r0, tq), n * HEAD:(n + 1) * HEAD]
                vt = vb_ref[pl.ds(r0, tq), n * HEAD:(n + 1) * HEAD]
                keep = sel_ref[j] > 0.0
                s = jnp.where(keep, _dot_nt(qh, kt) * scale, NEG_BIG)
                m_new = jnp.maximum(m_i, jnp.max(s, axis=1, keepdims=True))
                p = jnp.where(keep, jnp.exp(s - m_new), 0.0)
                alpha = jnp.exp(m_i - m_new)
                l_new = alpha * l_i + jnp.sum(p, axis=1, keepdims=True)
                acc = alpha * acc + _dot(p.astype(BF16), vt)
                return m_new, l_new, acc

            init = (jnp.full((tq, 1), NEG_BIG, F32), jnp.zeros((tq, 1), F32), jnp.zeros((tq, HEAD), F32))
            _, l_f, acc_f = lax.fori_loop(0, ntile, att_tile, init)
            o_ref[:, h * HEAD:(h + 1) * HEAD] = (acc_f / l_f).astype(o_ref.dtype)


def dsa_prompt_attend(q_bf, qi_bf, wi, ki_bf, k_bf, v_bf, *, bsz, t, nq, nkv, nidx):
    tq = min(256, t)
    nqb = t // tq
    topk = min(TOPK_MAX, t // 4)
    assert topk <= tq
    qrow = lambda wdt: pl.BlockSpec((tq, wdt), lambda b, i: (b * nqb + i, 0))
    brow = lambda wdt: pl.BlockSpec((t, wdt), lambda b, i: (b, 0))
    kern = functools.partial(_dsa_prompt_kernel, tq=tq, nidx=nidx, nq=nq, nkv=nkv, topk=topk,
                             scale=float(HEAD ** -0.5))
    return pl.pallas_call(
        kern,
        grid=(bsz, nqb),
        in_specs=[qrow(nidx * HEAD), qrow(nidx), brow(HEAD), qrow(nq * HEAD),
                  brow(nkv * HEAD), brow(nkv * HEAD)],
        out_specs=qrow(nq * HEAD),
        out_shape=jax.ShapeDtypeStruct((bsz * t, nq * HEAD), BF16),
        scratch_shapes=[pltpu.VMEM((nqb, tq, tq), jnp.int32), pltpu.VMEM((nqb, tq, tq), F32)],
        compiler_params=_cparams(("parallel", "arbitrary")),
        name="dsa_prompt",
    )(qi_bf, wi, ki_bf, q_bf, k_bf, v_bf)


def _dsa_s_scores_kernel(pt_ref, qi_ref, wi_ref, page_ref, new_ref, sc_ref, *, ts, nidx, npages):
    p = pl.program_id(1)
    is_new = p == npages
    kp = jnp.where(is_new, new_ref[0], page_ref[0]).astype(BF16)
    d = _dot_nt(qi_ref[...], kp)
    r = jnp.maximum(d, 0.0) * wi_ref[...]
    psz = r.shape[1]
    rows = [jnp.sum(r[tt * nidx:(tt + 1) * nidx, :], axis=0, keepdims=True) for tt in range(ts)]
    sc = jnp.concatenate(rows + [jnp.full((V7X_SUBLANES - ts, psz), NEG_BIG, F32)], axis=0)
    rowi = lax.broadcasted_iota(jnp.int32, sc.shape, 0)
    coli = lax.broadcasted_iota(jnp.int32, sc.shape, 1)
    sc_ref[0] = jnp.where(is_new & (coli > rowi), NEG_BIG, sc)


def _dsa_s_select_kernel(sc_ref, sel_ref, *, topk, ts, past):
    key = _sort_key(sc_ref[...])
    rows = key.shape[0]

    def count_ge(trial):
        return jnp.sum(jnp.where(key >= trial, 1.0, 0.0), axis=1, keepdims=True)

    thr = _kth_largest_key(count_ge, rows, float(topk))
    rowi = lax.broadcasted_iota(jnp.int32, key.shape, 0)
    coli = lax.broadcasted_iota(jnp.int32, key.shape, 1)
    visible = coli <= past + (rowi % V7X_SUBLANES)
    sel_ref[...] = jnp.where((key >= thr) & visible, 1.0, 0.0)


def _dsa_s_attend_kernel(pt_ref, q_ref, sel_ref, kp_ref, vp_ref, kn_ref, vn_ref, o_ref,
                         m_ref, l_ref, acc_ref, *, nkv, npages, scale):
    p = pl.program_id(1)
    is_new = p == npages

    @pl.when(p == 0)
    def _():
        m_ref[...] = jnp.full_like(m_ref, NEG_BIG)
        l_ref[...] = jnp.zeros_like(l_ref)
        acc_ref[...] = jnp.zeros_like(acc_ref)

    kpage = jnp.where(is_new, kn_ref[0], kp_ref[0]).astype(BF16)
    vpage = jnp.where(is_new, vn_ref[0], vp_ref[0]).astype(BF16)
    keep = sel_ref[0] > 0.0
    for n in range(nkv):
        s = _dot_nt(q_ref[0, n], kpage[:, n * HEAD:(n + 1) * HEAD]) * scale
        s = jnp.where(keep, s, NEG_BIG)
        m_i = m_ref[n]
        m_new = jnp.maximum(m_i, jnp.max(s, axis=1, keepdims=True))
        pr = jnp.where(keep, jnp.exp(s - m_new), 0.0)
        alpha = jnp.exp(m_i - m_new)
        l_ref[n] = alpha * l_ref[n] + jnp.sum(pr, axis=1, keepdims=True)
        acc_ref[n] = alpha * acc_ref[n] + _dot(pr.astype(BF16), vpage[:, n * HEAD:(n + 1) * HEAD])
        m_ref[n] = m_new

    @pl.when(is_new)
    def _():
        o_ref[0] = acc_ref[...] / l_ref[...]


def dsa_sample_attend(q_bf, qi_bf, wi, ki_new, k_new, v_new, ck, cv, cki, page_table,
                      *, bsz, ts, nq, nkv, nidx):
    n_pool, psz = ck.shape[0], ck.shape[1]
    npages = page_table.shape[1]
    past = npages * psz
    topk = min(TOPK_MAX, (past + ts) // 4)
    grp = nq // nkv
    ncol = (npages + 1) * psz
    sub = V7X_SUBLANES
    padrows = lambda x: jnp.pad(x.reshape(bsz, ts, -1), ((0, 0), (0, psz - ts), (0, 0)))
    page_idx = lambda b, p, pt: (pt[b, jnp.minimum(p, npages - 1)], 0, 0)
    per_b = lambda b, p, pt: (b, 0, 0)

    scores = pl.pallas_call(
        functools.partial(_dsa_s_scores_kernel, ts=ts, nidx=nidx, npages=npages),
        grid_spec=pltpu.PrefetchScalarGridSpec(
            num_scalar_prefetch=1,
            grid=(bsz, npages + 1),
            in_specs=[pl.BlockSpec((ts * nidx, HEAD), lambda b, p, pt: (b, 0)),
                      pl.BlockSpec((ts * nidx, 1), lambda b, p, pt: (b, 0)),
                      pl.BlockSpec((1, psz, HEAD), page_idx),
                      pl.BlockSpec((1, psz, HEAD), per_b)],
            out_specs=pl.BlockSpec((1, sub, psz), lambda b, p, pt: (b, 0, p))),
        out_shape=jax.ShapeDtypeStruct((bsz, sub, ncol), F32),
        compiler_params=_cparams(("parallel", "arbitrary")),
        name="dsa_s_scores",
    )(page_table, qi_bf.reshape(bsz * ts * nidx, HEAD), wi.reshape(bsz * ts * nidx, 1), cki,
      padrows(ki_new))

    sel = pl.pallas_call(
        functools.partial(_dsa_s_select_kernel, topk=topk, ts=ts, past=past),
        out_shape=jax.ShapeDtypeStruct((bsz * sub, ncol), F32),
        compiler_params=pltpu.CompilerParams(vmem_limit_bytes=VMEM_LIMIT),
        name="dsa_s_select",
    )(scores.reshape(bsz * sub, ncol))
    sel = jnp.repeat(sel.reshape(bsz, sub, ncol)[:, :ts], grp, axis=1)

    q5 = q_bf.reshape(bsz, ts, nkv, grp, HEAD).transpose(0, 2, 1, 3, 4).reshape(bsz, nkv, ts * grp, HEAD)
    kv_idx = lambda b, p, pt: (pt[b, jnp.minimum(p, npages - 1)], 0, 0)
    o = pl.pallas_call(
        functools.partial(_dsa_s_attend_kernel, nkv=nkv, npages=npages, scale=float(HEAD ** -0.5)),
        grid_spec=pltpu.PrefetchScalarGridSpec(
            num_scalar_prefetch=1,
            grid=(bsz, npages + 1),
            in_specs=[pl.BlockSpec((1, nkv, ts * grp, HEAD), lambda b, p, pt: (b, 0, 0, 0)),
                      pl.BlockSpec((1, ts * grp, psz), lambda b, p, pt: (b, 0, p)),
                      pl.BlockSpec((1, psz, nkv * HEAD), kv_idx),
                      pl.BlockSpec((1, psz, nkv * HEAD), kv_idx),
                      pl.BlockSpec((1, psz, nkv * HEAD), per_b),
                      pl.BlockSpec((1, psz, nkv * HEAD), per_b)],
            out_specs=pl.BlockSpec((1, nkv, ts * grp, HEAD), lambda b, p, pt: (b, 0, 0, 0)),
            scratch_shapes=[pltpu.VMEM((nkv, ts * grp, 1), F32), pltpu.VMEM((nkv, ts * grp, 1), F32),
                            pltpu.VMEM((nkv, ts * grp, HEAD), F32)]),
        out_shape=jax.ShapeDtypeStruct((bsz, nkv, ts * grp, HEAD), F32),
        compiler_params=_cparams(("parallel", "arbitrary")),
        name="dsa_s_attend",
    )(page_table, q5, sel, ck.reshape(n_pool, psz, nkv * HEAD), cv.reshape(n_pool, psz, nkv * HEAD),
      padrows(k_new), padrows(v_new))
    o = o.reshape(bsz, nkv, ts, grp, HEAD).transpose(0, 2, 1, 3, 4).reshape(bsz * ts, nq * HEAD)
    return o.astype(BF16)


def kernel(x_prompt, x_sample, state_hgrn, state_rglru_h, state_rglru_conv, cache_k, cache_v, cache_kidx,
           page_table, p_prompt, p_sample, norm_mix, norm_mlp, norm_pe, w_in_ab, w_out_ab, hgrn_lb_logits,
           hgrn_out_norm, rg_conv_w, rg_conv_b, rg_wa, rg_ba, rg_wx, rg_bx, rg_lambda, w_in_c, w_out_c,
           c_q_norm, c_k_norm, idx_k_ln_g, idx_k_ln_b, w_up, w_down, w_pe, w_pg):
    bp, tp, d = x_prompt.shape
    bs, ts, _ = x_sample.shape
    depth = norm_mix.shape[0]
    a_width = hgrn_out_norm.shape[1]
    b_width = rg_lambda.shape[1]
    nh_a = a_width // HEAD
    nkv = cache_k.shape[3]
    nq = w_out_c.shape[1] // HEAD
    nidx = (w_in_c.shape[2] - nq * HEAD - 2 * nkv * HEAD - HEAD) // (HEAD + 1)
    past = page_table.shape[1] * cache_k.shape[2]
    c_main = nq * HEAD + 2 * nkv * HEAD + nidx * HEAD
    pe_dim = p_prompt.shape[-1]

    hp = x_prompt.reshape(bp * tp, d)
    hs = x_sample.reshape(bs * ts, d)
    tab_p = _rope_tables(jnp.arange(tp, dtype=jnp.int32))
    tab_s = _rope_tables(past + jnp.arange(ts, dtype=jnp.int32))
    tab_s = tuple(jnp.tile(tb, (bs, 1)) for tb in tab_s)
    ts_pad = 16
    pad_s = lambda x: jnp.pad(x.reshape(bs, ts, -1), ((0, 0), (0, ts_pad - ts), (0, 0))).reshape(bs * ts_pad, -1)
    unpad_s = lambda x: x.reshape(bs, ts_pad, -1)[:, :ts].reshape(bs * ts, -1)

    outs = {k: [] for k in ("hg_p", "rh_p", "rc_p", "k_p", "v_p", "ki_p",
                            "hg_s", "rh_s", "rc_s", "k_s", "v_s", "ki_s")}
    for layer in range(depth):
        j = layer // 2
        xpn = rmsnorm_bf16(hp, norm_mix[layer])
        xsn = rmsnorm_bf16(hs, norm_mix[layer])
        if layer % 2 == 0:
            w_in = w_in_ab[j].astype(BF16)
            w_out = w_out_ab[j].astype(BF16)
            gb_blk = 4 * a_width // b_width
            x_blk = gb_blk + 1
            rg_args = (rg_conv_w[j], rg_conv_b[j], rg_wa[j], rg_ba[j], rg_wx[j], rg_bx[j], rg_lambda[j])

            proj = matmul(xpn, w_in)
            oa, sa = hgrn(proj, hgrn_lb_logits, hgrn_out_norm[j], jnp.zeros((bp, nh_a, HEAD, HEAD), F32),
                          bsz=bp, t=tp, layer_j=j, r=min(128, tp))
            ob, hl, cn = rglru(proj, jnp.zeros((bp, CONV_W - 1, b_width), F32), jnp.zeros((bp, b_width), F32),
                               *rg_args, bsz=bp, t=tp, r=min(256, tp), gb_blk=gb_blk, x_blk=x_blk,
                               first_pos_zero=True)
            hp = matmul(jnp.concatenate([oa, ob], axis=1), w_out, epilogue="residual", res=hp)
            outs["hg_p"].append(sa)
            outs["rh_p"].append(hl.reshape(bp, b_width))
            outs["rc_p"].append(cn)

            proj = pad_s(matmul(xsn, w_in))
            oa, sa = hgrn(proj, hgrn_lb_logits, hgrn_out_norm[j], state_hgrn[j],
                          bsz=bs, t=ts_pad, layer_j=j, r=ts_pad, t_valid=ts)
            ob, hl, cn = rglru(proj, state_rglru_conv[j], state_rglru_h[j], *rg_args, bsz=bs, t=ts_pad,
                               r=ts_pad, gb_blk=gb_blk, x_blk=x_blk, first_pos_zero=False, t_valid=ts)
            mix = unpad_s(jnp.concatenate([oa, ob], axis=1))
            hs = matmul(mix, w_out, epilogue="residual", res=hs)
            outs["hg_s"].append(sa)
            outs["rh_s"].append(hl.reshape(bs, b_width))
            outs["rc_s"].append(cn)
        else:
            w_in = w_in_c[j].astype(BF16)
            w_main = w_in[:, :c_main]
            w_tail = jnp.pad(w_in[:, c_main:], ((0, 0), (0, 2 * HEAD - (HEAD + nidx))))
            w_out = w_out_c[j].astype(BF16)
            norms = (c_q_norm[j], c_k_norm[j], idx_k_ln_g[j], idx_k_ln_b[j])

            post = dsa_post(matmul(xpn, w_main), matmul(xpn, w_tail), tab_p, *norms,
                            nq=nq, nkv=nkv, nidx=nidx, t_tab=tp)
            q_bf, qi_bf, kk, vv, ki, wi, k_bf, v_bf, ki_bf = post
            o = dsa_prompt_attend(q_bf, qi_bf, wi, ki_bf, k_bf, v_bf, bsz=bp, t=tp, nq=nq, nkv=nkv, nidx=nidx)
            hp = matmul(o, w_out, epilogue="residual", res=hp)
            outs["k_p"].append(kk.reshape(bp, tp, nkv, HEAD))
            outs["v_p"].append(vv.reshape(bp, tp, nkv, HEAD))
            outs["ki_p"].append(ki.reshape(bp, tp, HEAD))

            post = dsa_post(matmul(xsn, w_main), matmul(xsn, w_tail), tab_s, *norms,
                            nq=nq, nkv=nkv, nidx=nidx, t_tab=bs * ts)
            q_bf, qi_bf, kk, vv, ki, wi, _, _, _ = post
            o = dsa_sample_attend(q_bf, qi_bf, wi, ki, kk, vv, cache_k[j], cache_v[j], cache_kidx[j],
                                  page_table, bsz=bs, ts=ts, nq=nq, nkv=nkv, nidx=nidx)
            hs = matmul(o, w_out, epilogue="residual", res=hs)
            outs["k_s"].append(kk.reshape(bs, ts, nkv, HEAD))
            outs["v_s"].append(vv.reshape(bs, ts, nkv, HEAD))
            outs["ki_s"].append(ki.reshape(bs, ts, HEAD))

        wu, wd = w_up[layer].astype(BF16), w_down[layer].astype(BF16)
        wpe, wpg = w_pe[layer].astype(BF16), w_pg[layer].astype(BF16)
        for grp_name in ("p", "s"):
            h = hp if grp_name == "p" else hs
            pl_in = (p_prompt if grp_name == "p" else p_sample)[layer].reshape(-1, pe_dim).astype(BF16)
            mid = matmul(rmsnorm_bf16(h, norm_mlp[layer]), wu, epilogue="relu2", out_dtype=BF16)
            h = matmul(mid, wd, epilogue="residual", res=h)
            h = matmul(rmsnorm_bf16(h, norm_pe[layer]), wpg, epilogue="pe_gate", res=h, p=pl_in, w_pe=wpe)
            if grp_name == "p":
                hp = h
            else:
                hs = h

    st = lambda name: jnp.stack(outs[name])
    return (hp.reshape(bp, tp, d), hs.reshape(bs, ts, d),
            st("hg_p"), st("rh_p"), st("rc_p"), st("k_p"), st("v_p"), st("ki_p"),
            st("hg_s"), st("rh_s"), st("rc_s"), st("k_s"), st("v_s"), st("ki_s"))
```

```python
import functools
import math

import jax
import jax.numpy as jnp
import numpy as np
from jax import lax
from jax.experimental import pallas as pl
from jax.experimental.pallas import tpu as pltpu

F32 = jnp.float32
BF16 = jnp.bfloat16

V7X_LANES = 128
V7X_SUBLANES = 8
V7X_VMEM_BYTES = 64 * 1024 * 1024
VMEM_LIMIT = 56 * 1024 * 1024

EPS = 1e-6
NEG_BIG = -1e30
RG_C = 8.0
ROPE_THETA = 500000.0
CONV_W = 4
TOPK_MAX = 256

HEAD = 128
ROPE_HALF = HEAD // 8
DSA_TQ = 256
HGRN_HEADS_PER_STEP = 4

_NT = (((1,), (1,)), ((), ()))
_TN = (((0,), (0,)), ((), ()))


def _cparams(sem):
    return pltpu.CompilerParams(dimension_semantics=sem, vmem_limit_bytes=VMEM_LIMIT)


def _dot(a, b):
    return jnp.dot(a, b, preferred_element_type=F32)


def _dot_nt(a, b):
    return lax.dot_general(a, b, _NT, preferred_element_type=F32)


def _dot_tn(a, b):
    return lax.dot_general(a, b, _TN, preferred_element_type=F32)


def _rmsnorm_kernel(x_ref, g_ref, o_ref):
    x = x_ref[...]
    y = x * lax.rsqrt(jnp.mean(x * x, axis=-1, keepdims=True) + EPS)
    o_ref[...] = (y * g_ref[...]).astype(o_ref.dtype)


def rmsnorm_bf16(x, g):
    m, d = x.shape
    tm = min(m, 256)
    return pl.pallas_call(
        _rmsnorm_kernel,
        grid=(m // tm,),
        in_specs=[pl.BlockSpec((tm, d), lambda i: (i, 0)),
                  pl.BlockSpec((1, d), lambda i: (0, 0))],
        out_specs=pl.BlockSpec((tm, d), lambda i: (i, 0)),
        out_shape=jax.ShapeDtypeStruct((m, d), BF16),
        compiler_params=_cparams(("parallel",)),
        name="rmsnorm",
    )(x, g.reshape(1, d))


def _mm_kernel(*refs, nk, epilogue):
    x_ref, w_ref = refs[0], refs[1]
    pos = 2
    res_ref = p_ref = wpe_ref = None
    if epilogue in ("residual", "pe_gate"):
        res_ref = refs[pos]
        pos += 1
    if epilogue == "pe_gate":
        p_ref, wpe_ref = refs[pos], refs[pos + 1]
        pos += 2
    o_ref = refs[pos]
    acc_ref = refs[pos + 1] if nk > 1 else None

    def finish(acc):
        if epilogue == "none":
            out = acc
        elif epilogue == "relu2":
            r = jnp.maximum(acc, 0.0)
            out = r * r
        elif epilogue == "residual":
            out = res_ref[...] + acc
        else:
            pe = _dot(p_ref[...], wpe_ref[...])
            out = res_ref[...] + pe * jax.nn.sigmoid(acc)
        o_ref[...] = out.astype(o_ref.dtype)

    if nk == 1:
        finish(_dot(x_ref[...], w_ref[...]))
    else:
        k = pl.program_id(2)

        @pl.when(k == 0)
        def _():
            acc_ref[...] = jnp.zeros_like(acc_ref)

        acc_ref[...] += _dot(x_ref[...], w_ref[...])

        @pl.when(k == nk - 1)
        def _():
            finish(acc_ref[...])


def _mm_tiles(m, n, k):
    tm = min(m, 1024)
    if m <= 64:
        tn = min(n, 2048)
    else:
        tn = min(n, 512)
    while n % tn:
        tn //= 2
    tk = min(k, 4096)
    return tm, tn, tk


def matmul(x, w, *, epilogue="none", res=None, p=None, w_pe=None, out_dtype=F32):
    m, k = x.shape
    _, n = w.shape
    tm, tn, tk = _mm_tiles(m, n, k)
    nk = k // tk
    if nk == 1:
        grid = (m // tm, n // tn)
        xmap = lambda i, j: (i, 0)
        wmap = lambda i, j: (0, j)
        omap = lambda i, j: (i, j)
        pmap = lambda i, j: (i, 0)
        sem = ("parallel", "parallel")
    else:
        grid = (m // tm, n // tn, nk)
        xmap = lambda i, j, kk: (i, kk)
        wmap = lambda i, j, kk: (kk, j)
        omap = lambda i, j, kk: (i, j)
        pmap = lambda i, j, kk: (i, 0)
        sem = ("parallel", "parallel", "arbitrary")
    in_specs = [pl.BlockSpec((tm, tk), xmap), pl.BlockSpec((tk, tn), wmap)]
    args = [x, w]
    if epilogue in ("residual", "pe_gate"):
        in_specs.append(pl.BlockSpec((tm, tn), omap))
        args.append(res)
    if epilogue == "pe_gate":
        pe_dim = p.shape[1]
        in_specs.append(pl.BlockSpec((tm, pe_dim), pmap))
        in_specs.append(pl.BlockSpec((pe_dim, tn), wmap if nk == 1 else (lambda i, j, kk: (0, j))))
        args += [p, w_pe]
    scratch = [pltpu.VMEM((tm, tn), F32)] if nk > 1 else []
    return pl.pallas_call(
        functools.partial(_mm_kernel, nk=nk, epilogue=epilogue),
        grid=grid,
        in_specs=in_specs,
        out_specs=pl.BlockSpec((tm, tn), omap),
        out_shape=jax.ShapeDtypeStruct((m, n), out_dtype),
        scratch_shapes=scratch,
        compiler_params=_cparams(sem),
        name="mm_" + epilogue,
    )(*args)


def _mm_rw_kernel(*refs, epilogue):
    x_ref, w_ref = refs[0], refs[1]
    pos = 2
    res_ref = p_ref = wpe_ref = wpeb_ref = None
    if epilogue in ("residual", "pe_gate"):
        res_ref = refs[pos]
        pos += 1
    if epilogue == "pe_gate":
        p_ref, wpe_ref = refs[pos], refs[pos + 1]
        pos += 2
    o_ref, wb_ref = refs[pos], refs[pos + 1]
    if epilogue == "pe_gate":
        wpeb_ref = refs[pos + 2]

    @pl.when(pl.program_id(1) == 0)
    def _():
        wb_ref[...] = w_ref[...].astype(BF16)
        if epilogue == "pe_gate":
            wpeb_ref[...] = wpe_ref[...].astype(BF16)

    acc = _dot(x_ref[...], wb_ref[...])
    if epilogue == "none":
        out = acc
    elif epilogue == "relu2":
        r = jnp.maximum(acc, 0.0)
        out = r * r
    elif epilogue == "residual":
        out = res_ref[...] + acc
    else:
        out = res_ref[...] + _dot(p_ref[...], wpeb_ref[...]) * jax.nn.sigmoid(acc)
    o_ref[...] = out.astype(o_ref.dtype)


def matmul_rw(x, w_stack, layer, *, n_out=None, epilogue="none", res=None, p=None, wpe_stack=None,
              out_dtype=F32):
    m, k = x.shape
    n = w_stack.shape[2] if n_out is None else n_out
    tm = min(m, 1024)
    tn = min(n, 512)
    assert n % tn == 0 and m % tm == 0
    in_specs = [pl.BlockSpec((tm, k), lambda j, i: (i, 0)),
                pl.BlockSpec((None, k, tn), lambda j, i: (layer, 0, j))]
    args = [x, w_stack]
    out_specs = [pl.BlockSpec((tm, tn), lambda j, i: (i, j)), pl.BlockSpec((k, tn), lambda j, i: (0, j))]
    out_shape = [jax.ShapeDtypeStruct((m, n), out_dtype), jax.ShapeDtypeStruct((k, n), BF16)]
    if epilogue in ("residual", "pe_gate"):
        in_specs.append(pl.BlockSpec((tm, tn), lambda j, i: (i, j)))
        args.append(res)
    if epilogue == "pe_gate":
        pe_dim = p.shape[1]
        in_specs.append(pl.BlockSpec((tm, pe_dim), lambda j, i: (i, 0)))
        in_specs.append(pl.BlockSpec((None, pe_dim, tn), lambda j, i: (layer, 0, j)))
        args += [p, wpe_stack]
        out_specs.append(pl.BlockSpec((pe_dim, tn), lambda j, i: (0, j)))
        out_shape.append(jax.ShapeDtypeStruct((pe_dim, n), BF16))
    return pl.pallas_call(
        functools.partial(_mm_rw_kernel, epilogue=epilogue),
        grid=(n // tn, m // tm),
        in_specs=in_specs,
        out_specs=out_specs,
        out_shape=out_shape,
        compiler_params=_cparams(("parallel", "arbitrary")),
        name="mmrw_" + epilogue,
    )(*args)


def _hgrn_level_matrices(r):
    idx = np.arange(r)
    t = idx[:, None]
    u = idx[None, :]
    mats = []
    lvl = 2
    while lvl <= r:
        h = lvl // 2
        pos = t % lvl
        mid = t - pos + h
        upper = pos >= h
        m_up = (u >= mid) & (u <= t)
        m_lo = (u > t) & (u <= mid - 1)
        mats.append(np.where(upper, m_up, m_lo))
        lvl *= 2
    mats.append(u <= t)
    mats.append(u > t)
    return np.concatenate(mats, axis=0).astype(np.float32)


def _hgrn_kernel(q_ref, f_ref, i_ref, ga_ref, lbl_ref, an_ref, s0_ref, mall_ref,
                 o_ref, s_ref, st_ref, *, r, nhb, layer_j, t_valid):
    c = pl.program_id(2)
    nc = pl.num_programs(2)
    nlev = int(math.log2(r))

    @pl.when(c == 0)
    def _():
        for hh in range(nhb):
            st_ref[hh] = s0_ref[0, hh].T

    lbl = lbl_ref[...]
    e = jnp.exp(lbl - jnp.max(lbl, axis=0, keepdims=True))
    soft = e / jnp.sum(e, axis=0, keepdims=True)
    lb_all = jnp.zeros((1, nhb * HEAD), F32)
    for jj in range(1, layer_j + 1):
        lb_all = lb_all + soft[jj:jj + 1, :]

    z = f_ref[...]
    logf = jnp.log1p(lb_all * jnp.exp(-z)) - _softplus(-z)
    kk_all = (1.0 - lb_all) * jax.nn.sigmoid(-z)
    qq_all = jax.nn.silu(q_ref[...])
    vv_all = i_ref[...]
    if t_valid is not None:
        live = lax.broadcasted_iota(jnp.int32, (r, 1), 0) < t_valid
        logf = jnp.where(live, logf, 0.0)
        kk_all = jnp.where(live, kk_all, 0.0)
        qq_all = jnp.where(live, qq_all, 0.0)
        vv_all = jnp.where(live, vv_all, 0.0)

    hi = logf.astype(BF16)
    mid = (logf - hi.astype(F32)).astype(BF16)
    xs_all = _dot(mall_ref[...], hi) + _dot(mall_ref[...], mid)

    ti = lax.broadcasted_iota(jnp.int32, (r, r), 0)
    si = lax.broadcasted_iota(jnp.int32, (r, r), 1)
    rowi = lax.broadcasted_iota(jnp.int32, (r, 1), 0)
    diag = ti == si
    ups, pairs = [], []
    for lv in range(nlev):
        blk = 2 << lv
        h = blk // 2
        ups.append((rowi & (blk - 1)) >= h)
        pairs.append(((ti >> (lv + 1)) == (si >> (lv + 1))) & ((ti & (blk - 1)) >= h) & ((si & (blk - 1)) < h))

    ga = ga_ref[...]
    an = an_ref[...]
    sls = [slice(hh * HEAD, (hh + 1) * HEAD) for hh in range(nhb)]
    xc0, xe0 = nlev * r, (nlev + 1) * r
    sts = [st_ref[hh] for hh in range(nhb)]
    vbs = [vv_all[:, sl].astype(BF16) for sl in sls]
    o_st = [_dot_nt((qq_all[:, sl] * jnp.exp(xs_all[xc0:xc0 + r, sl])).astype(BF16), sts[hh].astype(BF16))
            for hh, sl in enumerate(sls)]
    upd = [_dot_tn(vbs[hh], (kk_all[:, sl] * jnp.exp(xs_all[xe0:xe0 + r, sl])).astype(BF16))
           for hh, sl in enumerate(sls)]
    atts = []
    for hh, sl in enumerate(sls):
        qq, kk = qq_all[:, sl], kk_all[:, sl]
        att = jnp.where(diag, _dot_nt(qq.astype(BF16), kk.astype(BF16)), 0.0)
        for lv in range(nlev):
            w = jnp.exp(xs_all[lv * r:(lv + 1) * r, sl])
            qt = jnp.where(ups[lv], qq * w, 0.0).astype(BF16)
            kt = jnp.where(ups[lv], 0.0, kk * w).astype(BF16)
            att = att + jnp.where(pairs[lv], _dot_nt(qt, kt), 0.0)
        atts.append(att.astype(BF16))
    for hh, sl in enumerate(sls):
        o = _dot(atts[hh], vbs[hh]) + o_st[hh]
        g_end = jnp.exp(xs_all[xc0 + r - 1:xc0 + r, sl])
        st_ref[hh] = g_end * sts[hh] + upd[hh]
        on = o * lax.rsqrt(jnp.mean(o * o, axis=-1, keepdims=True) + EPS) * an[:, sl]
        o_ref[:, sl] = (on * jax.nn.silu(ga[:, sl])).astype(o_ref.dtype)

    @pl.when(c == nc - 1)
    def _():
        for hh in range(nhb):
            s_ref[0, hh] = st_ref[hh].T


def hgrn(proj, lb_logits, a_norm, s0, *, bsz, t, layer_j, r, t_valid=None):
    width = a_norm.shape[0]
    nh = width // HEAD
    nhb = HGRN_HEADS_PER_STEP
    ng = nh // nhb
    n_ab = lb_logits.shape[0]
    nchunk = t // r
    mall = jnp.asarray(_hgrn_level_matrices(r), BF16)
    bw = nhb * HEAD
    col = lambda off: (lambda b, h, c: (b * nchunk + c, off * ng + h))
    kern = functools.partial(_hgrn_kernel, r=r, nhb=nhb, layer_j=layer_j, t_valid=t_valid)
    return pl.pallas_call(
        kern,
        grid=(bsz, ng, nchunk),
        in_specs=[pl.BlockSpec((r, bw), col(0)),
                  pl.BlockSpec((r, bw), col(1)),
                  pl.BlockSpec((r, bw), col(2)),
                  pl.BlockSpec((r, bw), col(3)),
                  pl.BlockSpec((n_ab, bw), lambda b, h, c: (0, h)),
                  pl.BlockSpec((1, bw), lambda b, h, c: (0, h)),
                  pl.BlockSpec((1, nhb, HEAD, HEAD), lambda b, h, c: (b, h, 0, 0)),
                  pl.BlockSpec(mall.shape, lambda b, h, c: (0, 0))],
        out_specs=[pl.BlockSpec((r, bw), lambda b, h, c: (b * nchunk + c, h)),
                   pl.BlockSpec((1, nhb, HEAD, HEAD), lambda b, h, c: (b, h, 0, 0))],
        out_shape=[jax.ShapeDtypeStruct((bsz * t, width), BF16),
                   jax.ShapeDtypeStruct((bsz, nh, HEAD, HEAD), F32)],
        scratch_shapes=[pltpu.VMEM((nhb, HEAD, HEAD), F32)],
        compiler_params=_cparams(("parallel", "parallel", "arbitrary")),
        name="hgrn",
    )(proj, proj, proj, proj, lb_logits, a_norm.reshape(1, width), s0, mall)


def _softplus(x):
    return jnp.maximum(x, 0.0) + jnp.log1p(jnp.exp(-jnp.abs(x)))


def _rglru_kernel(x_ref, gb_ref, c0_ref, h0_ref, cw_ref, cb_ref, wa_ref, ba_ref, wx_ref, bx_ref,
                  lam_ref, o_ref, hl_ref, cn_ref, xp_ref, a_ref, b_ref, hc_ref,
                  *, r, nblk, first_pos_zero, t_valid):
    i = pl.program_id(1)
    ni = pl.num_programs(1)
    pad = V7X_SUBLANES

    @pl.when(i == 0)
    def _():
        xp_ref[0:pad, :] = c0_ref[0]
        hc_ref[...] = h0_ref[0]

    x = x_ref[...]
    xp_ref[pad:pad + r, :] = x
    cw = cw_ref[...]
    y = cw[0:1, :] * xp_ref[pad - 3:pad - 3 + r, :]
    y = y + cw[1:2, :] * xp_ref[pad - 2:pad - 2 + r, :]
    y = y + cw[2:3, :] * xp_ref[pad - 1:pad - 1 + r, :]
    y = cb_ref[...] + (y + cw[3:4, :] * x)

    last = r if t_valid is None else t_valid
    cn_ref[0] = xp_ref[pad + last - 3:pad + last, :]
    xp_ref[0:pad, :] = xp_ref[r:r + pad, :]

    ra = []
    rx = []
    for n in range(nblk):
        ys = y[:, n * HEAD:(n + 1) * HEAD].astype(BF16)
        ra.append(_dot(ys, wa_ref[n]))
        rx.append(_dot(ys, wx_ref[n]))
    rg = jax.nn.sigmoid(jnp.concatenate(ra, axis=1) + ba_ref[...])
    gi = jax.nn.sigmoid(jnp.concatenate(rx, axis=1) + bx_ref[...])
    log_a = (-RG_C * rg) * _softplus(-lam_ref[...])
    a = jnp.exp(log_a)
    th = jnp.tanh(log_a)
    mult = jnp.sqrt((-2.0 * th) / (1.0 - th))
    rowi = lax.broadcasted_iota(jnp.int32, (r, 1), 0)
    if first_pos_zero:
        mult = jnp.where((rowi == 0) & (i == 0), 1.0, mult)
    bterm = mult * gi * y
    if t_valid is not None:
        live = rowi < t_valid
        a = jnp.where(live, a, 1.0)
        bterm = jnp.where(live, bterm, 0.0)
    a_ref[...] = a
    b_ref[...] = bterm

    def step(tt, h):
        h = a_ref[pl.ds(tt, 1), :] * h + b_ref[pl.ds(tt, 1), :]
        b_ref[pl.ds(tt, 1), :] = h
        return h

    h_last = lax.fori_loop(0, r, step, hc_ref[...], unroll=8)
    hc_ref[...] = h_last
    o_ref[...] = (jax.nn.gelu(gb_ref[...]) * b_ref[...]).astype(o_ref.dtype)

    @pl.when(i == ni - 1)
    def _():
        hl_ref[0] = h_last


def rglru(proj, conv0, h0, conv_w, conv_b, wa, ba, wx, bx, lam, *, bsz, t, r, gb_blk, x_blk,
          first_pos_zero, t_valid=None):
    w = lam.shape[0]
    nblk = wa.shape[0]
    nrb = t // r
    pad = V7X_SUBLANES
    c0 = jnp.pad(conv0, ((0, 0), (pad - (CONV_W - 1), 0), (0, 0)))
    row = lambda v: v.reshape(1, w)
    kern = functools.partial(_rglru_kernel, r=r, nblk=nblk, first_pos_zero=first_pos_zero, t_valid=t_valid)
    full = lambda shape: pl.BlockSpec(shape, lambda b, i: (0,) * len(shape))
    return pl.pallas_call(
        kern,
        grid=(bsz, nrb),
        in_specs=[pl.BlockSpec((r, w), lambda b, i: (b * nrb + i, x_blk)),
                  pl.BlockSpec((r, w), lambda b, i: (b * nrb + i, gb_blk)),
                  pl.BlockSpec((1, pad, w), lambda b, i: (b, 0, 0)),
                  pl.BlockSpec((1, 1, w), lambda b, i: (b, 0, 0)),
                  full((CONV_W, w)), full((1, w)),
                  full(wa.shape), full((1, w)), full(wx.shape), full((1, w)), full((1, w))],
        out_specs=[pl.BlockSpec((r, w), lambda b, i: (b * nrb + i, 0)),
                   pl.BlockSpec((1, 1, w), lambda b, i: (b, 0, 0)),
                   pl.BlockSpec((1, CONV_W - 1, w), lambda b, i: (b, 0, 0))],
        out_shape=[jax.ShapeDtypeStruct((bsz * t, w), BF16),
                   jax.ShapeDtypeStruct((bsz, 1, w), F32),
                   jax.ShapeDtypeStruct((bsz, CONV_W - 1, w), F32)],
        scratch_shapes=[pltpu.VMEM((r + pad, w), F32), pltpu.VMEM((r, w), F32),
                        pltpu.VMEM((r, w), F32), pltpu.VMEM((1, w), F32)],
        compiler_params=_cparams(("parallel", "arbitrary")),
        name="rglru",
    )(proj, proj, c0, h0.reshape(bsz, 1, w), conv_w, row(conv_b), wa.astype(BF16), row(ba),
      wx.astype(BF16), row(bx), row(lam))


def _rope_tables(pos):
    rd = HEAD // 4
    half = rd // 2
    inv = jnp.exp(-math.log(ROPE_THETA) * jnp.arange(half, dtype=F32) * (2.0 / rd))
    ang = pos.astype(F32)[:, None] * inv[None, :]
    cos, sin = jnp.cos(ang), jnp.sin(ang)
    n = pos.shape[0]
    ones = jnp.ones((n, HEAD - rd), F32)
    zeros = jnp.zeros((n, HEAD - rd), F32)
    zh = jnp.zeros((n, half), F32)
    c = jnp.concatenate([cos, cos, ones], axis=1)
    a = jnp.concatenate([-sin, zh, zeros], axis=1)
    b = jnp.concatenate([zh, sin, zeros], axis=1)
    return c, a, b


def _rope(x, c, a, b):
    return x * c + pltpu.roll(x, HEAD - ROPE_HALF, 1) * a + pltpu.roll(x, ROPE_HALF, 1) * b


def _dsa_post_kernel(pm_ref, pt_ref, c_ref, a_ref, b_ref, qn_ref, kn_ref, kg_ref, kb_ref,
                     q_ref, qi_ref, k_ref, v_ref, ki_ref, wi_ref, kbf_ref, vbf_ref, kibf_ref,
                     *, nq, nkv, nidx, wi_scale):
    c, a, b = c_ref[...], a_ref[...], b_ref[...]
    qn, kn = qn_ref[...], kn_ref[...]
    off = 0
    for h in range(nq):
        x = pm_ref[:, off + h * HEAD:off + (h + 1) * HEAD]
        y = x * lax.rsqrt(jnp.mean(x * x, axis=-1, keepdims=True) + EPS) * qn
        q_ref[:, h * HEAD:(h + 1) * HEAD] = _rope(y, c, a, b).astype(q_ref.dtype)
    off += nq * HEAD
    for h in range(nkv):
        x = pm_ref[:, off + h * HEAD:off + (h + 1) * HEAD]
        y = x * lax.rsqrt(jnp.mean(x * x, axis=-1, keepdims=True) + EPS) * kn
        kr = _rope(y, c, a, b)
        k_ref[:, h * HEAD:(h + 1) * HEAD] = kr
        kbf_ref[:, h * HEAD:(h + 1) * HEAD] = kr.astype(BF16)
    off += nkv * HEAD
    vv = pm_ref[:, off:off + nkv * HEAD]
    v_ref[...] = vv
    vbf_ref[0] = vv.T.astype(BF16)
    off += nkv * HEAD
    for h in range(nidx):
        x = pm_ref[:, off + h * HEAD:off + (h + 1) * HEAD]
        qi_ref[:, h * HEAD:(h + 1) * HEAD] = _rope(x, c, a, b).astype(qi_ref.dtype)
    x = pt_ref[:, 0:HEAD]
    mu = jnp.mean(x, axis=-1, keepdims=True)
    xc = x - mu
    y = xc * lax.rsqrt(jnp.mean(xc * xc, axis=-1, keepdims=True) + EPS)
    kir = _rope(y * kg_ref[...] + kb_ref[...], c, a, b)
    ki_ref[...] = kir
    kibf_ref[...] = kir.astype(BF16)
    wi_ref[...] = pt_ref[:, HEAD:HEAD + nidx] * wi_scale


def dsa_post(proj_main, proj_tail, tables, qn, kn, kg, kb, *, nq, nkv, nidx, t_tab):
    m = proj_main.shape[0]
    tm = min(m, DSA_TQ, t_tab)
    ntab = t_tab // tm
    wm = proj_main.shape[1]
    wt = proj_tail.shape[1]
    rowspec = lambda wdt: pl.BlockSpec((tm, wdt), lambda i: (i, 0))
    tabspec = pl.BlockSpec((tm, HEAD), lambda i: (i % ntab, 0))
    vec = pl.BlockSpec((1, HEAD), lambda i: (0, 0))
    kern = functools.partial(_dsa_post_kernel, nq=nq, nkv=nkv, nidx=nidx,
                             wi_scale=float((nidx * HEAD) ** -0.5))
    outs = [(nq * HEAD, BF16), (nidx * HEAD, BF16), (nkv * HEAD, F32), (nkv * HEAD, F32),
            (HEAD, F32), (nidx, F32), (nkv * HEAD, BF16), None, (HEAD, BF16)]
    vt_spec = pl.BlockSpec((1, nkv * HEAD, tm), lambda i: (i, 0, 0))
    vt_shape = jax.ShapeDtypeStruct((m // tm, nkv * HEAD, tm), BF16)
    return pl.pallas_call(
        kern,
        grid=(m // tm,),
        in_specs=[rowspec(wm), rowspec(wt), tabspec, tabspec, tabspec, vec, vec, vec, vec],
        out_specs=[vt_spec if o is None else rowspec(o[0]) for o in outs],
        out_shape=[vt_shape if o is None else jax.ShapeDtypeStruct((m, o[0]), o[1]) for o in outs],
        compiler_params=_cparams(("parallel",)),
        name="dsa_post",
    )(proj_main, proj_tail, *tables, qn.reshape(1, HEAD), kn.reshape(1, HEAD),
      kg.reshape(1, HEAD), kb.reshape(1, HEAD))


def _sort_key(x):
    bits = pltpu.bitcast(x, jnp.int32)
    return jnp.where(bits < 0, bits ^ jnp.int32(0x7FFFFFFF), bits)


def _kth_largest_key(count_ge, rows, k):
    def body(it, thr):
        bit = lax.shift_left(jnp.int32(1), jnp.int32(31) - it)
        trial = thr + bit
        return jnp.where(count_ge(trial) >= k, trial, thr)

    shape = rows if isinstance(rows, tuple) else (rows, 1)
    init = jnp.full(shape, jnp.iinfo(jnp.int32).min, jnp.int32)
    return lax.fori_loop(0, 32, body, init)


def _dsa_prompt_kernel(qi_ref, wit_ref, kib_ref, q_ref, kb_ref, vt_ref, o_ref, key_ref, bias_ref,
                       *, tq, nidx, nq, nkv, topk, scale):
    i = pl.program_id(1)
    ntile = i + 1
    keyi = lax.broadcasted_iota(jnp.int32, (tq, tq), 0)
    qryi = lax.broadcasted_iota(jnp.int32, (tq, tq), 1)
    tri = keyi <= qryi

    def idx_tile(j, carry):
        kt = kib_ref[pl.ds(pl.multiple_of(j * tq, tq), tq), :]
        acc = jnp.zeros((tq, tq), F32)
        for h in range(nidx):
            d = _dot_nt(kt, qi_ref[:, h * HEAD:(h + 1) * HEAD])
            acc = acc + jnp.maximum(d, 0.0) * wit_ref[h:h + 1, :]
        acc = jnp.where(tri | (j < i), acc, NEG_BIG)
        key_ref[j] = _sort_key(acc)
        return carry

    lax.fori_loop(0, ntile, idx_tile, 0)

    def count_ge(trial):
        def tile(j, cnt):
            return cnt + jnp.sum(jnp.where(key_ref[j] >= trial, 1.0, 0.0), axis=0, keepdims=True)

        return lax.fori_loop(0, ntile, tile, jnp.zeros((1, tq), F32))

    thr = _kth_largest_key(count_ge, (1, tq), float(topk))

    def bias_tile(j, carry):
        keep = (key_ref[j] >= thr) & (tri | (j < i))
        bias_ref[j] = jnp.where(keep, 0.0, NEG_BIG)
        return carry

    lax.fori_loop(0, ntile, bias_tile, 0)

    c2 = scale * math.log2(math.e)
    grp = nq // nkv
    for n in range(nkv):
        qs = [q_ref[:, (n * grp + g) * HEAD:(n * grp + g + 1) * HEAD] for g in range(grp)]

        def att_tile(j, carry, n=n, qs=qs):
            kt = kb_ref[pl.ds(pl.multiple_of(j * tq, tq), tq), n * HEAD:(n + 1) * HEAD]
            vt = vt_ref[j, n * HEAD:(n + 1) * HEAD, :]
            bias = bias_ref[j]
            ss = [_dot_nt(kt, qs[g]) + bias for g in range(grp)]
            mid = []
            for g in range(grp):
                m_i, l_i, _ = carry[g]
                m_new = jnp.maximum(m_i, jnp.max(ss[g], axis=0, keepdims=True))
                p = jnp.exp2((ss[g] - m_new) * c2)
                alpha = jnp.exp2((m_i - m_new) * c2)
                mid.append((m_new, alpha * l_i + jnp.sum(p, axis=0, keepdims=True), alpha, p.astype(BF16)))
            pvs = [_dot(vt, mid[g][3]) for g in range(grp)]
            return tuple((mid[g][0], mid[g][1], mid[g][2] * carry[g][2] + pvs[g]) for g in range(grp))

        one = (jnp.full((1, tq), NEG_BIG, F32), jnp.zeros((1, tq), F32), jnp.zeros((HEAD, tq), F32))
        fin = lax.fori_loop(0, ntile, att_tile, (one,) * grp)
        for g in range(grp):
            h = n * grp + g
            o_ref[:, h * HEAD:(h + 1) * HEAD] = (fin[g][2] / fin[g][1]).T.astype(o_ref.dtype)


def dsa_prompt_attend(q_bf, qi_bf, wi_t, ki_bf, k_bf, vt_bf, *, bsz, t, nq, nkv, nidx):
    tq = DSA_TQ
    nqb = t // tq
    topk = min(TOPK_MAX, t // 4)
    assert topk <= tq and t % tq == 0
    qrow = lambda wdt: pl.BlockSpec((tq, wdt), lambda b, i: (b * nqb + i, 0))
    brow = lambda wdt: pl.BlockSpec((t, wdt), lambda b, i: (b, 0))
    kern = functools.partial(_dsa_prompt_kernel, tq=tq, nidx=nidx, nq=nq, nkv=nkv, topk=topk,
                             scale=float(HEAD ** -0.5))
    return pl.pallas_call(
        kern,
        grid=(bsz, nqb),
        in_specs=[qrow(nidx * HEAD), pl.BlockSpec((nidx, tq), lambda b, i: (0, b * nqb + i)), brow(HEAD),
                  qrow(nq * HEAD), brow(nkv * HEAD),
                  pl.BlockSpec((nqb, nkv * HEAD, tq), lambda b, i: (b, 0, 0))],
        out_specs=qrow(nq * HEAD),
        out_shape=jax.ShapeDtypeStruct((bsz * t, nq * HEAD), BF16),
        scratch_shapes=[pltpu.VMEM((nqb, tq, tq), jnp.int32), pltpu.VMEM((nqb, tq, tq), F32)],
        compiler_params=_cparams(("parallel", "arbitrary")),
        name="dsa_prompt",
    )(qi_bf, wi_t, ki_bf, q_bf, k_bf, vt_bf)


def _dsa_s_scores_kernel(pt_ref, qi_ref, wi_ref, page_ref, new_ref, sc_ref, *, ts, nidx, npages):
    p = pl.program_id(1)
    is_new = p == npages
    kp = jnp.where(is_new, new_ref[0], page_ref[0]).astype(BF16)
    d = _dot_nt(qi_ref[...], kp)
    r = jnp.maximum(d, 0.0) * wi_ref[...]
    psz = r.shape[1]
    rows = [jnp.sum(r[tt * nidx:(tt + 1) * nidx, :], axis=0, keepdims=True) for tt in range(ts)]
    sc = jnp.concatenate(rows + [jnp.full((V7X_SUBLANES - ts, psz), NEG_BIG, F32)], axis=0)
    rowi = lax.broadcasted_iota(jnp.int32, sc.shape, 0)
    coli = lax.broadcasted_iota(jnp.int32, sc.shape, 1)
    sc_ref[0] = jnp.where(is_new & (coli > rowi), NEG_BIG, sc)


def _dsa_s_select_kernel(sc_ref, bias_ref, *, topk, past):
    key = _sort_key(sc_ref[...])
    rows = key.shape[0]

    def count_ge(trial):
        return jnp.sum(jnp.where(key >= trial, 1.0, 0.0), axis=1, keepdims=True)

    thr = _kth_largest_key(count_ge, rows, float(topk))
    rowi = lax.broadcasted_iota(jnp.int32, key.shape, 0)
    coli = lax.broadcasted_iota(jnp.int32, key.shape, 1)
    visible = coli <= past + (rowi % V7X_SUBLANES)
    bias_ref[...] = jnp.where((key >= thr) & visible, 0.0, NEG_BIG)


def _dsa_s_attend_kernel(pt_ref, q_ref, bias_ref, kp_ref, vp_ref, kn_ref, vn_ref, o_ref,
                         m_ref, l_ref, acc_ref, *, nkv, npages, scale):
    p = pl.program_id(1)
    is_new = p == npages

    @pl.when(p == 0)
    def _():
        m_ref[...] = jnp.full_like(m_ref, NEG_BIG)
        l_ref[...] = jnp.zeros_like(l_ref)
        acc_ref[...] = jnp.zeros_like(acc_ref)

    kpage = jnp.where(is_new, kn_ref[0], kp_ref[0]).astype(BF16)
    vpage = jnp.where(is_new, vn_ref[0], vp_ref[0]).astype(BF16)
    bias = bias_ref[0]
    c2 = scale * math.log2(math.e)
    old = [(m_ref[n], l_ref[n], acc_ref[n]) for n in range(nkv)]
    ss = [_dot_nt(q_ref[0, n], kpage[:, n * HEAD:(n + 1) * HEAD]) + bias for n in range(nkv)]
    mid = []
    for n in range(nkv):
        m_i, l_i, _ = old[n]
        m_new = jnp.maximum(m_i, jnp.max(ss[n], axis=1, keepdims=True))
        pr = jnp.exp2((ss[n] - m_new) * c2)
        alpha = jnp.exp2((m_i - m_new) * c2)
        mid.append((m_new, alpha * l_i + jnp.sum(pr, axis=1, keepdims=True), alpha, pr.astype(BF16)))
    pvs = [_dot(mid[n][3], vpage[:, n * HEAD:(n + 1) * HEAD]) for n in range(nkv)]
    new = [(mid[n][0], mid[n][1], mid[n][2] * old[n][2] + pvs[n]) for n in range(nkv)]
    for n in range(nkv):
        m_ref[n], l_ref[n], acc_ref[n] = new[n]

    @pl.when(is_new)
    def _():
        for n in range(nkv):
            o_ref[0, n] = new[n][2] / new[n][1]


def dsa_sample_attend(q_bf, qi_bf, wi, ki_new, k_new, v_new, ck, cv, cki, page_table,
                      *, layer_j, bsz, ts, nq, nkv, nidx):
    n_pool, psz = ck.shape[1], ck.shape[2]
    npages = page_table.shape[1]
    past = npages * psz
    topk = min(TOPK_MAX, (past + ts) // 4)
    grp = nq // nkv
    ncol = (npages + 1) * psz
    sub = V7X_SUBLANES
    padrows = lambda x: jnp.pad(x.reshape(bsz, ts, -1), ((0, 0), (0, psz - ts), (0, 0)))
    page_idx = lambda b, p, pt: (layer_j * n_pool + pt[b, jnp.minimum(p, npages - 1)], 0, 0)
    per_b = lambda b, p, pt: (b, 0, 0)

    scores = pl.pallas_call(
        functools.partial(_dsa_s_scores_kernel, ts=ts, nidx=nidx, npages=npages),
        grid_spec=pltpu.PrefetchScalarGridSpec(
            num_scalar_prefetch=1,
            grid=(bsz, npages + 1),
            in_specs=[pl.BlockSpec((ts * nidx, HEAD), lambda b, p, pt: (b, 0)),
                      pl.BlockSpec((ts * nidx, 1), lambda b, p, pt: (b, 0)),
                      pl.BlockSpec((1, psz, HEAD), page_idx),
                      pl.BlockSpec((1, psz, HEAD), per_b)],
            out_specs=pl.BlockSpec((1, sub, psz), lambda b, p, pt: (b, 0, p))),
        out_shape=jax.ShapeDtypeStruct((bsz, sub, ncol), F32),
        compiler_params=_cparams(("parallel", "arbitrary")),
        name="dsa_s_scores",
    )(page_table, qi_bf.reshape(bsz * ts * nidx, HEAD), wi.reshape(bsz * ts * nidx, 1),
      cki.reshape(-1, psz, HEAD), padrows(ki_new))

    bias = pl.pallas_call(
        functools.partial(_dsa_s_select_kernel, topk=topk, past=past),
        out_shape=jax.ShapeDtypeStruct((bsz * sub, ncol), F32),
        compiler_params=pltpu.CompilerParams(vmem_limit_bytes=VMEM_LIMIT),
        name="dsa_s_select",
    )(scores.reshape(bsz * sub, ncol))
    bias = jnp.repeat(bias.reshape(bsz, sub, ncol)[:, :ts], grp, axis=1)

    q5 = q_bf.reshape(bsz, ts, nkv, grp, HEAD).transpose(0, 2, 1, 3, 4).reshape(bsz, nkv, ts * grp, HEAD)
    o = pl.pallas_call(
        functools.partial(_dsa_s_attend_kernel, nkv=nkv, npages=npages, scale=float(HEAD ** -0.5)),
        grid_spec=pltpu.PrefetchScalarGridSpec(
            num_scalar_prefetch=1,
            grid=(bsz, npages + 1),
            in_specs=[pl.BlockSpec((1, nkv, ts * grp, HEAD), lambda b, p, pt: (b, 0, 0, 0)),
                      pl.BlockSpec((1, ts * grp, psz), lambda b, p, pt: (b, 0, p)),
                      pl.BlockSpec((1, psz, nkv * HEAD), page_idx),
                      pl.BlockSpec((1, psz, nkv * HEAD), page_idx),
                      pl.BlockSpec((1, psz, nkv * HEAD), per_b),
                      pl.BlockSpec((1, psz, nkv * HEAD), per_b)],
            out_specs=pl.BlockSpec((1, nkv, ts * grp, HEAD), lambda b, p, pt: (b, 0, 0, 0)),
            scratch_shapes=[pltpu.VMEM((nkv, ts * grp, 1), F32), pltpu.VMEM((nkv, ts * grp, 1), F32),
                            pltpu.VMEM((nkv, ts * grp, HEAD), F32)]),
        out_shape=jax.ShapeDtypeStruct((bsz, nkv, ts * grp, HEAD), F32),
        compiler_params=_cparams(("parallel", "arbitrary")),
        name="dsa_s_attend",
    )(page_table, q5, bias, ck.reshape(-1, psz, nkv * HEAD), cv.reshape(-1, psz, nkv * HEAD),
      padrows(k_new), padrows(v_new))
    o = o.reshape(bsz, nkv, ts, grp, HEAD).transpose(0, 2, 1, 3, 4).reshape(bsz * ts, nq * HEAD)
    return o.astype(BF16)


def kernel(x_prompt, x_sample, state_hgrn, state_rglru_h, state_rglru_conv, cache_k, cache_v, cache_kidx,
           page_table, p_prompt, p_sample, norm_mix, norm_mlp, norm_pe, w_in_ab, w_out_ab, hgrn_lb_logits,
           hgrn_out_norm, rg_conv_w, rg_conv_b, rg_wa, rg_ba, rg_wx, rg_bx, rg_lambda, w_in_c, w_out_c,
           c_q_norm, c_k_norm, idx_k_ln_g, idx_k_ln_b, w_up, w_down, w_pe, w_pg):
    bp, tp, d = x_prompt.shape
    bs, ts, _ = x_sample.shape
    depth = norm_mix.shape[0]
    a_width = hgrn_out_norm.shape[1]
    b_width = rg_lambda.shape[1]
    nh_a = a_width // HEAD
    nkv = cache_k.shape[3]
    nq = w_out_c.shape[1] // HEAD
    nidx = (w_in_c.shape[2] - nq * HEAD - 2 * nkv * HEAD - HEAD) // (HEAD + 1)
    past = page_table.shape[1] * cache_k.shape[2]
    c_main = nq * HEAD + 2 * nkv * HEAD + nidx * HEAD
    pe_dim = p_prompt.shape[-1]

    hp = x_prompt.reshape(bp * tp, d)
    hs = x_sample.reshape(bs * ts, d)
    tab_p = _rope_tables(jnp.arange(tp, dtype=jnp.int32))
    tab_s = _rope_tables(past + jnp.arange(ts, dtype=jnp.int32))
    tab_s = tuple(jnp.tile(tb, (bs, 1)) for tb in tab_s)
    ts_pad = 16
    pad_s = lambda x: jnp.pad(x.reshape(bs, ts, -1), ((0, 0), (0, ts_pad - ts), (0, 0))).reshape(bs * ts_pad, -1)
    unpad_s = lambda x: x.reshape(bs, ts_pad, -1)[:, :ts].reshape(bs * ts, -1)

    outs = {k: [] for k in ("hg_p", "rh_p", "rc_p", "k_p", "v_p", "ki_p",
                            "hg_s", "rh_s", "rc_s", "k_s", "v_s", "ki_s")}
    for layer in range(depth):
        j = layer // 2
        xpn = rmsnorm_bf16(hp, norm_mix[layer])
        xsn = rmsnorm_bf16(hs, norm_mix[layer])
        if layer % 2 == 0:
            gb_blk = 4 * a_width // b_width
            x_blk = gb_blk + 1
            rg_args = (rg_conv_w[j], rg_conv_b[j], rg_wa[j], rg_ba[j], rg_wx[j], rg_bx[j], rg_lambda[j])

            proj, w_in = matmul_rw(xpn, w_in_ab, j)
            oa, sa = hgrn(proj, hgrn_lb_logits, hgrn_out_norm[j], jnp.zeros((bp, nh_a, HEAD, HEAD), F32),
                          bsz=bp, t=tp, layer_j=j, r=min(128, tp))
            ob, hl, cn = rglru(proj, jnp.zeros((bp, CONV_W - 1, b_width), F32), jnp.zeros((bp, b_width), F32),
                               *rg_args, bsz=bp, t=tp, r=min(256, tp), gb_blk=gb_blk, x_blk=x_blk,
                               first_pos_zero=True)
            hp, w_out = matmul_rw(jnp.concatenate([oa, ob], axis=1), w_out_ab, j, epilogue="residual", res=hp)
            outs["hg_p"].append(sa)
            outs["rh_p"].append(hl.reshape(bp, b_width))
            outs["rc_p"].append(cn)

            proj = pad_s(matmul(xsn, w_in))
            oa, sa = hgrn(proj, hgrn_lb_logits, hgrn_out_norm[j], state_hgrn[j],
                          bsz=bs, t=ts_pad, layer_j=j, r=ts_pad, t_valid=ts)
            ob, hl, cn = rglru(proj, state_rglru_conv[j], state_rglru_h[j], *rg_args, bsz=bs, t=ts_pad,
                               r=ts_pad, gb_blk=gb_blk, x_blk=x_blk, first_pos_zero=False, t_valid=ts)
            mix = unpad_s(jnp.concatenate([oa, ob], axis=1))
            hs = matmul(mix, w_out, epilogue="residual", res=hs)
            outs["hg_s"].append(sa)
            outs["rh_s"].append(hl.reshape(bs, b_width))
            outs["rc_s"].append(cn)
        else:
            w_tail_f32 = jnp.pad(w_in_c[j:j + 1, :, c_main:], ((0, 0), (0, 0), (0, 2 * HEAD - (HEAD + nidx))))
            norms = (c_q_norm[j], c_k_norm[j], idx_k_ln_g[j], idx_k_ln_b[j])

            pm, w_main = matmul_rw(xpn, w_in_c, j, n_out=c_main)
            ptl, w_tail = matmul_rw(xpn, w_tail_f32, 0)
            post = dsa_post(pm, ptl, tab_p, *norms, nq=nq, nkv=nkv, nidx=nidx, t_tab=tp)
            q_bf, qi_bf, kk, vv, ki, wi, k_bf, vt_bf, ki_bf = post
            o = dsa_prompt_attend(q_bf, qi_bf, wi.T, ki_bf, k_bf, vt_bf, bsz=bp, t=tp, nq=nq, nkv=nkv, nidx=nidx)
            hp, w_out = matmul_rw(o, w_out_c, j, epilogue="residual", res=hp)
            outs["k_p"].append(kk.reshape(bp, tp, nkv, HEAD))
            outs["v_p"].append(vv.reshape(bp, tp, nkv, HEAD))
            outs["ki_p"].append(ki.reshape(bp, tp, HEAD))

            post = dsa_post(matmul(xsn, w_main), matmul(xsn, w_tail), tab_s, *norms,
                            nq=nq, nkv=nkv, nidx=nidx, t_tab=bs * ts)
            q_bf, qi_bf, kk, vv, ki, wi, _, _, _ = post
            o = dsa_sample_attend(q_bf, qi_bf, wi, ki, kk, vv, cache_k, cache_v, cache_kidx,
                                  page_table, layer_j=j, bsz=bs, ts=ts, nq=nq, nkv=nkv, nidx=nidx)
            hs = matmul(o, w_out, epilogue="residual", res=hs)
            outs["k_s"].append(kk.reshape(bs, ts, nkv, HEAD))
            outs["v_s"].append(vv.reshape(bs, ts, nkv, HEAD))
            outs["ki_s"].append(ki.reshape(bs, ts, HEAD))

        wd = w_down[layer].astype(BF16)
        pp = p_prompt[layer].reshape(-1, pe_dim).astype(BF16)
        ps = p_sample[layer].reshape(-1, pe_dim).astype(BF16)
        mid, wu = matmul_rw(rmsnorm_bf16(hp, norm_mlp[layer]), w_up, layer, epilogue="relu2", out_dtype=BF16)
        hp = matmul(mid, wd, epilogue="residual", res=hp)
        hp, wpg, wpe = matmul_rw(rmsnorm_bf16(hp, norm_pe[layer]), w_pg, layer, epilogue="pe_gate", res=hp,
                                 p=pp, wpe_stack=w_pe)
        mid = matmul(rmsnorm_bf16(hs, norm_mlp[layer]), wu, epilogue="relu2", out_dtype=BF16)
        hs = matmul(mid, wd, epilogue="residual", res=hs)
        hs = matmul(rmsnorm_bf16(hs, norm_pe[layer]), wpg, epilogue="pe_gate", res=hs, p=ps, w_pe=wpe)

    st = lambda name: jnp.stack(outs[name])
    return (hp.reshape(bp, tp, d), hs.reshape(bs, ts, d),
            st("hg_p"), st("rh_p"), st("rc_p"), st("k_p"), st("v_p"), st("ki_p"),
            st("hg_s"), st("rh_s"), st("rc_s"), st("k_s"), st("v_s"), st("ki_s"))
```

```python
import functools
import math

import jax
import jax.numpy as jnp
import numpy as np
from jax import lax
from jax.experimental import pallas as pl
from jax.experimental.pallas import tpu as pltpu

F32 = jnp.float32
BF16 = jnp.bfloat16

V7X_LANES = 128
V7X_SUBLANES = 8
V7X_VMEM_BYTES = 64 * 1024 * 1024
VMEM_LIMIT = 56 * 1024 * 1024

EPS = 1e-6
NEG_BIG = -1e30
RG_C = 8.0
ROPE_THETA = 500000.0
CONV_W = 4
TOPK_MAX = 256

HEAD = 128
ROPE_HALF = HEAD // 8
DSA_PAGES_PER_STEP = 4
DSA_TQ = 256
HGRN_HEADS_PER_STEP = 4

_NT = (((1,), (1,)), ((), ()))
_TN = (((0,), (0,)), ((), ()))


def _cparams(sem):
    return pltpu.CompilerParams(dimension_semantics=sem, vmem_limit_bytes=VMEM_LIMIT)


def _dot(a, b):
    return jnp.dot(a, b, preferred_element_type=F32)


def _dot_nt(a, b):
    return lax.dot_general(a, b, _NT, preferred_element_type=F32)


def _dot_tn(a, b):
    return lax.dot_general(a, b, _TN, preferred_element_type=F32)


def _rmsnorm_kernel(x_ref, g_ref, o_ref):
    x = x_ref[...]
    y = x * lax.rsqrt(jnp.mean(x * x, axis=-1, keepdims=True) + EPS)
    o_ref[...] = (y * g_ref[...]).astype(o_ref.dtype)


def rmsnorm_bf16(x, g):
    m, d = x.shape
    tm = min(m, 256)
    return pl.pallas_call(
        _rmsnorm_kernel,
        grid=(m // tm,),
        in_specs=[pl.BlockSpec((tm, d), lambda i: (i, 0)),
                  pl.BlockSpec((1, d), lambda i: (0, 0))],
        out_specs=pl.BlockSpec((tm, d), lambda i: (i, 0)),
        out_shape=jax.ShapeDtypeStruct((m, d), BF16),
        compiler_params=_cparams(("parallel",)),
        name="rmsnorm",
    )(x, g.reshape(1, d))


def _mm_kernel(*refs, nk, epilogue):
    x_ref, w_ref = refs[0], refs[1]
    pos = 2
    res_ref = p_ref = wpe_ref = None
    if epilogue in ("residual", "pe_gate"):
        res_ref = refs[pos]
        pos += 1
    if epilogue == "pe_gate":
        p_ref, wpe_ref = refs[pos], refs[pos + 1]
        pos += 2
    o_ref = refs[pos]
    acc_ref = refs[pos + 1] if nk > 1 else None

    def finish(acc):
        if epilogue == "none":
            out = acc
        elif epilogue == "relu2":
            r = jnp.maximum(acc, 0.0)
            out = r * r
        elif epilogue == "residual":
            out = res_ref[...] + acc
        else:
            pe = _dot(p_ref[...], wpe_ref[...])
            out = res_ref[...] + pe * jax.nn.sigmoid(acc)
        o_ref[...] = out.astype(o_ref.dtype)

    if nk == 1:
        finish(_dot(x_ref[...], w_ref[...]))
    else:
        k = pl.program_id(2)

        @pl.when(k == 0)
        def _():
            acc_ref[...] = jnp.zeros_like(acc_ref)

        acc_ref[...] += _dot(x_ref[...], w_ref[...])

        @pl.when(k == nk - 1)
        def _():
            finish(acc_ref[...])


def _mm_tiles(m, n, k):
    tm = min(m, 1024)
    if m <= 64:
        tn = min(n, 2048)
    else:
        tn = min(n, 512)
    while n % tn:
        tn //= 2
    tk = min(k, 4096)
    return tm, tn, tk


def matmul(x, w, *, epilogue="none", res=None, p=None, w_pe=None, out_dtype=F32):
    m, k = x.shape
    _, n = w.shape
    tm, tn, tk = _mm_tiles(m, n, k)
    nk = k // tk
    if nk == 1:
        grid = (m // tm, n // tn)
        xmap = lambda i, j: (i, 0)
        wmap = lambda i, j: (0, j)
        omap = lambda i, j: (i, j)
        pmap = lambda i, j: (i, 0)
        sem = ("parallel", "parallel")
    else:
        grid = (m // tm, n // tn, nk)
        xmap = lambda i, j, kk: (i, kk)
        wmap = lambda i, j, kk: (kk, j)
        omap = lambda i, j, kk: (i, j)
        pmap = lambda i, j, kk: (i, 0)
        sem = ("parallel", "parallel", "arbitrary")
    in_specs = [pl.BlockSpec((tm, tk), xmap), pl.BlockSpec((tk, tn), wmap)]
    args = [x, w]
    if epilogue in ("residual", "pe_gate"):
        in_specs.append(pl.BlockSpec((tm, tn), omap))
        args.append(res)
    if epilogue == "pe_gate":
        pe_dim = p.shape[1]
        in_specs.append(pl.BlockSpec((tm, pe_dim), pmap))
        in_specs.append(pl.BlockSpec((pe_dim, tn), wmap if nk == 1 else (lambda i, j, kk: (0, j))))
        args += [p, w_pe]
    scratch = [pltpu.VMEM((tm, tn), F32)] if nk > 1 else []
    return pl.pallas_call(
        functools.partial(_mm_kernel, nk=nk, epilogue=epilogue),
        grid=grid,
        in_specs=in_specs,
        out_specs=pl.BlockSpec((tm, tn), omap),
        out_shape=jax.ShapeDtypeStruct((m, n), out_dtype),
        scratch_shapes=scratch,
        compiler_params=_cparams(sem),
        name="mm_" + epilogue,
    )(*args)


def _mm_rw_kernel(*refs, epilogue):
    x_ref, w_ref = refs[0], refs[1]
    pos = 2
    res_ref = p_ref = wpe_ref = wpeb_ref = None
    if epilogue in ("residual", "pe_gate"):
        res_ref = refs[pos]
        pos += 1
    if epilogue == "pe_gate":
        p_ref, wpe_ref = refs[pos], refs[pos + 1]
        pos += 2
    o_ref, wb_ref = refs[pos], refs[pos + 1]
    if epilogue == "pe_gate":
        wpeb_ref = refs[pos + 2]

    @pl.when(pl.program_id(1) == 0)
    def _():
        wb_ref[...] = w_ref[...].astype(BF16)
        if epilogue == "pe_gate":
            wpeb_ref[...] = wpe_ref[...].astype(BF16)

    acc = _dot(x_ref[...], wb_ref[...])
    if epilogue == "none":
        out = acc
    elif epilogue == "relu2":
        r = jnp.maximum(acc, 0.0)
        out = r * r
    elif epilogue == "residual":
        out = res_ref[...] + acc
    else:
        out = res_ref[...] + _dot(p_ref[...], wpeb_ref[...]) * jax.nn.sigmoid(acc)
    o_ref[...] = out.astype(o_ref.dtype)


def matmul_rw(x, w_stack, layer, *, n_out=None, epilogue="none", res=None, p=None, wpe_stack=None,
              out_dtype=F32):
    m, k = x.shape
    n = w_stack.shape[2] if n_out is None else n_out
    tm = min(m, 1024)
    tn = min(n, 512)
    assert n % tn == 0 and m % tm == 0
    in_specs = [pl.BlockSpec((tm, k), lambda j, i: (i, 0)),
                pl.BlockSpec((None, k, tn), lambda j, i: (layer, 0, j))]
    args = [x, w_stack]
    out_specs = [pl.BlockSpec((tm, tn), lambda j, i: (i, j)), pl.BlockSpec((k, tn), lambda j, i: (0, j))]
    out_shape = [jax.ShapeDtypeStruct((m, n), out_dtype), jax.ShapeDtypeStruct((k, n), BF16)]
    if epilogue in ("residual", "pe_gate"):
        in_specs.append(pl.BlockSpec((tm, tn), lambda j, i: (i, j)))
        args.append(res)
    if epilogue == "pe_gate":
        pe_dim = p.shape[1]
        in_specs.append(pl.BlockSpec((tm, pe_dim), lambda j, i: (i, 0)))
        in_specs.append(pl.BlockSpec((None, pe_dim, tn), lambda j, i: (layer, 0, j)))
        args += [p, wpe_stack]
        out_specs.append(pl.BlockSpec((pe_dim, tn), lambda j, i: (0, j)))
        out_shape.append(jax.ShapeDtypeStruct((pe_dim, n), BF16))
    return pl.pallas_call(
        functools.partial(_mm_rw_kernel, epilogue=epilogue),
        grid=(n // tn, m // tm),
        in_specs=in_specs,
        out_specs=out_specs,
        out_shape=out_shape,
        compiler_params=_cparams(("parallel", "arbitrary")),
        name="mmrw_" + epilogue,
    )(*args)


def _hgrn_level_matrices(r):
    idx = np.arange(r)
    t = idx[:, None]
    u = idx[None, :]
    mats = []
    lvl = 2
    while lvl <= r:
        h = lvl // 2
        pos = t % lvl
        mid = t - pos + h
        upper = pos >= h
        m_up = (u >= mid) & (u <= t)
        m_lo = (u > t) & (u <= mid - 1)
        mats.append(np.where(upper, m_up, m_lo))
        lvl *= 2
    mats.append(u <= t)
    mats.append(u > t)
    return np.concatenate(mats, axis=0).astype(np.float32)


def _hgrn_kernel(q_ref, f_ref, i_ref, ga_ref, lbl_ref, an_ref, s0_ref, mall_ref,
                 o_ref, s_ref, st_ref, *, r, nhb, layer_j, t_valid):
    c = pl.program_id(2)
    nc = pl.num_programs(2)
    nlev = int(math.log2(r))

    @pl.when(c == 0)
    def _():
        for hh in range(nhb):
            st_ref[hh] = s0_ref[0, hh].T

    lbl = lbl_ref[...]
    e = jnp.exp(lbl - jnp.max(lbl, axis=0, keepdims=True))
    soft = e / jnp.sum(e, axis=0, keepdims=True)
    lb_all = jnp.zeros((1, nhb * HEAD), F32)
    for jj in range(1, layer_j + 1):
        lb_all = lb_all + soft[jj:jj + 1, :]

    z = f_ref[...]
    logf = jnp.log1p(lb_all * jnp.exp(-z)) - _softplus(-z)
    kk_all = (1.0 - lb_all) * jax.nn.sigmoid(-z)
    qq_all = jax.nn.silu(q_ref[...])
    vv_all = i_ref[...]
    if t_valid is not None:
        live = lax.broadcasted_iota(jnp.int32, (r, 1), 0) < t_valid
        logf = jnp.where(live, logf, 0.0)
        kk_all = jnp.where(live, kk_all, 0.0)
        qq_all = jnp.where(live, qq_all, 0.0)
        vv_all = jnp.where(live, vv_all, 0.0)

    hi = logf.astype(BF16)
    mid = (logf - hi.astype(F32)).astype(BF16)
    xs_all = _dot(mall_ref[...], hi) + _dot(mall_ref[...], mid)

    ti = lax.broadcasted_iota(jnp.int32, (r, r), 0)
    si = lax.broadcasted_iota(jnp.int32, (r, r), 1)
    rowi = lax.broadcasted_iota(jnp.int32, (r, 1), 0)
    diag = ti == si
    ups, pairs = [], []
    for lv in range(nlev):
        blk = 2 << lv
        h = blk // 2
        ups.append((rowi & (blk - 1)) >= h)
        pairs.append(((ti >> (lv + 1)) == (si >> (lv + 1))) & ((ti & (blk - 1)) >= h) & ((si & (blk - 1)) < h))

    ga = ga_ref[...]
    an = an_ref[...]
    sls = [slice(hh * HEAD, (hh + 1) * HEAD) for hh in range(nhb)]
    xc0, xe0 = nlev * r, (nlev + 1) * r
    sts = [st_ref[hh] for hh in range(nhb)]
    vbs = [vv_all[:, sl].astype(BF16) for sl in sls]
    o_st = [_dot_nt((qq_all[:, sl] * jnp.exp(xs_all[xc0:xc0 + r, sl])).astype(BF16), sts[hh].astype(BF16))
            for hh, sl in enumerate(sls)]
    upd = [_dot_tn(vbs[hh], (kk_all[:, sl] * jnp.exp(xs_all[xe0:xe0 + r, sl])).astype(BF16))
           for hh, sl in enumerate(sls)]
    atts = []
    for hh, sl in enumerate(sls):
        qq, kk = qq_all[:, sl], kk_all[:, sl]
        att = jnp.where(diag, _dot_nt(qq.astype(BF16), kk.astype(BF16)), 0.0)
        for lv in range(nlev):
            w = jnp.exp(xs_all[lv * r:(lv + 1) * r, sl])
            qt = jnp.where(ups[lv], qq * w, 0.0).astype(BF16)
            kt = jnp.where(ups[lv], 0.0, kk * w).astype(BF16)
            att = att + jnp.where(pairs[lv], _dot_nt(qt, kt), 0.0)
        atts.append(att.astype(BF16))
    for hh, sl in enumerate(sls):
        o = _dot(atts[hh], vbs[hh]) + o_st[hh]
        g_end = jnp.exp(xs_all[xc0 + r - 1:xc0 + r, sl])
        st_ref[hh] = g_end * sts[hh] + upd[hh]
        on = o * lax.rsqrt(jnp.mean(o * o, axis=-1, keepdims=True) + EPS) * an[:, sl]
        o_ref[:, sl] = (on * jax.nn.silu(ga[:, sl])).astype(o_ref.dtype)

    @pl.when(c == nc - 1)
    def _():
        for hh in range(nhb):
            s_ref[0, hh] = st_ref[hh].T


def hgrn(proj, lb_logits, a_norm, s0, *, bsz, t, layer_j, r, t_valid=None):
    width = a_norm.shape[0]
    nh = width // HEAD
    nhb = HGRN_HEADS_PER_STEP
    ng = nh // nhb
    n_ab = lb_logits.shape[0]
    nchunk = t // r
    mall = jnp.asarray(_hgrn_level_matrices(r), BF16)
    bw = nhb * HEAD
    col = lambda off: (lambda b, h, c: (b * nchunk + c, off * ng + h))
    kern = functools.partial(_hgrn_kernel, r=r, nhb=nhb, layer_j=layer_j, t_valid=t_valid)
    return pl.pallas_call(
        kern,
        grid=(bsz, ng, nchunk),
        in_specs=[pl.BlockSpec((r, bw), col(0)),
                  pl.BlockSpec((r, bw), col(1)),
                  pl.BlockSpec((r, bw), col(2)),
                  pl.BlockSpec((r, bw), col(3)),
                  pl.BlockSpec((n_ab, bw), lambda b, h, c: (0, h)),
                  pl.BlockSpec((1, bw), lambda b, h, c: (0, h)),
                  pl.BlockSpec((1, nhb, HEAD, HEAD), lambda b, h, c: (b, h, 0, 0)),
                  pl.BlockSpec(mall.shape, lambda b, h, c: (0, 0))],
        out_specs=[pl.BlockSpec((r, bw), lambda b, h, c: (b * nchunk + c, h)),
                   pl.BlockSpec((1, nhb, HEAD, HEAD), lambda b, h, c: (b, h, 0, 0))],
        out_shape=[jax.ShapeDtypeStruct((bsz * t, width), BF16),
                   jax.ShapeDtypeStruct((bsz, nh, HEAD, HEAD), F32)],
        scratch_shapes=[pltpu.VMEM((nhb, HEAD, HEAD), F32)],
        compiler_params=_cparams(("parallel", "parallel", "arbitrary")),
        name="hgrn",
    )(proj, proj, proj, proj, lb_logits, a_norm.reshape(1, width), s0, mall)


def _softplus(x):
    return jnp.maximum(x, 0.0) + jnp.log1p(jnp.exp(-jnp.abs(x)))


def _rglru_kernel(x_ref, gb_ref, c0_ref, h0_ref, cw_ref, cb_ref, wa_ref, ba_ref, wx_ref, bx_ref,
                  lam_ref, o_ref, hl_ref, cn_ref, xp_ref, a_ref, b_ref, hc_ref,
                  *, r, nblk, first_pos_zero, t_valid):
    i = pl.program_id(1)
    ni = pl.num_programs(1)
    pad = V7X_SUBLANES

    @pl.when(i == 0)
    def _():
        xp_ref[0:pad, :] = c0_ref[0]
        hc_ref[...] = h0_ref[0]

    x = x_ref[...]
    xp_ref[pad:pad + r, :] = x
    cw = cw_ref[...]
    y = cw[0:1, :] * xp_ref[pad - 3:pad - 3 + r, :]
    y = y + cw[1:2, :] * xp_ref[pad - 2:pad - 2 + r, :]
    y = y + cw[2:3, :] * xp_ref[pad - 1:pad - 1 + r, :]
    y = cb_ref[...] + (y + cw[3:4, :] * x)

    last = r if t_valid is None else t_valid
    cn_ref[0] = xp_ref[pad + last - 3:pad + last, :]
    xp_ref[0:pad, :] = xp_ref[r:r + pad, :]

    ra = []
    rx = []
    for n in range(nblk):
        ys = y[:, n * HEAD:(n + 1) * HEAD].astype(BF16)
        ra.append(_dot(ys, wa_ref[n]))
        rx.append(_dot(ys, wx_ref[n]))
    rg = jax.nn.sigmoid(jnp.concatenate(ra, axis=1) + ba_ref[...])
    gi = jax.nn.sigmoid(jnp.concatenate(rx, axis=1) + bx_ref[...])
    log_a = (-RG_C * rg) * _softplus(-lam_ref[...])
    a = jnp.exp(log_a)
    th = jnp.tanh(log_a)
    mult = jnp.sqrt((-2.0 * th) / (1.0 - th))
    rowi = lax.broadcasted_iota(jnp.int32, (r, 1), 0)
    if first_pos_zero:
        mult = jnp.where((rowi == 0) & (i == 0), 1.0, mult)
    bterm = mult * gi * y
    if t_valid is not None:
        live = rowi < t_valid
        a = jnp.where(live, a, 1.0)
        bterm = jnp.where(live, bterm, 0.0)
    a_ref[...] = a
    b_ref[...] = bterm

    def step(tt, h):
        h = a_ref[pl.ds(tt, 1), :] * h + b_ref[pl.ds(tt, 1), :]
        b_ref[pl.ds(tt, 1), :] = h
        return h

    h_last = lax.fori_loop(0, r, step, hc_ref[...], unroll=8)
    hc_ref[...] = h_last
    o_ref[...] = (jax.nn.gelu(gb_ref[...]) * b_ref[...]).astype(o_ref.dtype)

    @pl.when(i == ni - 1)
    def _():
        hl_ref[0] = h_last


def rglru(proj, conv0, h0, conv_w, conv_b, wa, ba, wx, bx, lam, *, bsz, t, r, gb_blk, x_blk,
          first_pos_zero, t_valid=None):
    w = lam.shape[0]
    nblk = wa.shape[0]
    nrb = t // r
    pad = V7X_SUBLANES
    c0 = jnp.pad(conv0, ((0, 0), (pad - (CONV_W - 1), 0), (0, 0)))
    row = lambda v: v.reshape(1, w)
    kern = functools.partial(_rglru_kernel, r=r, nblk=nblk, first_pos_zero=first_pos_zero, t_valid=t_valid)
    full = lambda shape: pl.BlockSpec(shape, lambda b, i: (0,) * len(shape))
    return pl.pallas_call(
        kern,
        grid=(bsz, nrb),
        in_specs=[pl.BlockSpec((r, w), lambda b, i: (b * nrb + i, x_blk)),
                  pl.BlockSpec((r, w), lambda b, i: (b * nrb + i, gb_blk)),
                  pl.BlockSpec((1, pad, w), lambda b, i: (b, 0, 0)),
                  pl.BlockSpec((1, 1, w), lambda b, i: (b, 0, 0)),
                  full((CONV_W, w)), full((1, w)),
                  full(wa.shape), full((1, w)), full(wx.shape), full((1, w)), full((1, w))],
        out_specs=[pl.BlockSpec((r, w), lambda b, i: (b * nrb + i, 0)),
                   pl.BlockSpec((1, 1, w), lambda b, i: (b, 0, 0)),
                   pl.BlockSpec((1, CONV_W - 1, w), lambda b, i: (b, 0, 0))],
        out_shape=[jax.ShapeDtypeStruct((bsz * t, w), BF16),
                   jax.ShapeDtypeStruct((bsz, 1, w), F32),
                   jax.ShapeDtypeStruct((bsz, CONV_W - 1, w), F32)],
        scratch_shapes=[pltpu.VMEM((r + pad, w), F32), pltpu.VMEM((r, w), F32),
                        pltpu.VMEM((r, w), F32), pltpu.VMEM((1, w), F32)],
        compiler_params=_cparams(("parallel", "arbitrary")),
        name="rglru",
    )(proj, proj, c0, h0.reshape(bsz, 1, w), conv_w, row(conv_b), wa.astype(BF16), row(ba),
      wx.astype(BF16), row(bx), row(lam))


def _rope_tables(pos):
    rd = HEAD // 4
    half = rd // 2
    inv = jnp.exp(-math.log(ROPE_THETA) * jnp.arange(half, dtype=F32) * (2.0 / rd))
    ang = pos.astype(F32)[:, None] * inv[None, :]
    cos, sin = jnp.cos(ang), jnp.sin(ang)
    n = pos.shape[0]
    ones = jnp.ones((n, HEAD - rd), F32)
    zeros = jnp.zeros((n, HEAD - rd), F32)
    zh = jnp.zeros((n, half), F32)
    c = jnp.concatenate([cos, cos, ones], axis=1)
    a = jnp.concatenate([-sin, zh, zeros], axis=1)
    b = jnp.concatenate([zh, sin, zeros], axis=1)
    return c, a, b


def _rope(x, c, a, b):
    return x * c + pltpu.roll(x, HEAD - ROPE_HALF, 1) * a + pltpu.roll(x, ROPE_HALF, 1) * b


def _dsa_post_kernel(pm_ref, pt_ref, c_ref, a_ref, b_ref, qn_ref, kn_ref, kg_ref, kb_ref,
                     q_ref, qi_ref, k_ref, v_ref, ki_ref, wi_ref, kbf_ref, vbf_ref, kibf_ref,
                     *, nq, nkv, nidx, wi_scale):
    c, a, b = c_ref[...], a_ref[...], b_ref[...]
    qn, kn = qn_ref[...], kn_ref[...]
    off = 0
    for h in range(nq):
        x = pm_ref[:, off + h * HEAD:off + (h + 1) * HEAD]
        y = x * lax.rsqrt(jnp.mean(x * x, axis=-1, keepdims=True) + EPS) * qn
        q_ref[:, h * HEAD:(h + 1) * HEAD] = _rope(y, c, a, b).astype(q_ref.dtype)
    off += nq * HEAD
    for h in range(nkv):
        x = pm_ref[:, off + h * HEAD:off + (h + 1) * HEAD]
        y = x * lax.rsqrt(jnp.mean(x * x, axis=-1, keepdims=True) + EPS) * kn
        kr = _rope(y, c, a, b)
        k_ref[:, h, :] = kr
        kbf_ref[:, h * HEAD:(h + 1) * HEAD] = kr.astype(BF16)
    off += nkv * HEAD
    vv = pm_ref[:, off:off + nkv * HEAD]
    for h in range(nkv):
        v_ref[:, h, :] = vv[:, h * HEAD:(h + 1) * HEAD]
    vbf_ref[0] = vv.T.astype(BF16)
    off += nkv * HEAD
    for h in range(nidx):
        x = pm_ref[:, off + h * HEAD:off + (h + 1) * HEAD]
        qi_ref[:, h * HEAD:(h + 1) * HEAD] = _rope(x, c, a, b).astype(qi_ref.dtype)
    x = pt_ref[:, 0:HEAD]
    mu = jnp.mean(x, axis=-1, keepdims=True)
    xc = x - mu
    y = xc * lax.rsqrt(jnp.mean(xc * xc, axis=-1, keepdims=True) + EPS)
    kir = _rope(y * kg_ref[...] + kb_ref[...], c, a, b)
    ki_ref[...] = kir
    kibf_ref[...] = kir.astype(BF16)
    wi_ref[...] = pt_ref[:, HEAD:HEAD + nidx] * wi_scale


def dsa_post(proj_main, proj_tail, tables, qn, kn, kg, kb, *, nq, nkv, nidx, t_tab):
    m = proj_main.shape[0]
    tm = min(m, DSA_TQ, t_tab)
    ntab = t_tab // tm
    wm = proj_main.shape[1]
    wt = proj_tail.shape[1]
    rowspec = lambda wdt: pl.BlockSpec((tm, wdt), lambda i: (i, 0))
    tabspec = pl.BlockSpec((tm, HEAD), lambda i: (i % ntab, 0))
    vec = pl.BlockSpec((1, HEAD), lambda i: (0, 0))
    kern = functools.partial(_dsa_post_kernel, nq=nq, nkv=nkv, nidx=nidx,
                             wi_scale=float((nidx * HEAD) ** -0.5))
    kv_spec = pl.BlockSpec((tm, nkv, HEAD), lambda i: (i, 0, 0))
    kv_shape = jax.ShapeDtypeStruct((m, nkv, HEAD), F32)
    vt_spec = pl.BlockSpec((1, nkv * HEAD, tm), lambda i: (i, 0, 0))
    vt_shape = jax.ShapeDtypeStruct((m // tm, nkv * HEAD, tm), BF16)
    flat = lambda wdt, dt: (rowspec(wdt), jax.ShapeDtypeStruct((m, wdt), dt))
    outs = [flat(nq * HEAD, BF16), flat(nidx * HEAD, BF16), (kv_spec, kv_shape), (kv_spec, kv_shape),
            flat(HEAD, F32), flat(nidx, F32), flat(nkv * HEAD, BF16), (vt_spec, vt_shape), flat(HEAD, BF16)]
    return pl.pallas_call(
        kern,
        grid=(m // tm,),
        in_specs=[rowspec(wm), rowspec(wt), tabspec, tabspec, tabspec, vec, vec, vec, vec],
        out_specs=[o[0] for o in outs],
        out_shape=[o[1] for o in outs],
        compiler_params=_cparams(("parallel",)),
        name="dsa_post",
    )(proj_main, proj_tail, *tables, qn.reshape(1, HEAD), kn.reshape(1, HEAD),
      kg.reshape(1, HEAD), kb.reshape(1, HEAD))


def _sort_key(x):
    bits = pltpu.bitcast(x, jnp.int32)
    return jnp.where(bits < 0, bits ^ jnp.int32(0x7FFFFFFF), bits)


def _kth_largest_key(count_ge, rows, k):
    def body(it, thr):
        bit = lax.shift_left(jnp.int32(1), jnp.int32(31) - it)
        trial = thr + bit
        return jnp.where(count_ge(trial) >= k, trial, thr)

    shape = rows if isinstance(rows, tuple) else (rows, 1)
    init = jnp.full(shape, jnp.iinfo(jnp.int32).min, jnp.int32)
    return lax.fori_loop(0, 32, body, init)


def _dsa_prompt_kernel(qi_ref, wit_ref, kib_ref, q_ref, kb_ref, vt_ref, o_ref, key_ref, bias_ref,
                       *, tq, nidx, nq, nkv, topk, scale):
    i = pl.program_id(1)
    ntile = i + 1
    keyi = lax.broadcasted_iota(jnp.int32, (tq, tq), 0)
    qryi = lax.broadcasted_iota(jnp.int32, (tq, tq), 1)
    tri = keyi <= qryi

    def idx_tile(j, carry):
        kt = kib_ref[pl.ds(pl.multiple_of(j * tq, tq), tq), :]
        acc = jnp.zeros((tq, tq), F32)
        for h in range(nidx):
            d = _dot_nt(kt, qi_ref[:, h * HEAD:(h + 1) * HEAD])
            acc = acc + jnp.maximum(d, 0.0) * wit_ref[h:h + 1, :]
        acc = jnp.where(tri | (j < i), acc, NEG_BIG)
        key_ref[j] = _sort_key(acc)
        return carry

    lax.fori_loop(0, ntile, idx_tile, 0)

    def count_ge(trial):
        def tile(j, cnt):
            return cnt + jnp.sum(jnp.where(key_ref[j] >= trial, 1.0, 0.0), axis=0, keepdims=True)

        return lax.fori_loop(0, ntile, tile, jnp.zeros((1, tq), F32))

    thr = _kth_largest_key(count_ge, (1, tq), float(topk))

    def bias_tile(j, carry):
        keep = (key_ref[j] >= thr) & (tri | (j < i))
        bias_ref[j] = jnp.where(keep, 0.0, NEG_BIG)
        return carry

    lax.fori_loop(0, ntile, bias_tile, 0)

    c2 = scale * math.log2(math.e)
    grp = nq // nkv
    for n in range(nkv):
        qs = [q_ref[:, (n * grp + g) * HEAD:(n * grp + g + 1) * HEAD] for g in range(grp)]

        def att_tile(j, carry, n=n, qs=qs):
            kt = kb_ref[pl.ds(pl.multiple_of(j * tq, tq), tq), n * HEAD:(n + 1) * HEAD]
            vt = vt_ref[j, n * HEAD:(n + 1) * HEAD, :]
            bias = bias_ref[j]
            ss = [_dot_nt(kt, qs[g]) + bias for g in range(grp)]
            mid = []
            for g in range(grp):
                m_i, l_i, _ = carry[g]
                m_new = jnp.maximum(m_i, jnp.max(ss[g], axis=0, keepdims=True))
                p = jnp.exp2((ss[g] - m_new) * c2)
                alpha = jnp.exp2((m_i - m_new) * c2)
                mid.append((m_new, alpha * l_i + jnp.sum(p, axis=0, keepdims=True), alpha, p.astype(BF16)))
            pvs = [_dot(vt, mid[g][3]) for g in range(grp)]
            return tuple((mid[g][0], mid[g][1], mid[g][2] * carry[g][2] + pvs[g]) for g in range(grp))

        one = (jnp.full((1, tq), NEG_BIG, F32), jnp.zeros((1, tq), F32), jnp.zeros((HEAD, tq), F32))
        fin = lax.fori_loop(0, ntile, att_tile, (one,) * grp)
        for g in range(grp):
            h = n * grp + g
            o_ref[:, h * HEAD:(h + 1) * HEAD] = (fin[g][2] / fin[g][1]).T.astype(o_ref.dtype)


def dsa_prompt_attend(q_bf, qi_bf, wi_t, ki_bf, k_bf, vt_bf, *, bsz, t, nq, nkv, nidx):
    tq = DSA_TQ
    nqb = t // tq
    topk = min(TOPK_MAX, t // 4)
    assert topk <= tq and t % tq == 0
    qrow = lambda wdt: pl.BlockSpec((tq, wdt), lambda b, i: (b * nqb + i, 0))
    brow = lambda wdt: pl.BlockSpec((t, wdt), lambda b, i: (b, 0))
    kern = functools.partial(_dsa_prompt_kernel, tq=tq, nidx=nidx, nq=nq, nkv=nkv, topk=topk,
                             scale=float(HEAD ** -0.5))
    return pl.pallas_call(
        kern,
        grid=(bsz, nqb),
        in_specs=[qrow(nidx * HEAD), pl.BlockSpec((nidx, tq), lambda b, i: (0, b * nqb + i)), brow(HEAD),
                  qrow(nq * HEAD), brow(nkv * HEAD),
                  pl.BlockSpec((nqb, nkv * HEAD, tq), lambda b, i: (b, 0, 0))],
        out_specs=qrow(nq * HEAD),
        out_shape=jax.ShapeDtypeStruct((bsz * t, nq * HEAD), BF16),
        scratch_shapes=[pltpu.VMEM((nqb, tq, tq), jnp.int32), pltpu.VMEM((nqb, tq, tq), F32)],
        compiler_params=_cparams(("parallel", "arbitrary")),
        name="dsa_prompt",
    )(qi_bf, wi_t, ki_bf, q_bf, k_bf, vt_bf)


def _dsa_s_scores_kernel(pt_ref, qi_ref, wi_ref, *rest, ts, nidx, nstep, npp):
    page_refs, new_ref, sc_ref = rest[:npp], rest[npp], rest[npp + 1]
    p = pl.program_id(1)
    is_new = p == nstep
    psz = new_ref.shape[1]
    first = jnp.where(is_new, new_ref[0], page_refs[0][...])
    kcat = jnp.concatenate([first] + [page_refs[u][...] for u in range(1, npp)], axis=0).astype(BF16)
    d = _dot_nt(qi_ref[...], kcat)
    r = jnp.maximum(d, 0.0) * wi_ref[...]
    rows = [jnp.sum(r[tt * nidx:(tt + 1) * nidx, :], axis=0, keepdims=True) for tt in range(ts)]
    sc = jnp.concatenate(rows + [jnp.full((V7X_SUBLANES - ts, npp * psz), NEG_BIG, F32)], axis=0)
    rowi = lax.broadcasted_iota(jnp.int32, sc.shape, 0)
    coli = lax.broadcasted_iota(jnp.int32, sc.shape, 1)
    sc_ref[0] = jnp.where(is_new & (coli > rowi), NEG_BIG, sc)


def _dsa_s_select_kernel(sc_ref, bias_ref, *, topk, past):
    key = _sort_key(sc_ref[...])
    rows = key.shape[0]

    def count_ge(trial):
        return jnp.sum(jnp.where(key >= trial, 1.0, 0.0), axis=1, keepdims=True)

    thr = _kth_largest_key(count_ge, rows, float(topk))
    rowi = lax.broadcasted_iota(jnp.int32, key.shape, 0)
    coli = lax.broadcasted_iota(jnp.int32, key.shape, 1)
    visible = coli <= past + (rowi % V7X_SUBLANES)
    bias_ref[...] = jnp.where((key >= thr) & visible, 0.0, NEG_BIG)


def _dsa_s_attend_kernel(pt_ref, q_ref, bias_ref, *rest, nkv, nstep, npp, scale):
    kp_refs, vp_refs = rest[:npp], rest[npp:2 * npp]
    kn_ref, vn_ref, o_ref, m_ref, l_ref, acc_ref = rest[2 * npp:]
    p = pl.program_id(1)
    is_new = p == nstep

    @pl.when(p == 0)
    def _():
        m_ref[...] = jnp.full_like(m_ref, NEG_BIG)
        l_ref[...] = jnp.zeros_like(l_ref)
        acc_ref[...] = jnp.zeros_like(acc_ref)

    def head(refs, new_ref, n):
        first = jnp.where(is_new, new_ref[0, :, n, :], refs[0][:, n, :])
        return jnp.concatenate([first] + [refs[u][:, n, :] for u in range(1, npp)], axis=0).astype(BF16)

    bias = bias_ref[0]
    c2 = scale * math.log2(math.e)
    old = [(m_ref[n], l_ref[n], acc_ref[n]) for n in range(nkv)]
    ss = [_dot_nt(q_ref[0, n], head(kp_refs, kn_ref, n)) + bias for n in range(nkv)]
    mid = []
    for n in range(nkv):
        m_i, l_i, _ = old[n]
        m_new = jnp.maximum(m_i, jnp.max(ss[n], axis=1, keepdims=True))
        pr = jnp.exp2((ss[n] - m_new) * c2)
        alpha = jnp.exp2((m_i - m_new) * c2)
        mid.append((m_new, alpha * l_i + jnp.sum(pr, axis=1, keepdims=True), alpha, pr.astype(BF16)))
    pvs = [_dot(mid[n][3], head(vp_refs, vn_ref, n)) for n in range(nkv)]
    new = [(mid[n][0], mid[n][1], mid[n][2] * old[n][2] + pvs[n]) for n in range(nkv)]
    for n in range(nkv):
        m_ref[n], l_ref[n], acc_ref[n] = new[n]

    @pl.when(is_new)
    def _():
        for n in range(nkv):
            o_ref[0, n] = new[n][2] / new[n][1]


def dsa_sample_attend(q_bf, qi_bf, wi, ki_new, k_new, v_new, ck, cv, cki, page_table,
                      *, layer_j, bsz, ts, nq, nkv, nidx):
    psz = ck.shape[2]
    npages = page_table.shape[1]
    npp = DSA_PAGES_PER_STEP
    assert npages % npp == 0
    nstep = npages // npp
    past = npages * psz
    topk = min(TOPK_MAX, (past + ts) // 4)
    grp = nq // nkv
    ncol = (nstep + 1) * npp * psz
    sub = V7X_SUBLANES
    padrows = lambda x, tail: jnp.pad(x.reshape((bsz, ts) + tail), ((0, 0), (0, psz - ts)) + ((0, 0),) * len(tail))

    def page_idx(u, ndim):
        def index(b, p, pt):
            return (layer_j, pt[b, jnp.minimum(p * npp + u, npages - 1)]) + (0,) * ndim
        return index

    scores = pl.pallas_call(
        functools.partial(_dsa_s_scores_kernel, ts=ts, nidx=nidx, nstep=nstep, npp=npp),
        grid_spec=pltpu.PrefetchScalarGridSpec(
            num_scalar_prefetch=1,
            grid=(bsz, nstep + 1),
            in_specs=[pl.BlockSpec((ts * nidx, HEAD), lambda b, p, pt: (b, 0)),
                      pl.BlockSpec((ts * nidx, 1), lambda b, p, pt: (b, 0))]
                     + [pl.BlockSpec((None, None, psz, HEAD), page_idx(u, 2)) for u in range(npp)]
                     + [pl.BlockSpec((1, psz, HEAD), lambda b, p, pt: (b, 0, 0))],
            out_specs=pl.BlockSpec((1, sub, npp * psz), lambda b, p, pt: (b, 0, p))),
        out_shape=jax.ShapeDtypeStruct((bsz, sub, ncol), F32),
        compiler_params=_cparams(("parallel", "arbitrary")),
        name="dsa_s_scores",
    )(page_table, qi_bf.reshape(bsz * ts * nidx, HEAD), wi.reshape(bsz * ts * nidx, 1),
      *([cki] * npp), padrows(ki_new, (HEAD,)))

    bias = pl.pallas_call(
        functools.partial(_dsa_s_select_kernel, topk=topk, past=past),
        out_shape=jax.ShapeDtypeStruct((bsz * sub, ncol), F32),
        compiler_params=pltpu.CompilerParams(vmem_limit_bytes=VMEM_LIMIT),
        name="dsa_s_select",
    )(scores.reshape(bsz * sub, ncol))
    bias = jnp.repeat(bias.reshape(bsz, sub, ncol)[:, :ts], grp, axis=1)

    q5 = q_bf.reshape(bsz, ts, nkv, grp, HEAD).transpose(0, 2, 1, 3, 4).reshape(bsz, nkv, ts * grp, HEAD)
    kv_new_spec = pl.BlockSpec((1, psz, nkv, HEAD), lambda b, p, pt: (b, 0, 0, 0))
    o = pl.pallas_call(
        functools.partial(_dsa_s_attend_kernel, nkv=nkv, nstep=nstep, npp=npp, scale=float(HEAD ** -0.5)),
        grid_spec=pltpu.PrefetchScalarGridSpec(
            num_scalar_prefetch=1,
            grid=(bsz, nstep + 1),
            in_specs=[pl.BlockSpec((1, nkv, ts * grp, HEAD), lambda b, p, pt: (b, 0, 0, 0)),
                      pl.BlockSpec((1, ts * grp, npp * psz), lambda b, p, pt: (b, 0, p))]
                     + [pl.BlockSpec((None, None, psz, nkv, HEAD), page_idx(u, 3)) for u in range(npp)] * 2
                     + [kv_new_spec, kv_new_spec],
            out_specs=pl.BlockSpec((1, nkv, ts * grp, HEAD), lambda b, p, pt: (b, 0, 0, 0)),
            scratch_shapes=[pltpu.VMEM((nkv, ts * grp, 1), F32), pltpu.VMEM((nkv, ts * grp, 1), F32),
                            pltpu.VMEM((nkv, ts * grp, HEAD), F32)]),
        out_shape=jax.ShapeDtypeStruct((bsz, nkv, ts * grp, HEAD), F32),
        compiler_params=_cparams(("parallel", "arbitrary")),
        name="dsa_s_attend",
    )(page_table, q5, bias, *([ck] * npp), *([cv] * npp), padrows(k_new, (nkv, HEAD)), padrows(v_new, (nkv, HEAD)))
    o = o.reshape(bsz, nkv, ts, grp, HEAD).transpose(0, 2, 1, 3, 4).reshape(bsz * ts, nq * HEAD)
    return o.astype(BF16)


def kernel(x_prompt, x_sample, state_hgrn, state_rglru_h, state_rglru_conv, cache_k, cache_v, cache_kidx,
           page_table, p_prompt, p_sample, norm_mix, norm_mlp, norm_pe, w_in_ab, w_out_ab, hgrn_lb_logits,
           hgrn_out_norm, rg_conv_w, rg_conv_b, rg_wa, rg_ba, rg_wx, rg_bx, rg_lambda, w_in_c, w_out_c,
           c_q_norm, c_k_norm, idx_k_ln_g, idx_k_ln_b, w_up, w_down, w_pe, w_pg):
    bp, tp, d = x_prompt.shape
    bs, ts, _ = x_sample.shape
    depth = norm_mix.shape[0]
    a_width = hgrn_out_norm.shape[1]
    b_width = rg_lambda.shape[1]
    nh_a = a_width // HEAD
    nkv = cache_k.shape[3]
    nq = w_out_c.shape[1] // HEAD
    nidx = (w_in_c.shape[2] - nq * HEAD - 2 * nkv * HEAD - HEAD) // (HEAD + 1)
    past = page_table.shape[1] * cache_k.shape[2]
    c_main = nq * HEAD + 2 * nkv * HEAD + nidx * HEAD
    pe_dim = p_prompt.shape[-1]

    hp = x_prompt.reshape(bp * tp, d)
    hs = x_sample.reshape(bs * ts, d)
    tab_p = _rope_tables(jnp.arange(tp, dtype=jnp.int32))
    tab_s = _rope_tables(past + jnp.arange(ts, dtype=jnp.int32))
    tab_s = tuple(jnp.tile(tb, (bs, 1)) for tb in tab_s)
    ts_pad = 16
    pad_s = lambda x: jnp.pad(x.reshape(bs, ts, -1), ((0, 0), (0, ts_pad - ts), (0, 0))).reshape(bs * ts_pad, -1)
    unpad_s = lambda x: x.reshape(bs, ts_pad, -1)[:, :ts].reshape(bs * ts, -1)

    outs = {k: [] for k in ("hg_p", "rh_p", "rc_p", "k_p", "v_p", "ki_p",
                            "hg_s", "rh_s", "rc_s", "k_s", "v_s", "ki_s")}
    for layer in range(depth):
        j = layer // 2
        xpn = rmsnorm_bf16(hp, norm_mix[layer])
        xsn = rmsnorm_bf16(hs, norm_mix[layer])
        if layer % 2 == 0:
            gb_blk = 4 * a_width // b_width
            x_blk = gb_blk + 1
            rg_args = (rg_conv_w[j], rg_conv_b[j], rg_wa[j], rg_ba[j], rg_wx[j], rg_bx[j], rg_lambda[j])

            proj, w_in = matmul_rw(xpn, w_in_ab, j)
            oa, sa = hgrn(proj, hgrn_lb_logits, hgrn_out_norm[j], jnp.zeros((bp, nh_a, HEAD, HEAD), F32),
                          bsz=bp, t=tp, layer_j=j, r=min(128, tp))
            ob, hl, cn = rglru(proj, jnp.zeros((bp, CONV_W - 1, b_width), F32), jnp.zeros((bp, b_width), F32),
                               *rg_args, bsz=bp, t=tp, r=min(256, tp), gb_blk=gb_blk, x_blk=x_blk,
                               first_pos_zero=True)
            hp, w_out = matmul_rw(jnp.concatenate([oa, ob], axis=1), w_out_ab, j, epilogue="residual", res=hp)
            outs["hg_p"].append(sa)
            outs["rh_p"].append(hl.reshape(bp, b_width))
            outs["rc_p"].append(cn)

            proj = pad_s(matmul(xsn, w_in))
            oa, sa = hgrn(proj, hgrn_lb_logits, hgrn_out_norm[j], state_hgrn[j],
                          bsz=bs, t=ts_pad, layer_j=j, r=ts_pad, t_valid=ts)
            ob, hl, cn = rglru(proj, state_rglru_conv[j], state_rglru_h[j], *rg_args, bsz=bs, t=ts_pad,
                               r=ts_pad, gb_blk=gb_blk, x_blk=x_blk, first_pos_zero=False, t_valid=ts)
            mix = unpad_s(jnp.concatenate([oa, ob], axis=1))
            hs = matmul(mix, w_out, epilogue="residual", res=hs)
            outs["hg_s"].append(sa)
            outs["rh_s"].append(hl.reshape(bs, b_width))
            outs["rc_s"].append(cn)
        else:
            w_tail_f32 = jnp.pad(w_in_c[j:j + 1, :, c_main:], ((0, 0), (0, 0), (0, 2 * HEAD - (HEAD + nidx))))
            norms = (c_q_norm[j], c_k_norm[j], idx_k_ln_g[j], idx_k_ln_b[j])

            pm, w_main = matmul_rw(xpn, w_in_c, j, n_out=c_main)
            ptl, w_tail = matmul_rw(xpn, w_tail_f32, 0)
            post = dsa_post(pm, ptl, tab_p, *norms, nq=nq, nkv=nkv, nidx=nidx, t_tab=tp)
            q_bf, qi_bf, kk, vv, ki, wi, k_bf, vt_bf, ki_bf = post
            o = dsa_prompt_attend(q_bf, qi_bf, wi.T, ki_bf, k_bf, vt_bf, bsz=bp, t=tp, nq=nq, nkv=nkv, nidx=nidx)
            hp, w_out = matmul_rw(o, w_out_c, j, epilogue="residual", res=hp)
            outs["k_p"].append(kk.reshape(bp, tp, nkv, HEAD))
            outs["v_p"].append(vv.reshape(bp, tp, nkv, HEAD))
            outs["ki_p"].append(ki.reshape(bp, tp, HEAD))

            post = dsa_post(matmul(xsn, w_main), matmul(xsn, w_tail), tab_s, *norms,
                            nq=nq, nkv=nkv, nidx=nidx, t_tab=bs * ts)
            q_bf, qi_bf, kk, vv, ki, wi, _, _, _ = post
            o = dsa_sample_attend(q_bf, qi_bf, wi, ki, kk, vv, cache_k, cache_v, cache_kidx,
                                  page_table, layer_j=j, bsz=bs, ts=ts, nq=nq, nkv=nkv, nidx=nidx)
            hs = matmul(o, w_out, epilogue="residual", res=hs)
            outs["k_s"].append(kk.reshape(bs, ts, nkv, HEAD))
            outs["v_s"].append(vv.reshape(bs, ts, nkv, HEAD))
            outs["ki_s"].append(ki.reshape(bs, ts, HEAD))

        wd = w_down[layer].astype(BF16)
        pp = p_prompt[layer].reshape(-1, pe_dim).astype(BF16)
        ps = p_sample[layer].reshape(-1, pe_dim).astype(BF16)
        mid, wu = matmul_rw(rmsnorm_bf16(hp, norm_mlp[layer]), w_up, layer, epilogue="relu2", out_dtype=BF16)
        hp = matmul(mid, wd, epilogue="residual", res=hp)
        hp, wpg, wpe = matmul_rw(rmsnorm_bf16(hp, norm_pe[layer]), w_pg, layer, epilogue="pe_gate", res=hp,
                                 p=pp, wpe_stack=w_pe)
        mid = matmul(rmsnorm_bf16(hs, norm_mlp[layer]), wu, epilogue="relu2", out_dtype=BF16)
        hs = matmul(mid, wd, epilogue="residual", res=hs)
        hs = matmul(rmsnorm_bf16(hs, norm_pe[layer]), wpg, epilogue="pe_gate", res=hs, p=ps, w_pe=wpe)

    st = lambda name: jnp.stack(outs[name])
    return (hp.reshape(bp, tp, d), hs.reshape(bs, ts, d),
            st("hg_p"), st("rh_p"), st("rc_p"), st("k_p"), st("v_p"), st("ki_p"),
            st("hg_s"), st("rh_s"), st("rc_s"), st("k_s"), st("v_s"), st("ki_s"))
```

```python
import functools
import math

import jax
import jax.numpy as jnp
import numpy as np
from jax import lax
from jax.experimental import pallas as pl
from jax.experimental.pallas import tpu as pltpu

F32 = jnp.float32
BF16 = jnp.bfloat16

V7X_LANES = 128
V7X_SUBLANES = 8
V7X_VMEM_BYTES = 64 * 1024 * 1024
VMEM_LIMIT = 56 * 1024 * 1024

EPS = 1e-6
NEG_BIG = -1e30
RG_C = 8.0
ROPE_THETA = 500000.0
CONV_W = 4
TOPK_MAX = 256

HEAD = 128
ROPE_HALF = HEAD // 8
DSA_PAGES_PER_STEP = 4
DSA_TQ = 256
HGRN_HEADS_PER_STEP = 4

_NT = (((1,), (1,)), ((), ()))
_TN = (((0,), (0,)), ((), ()))


def _cparams(sem):
    return pltpu.CompilerParams(dimension_semantics=sem, vmem_limit_bytes=VMEM_LIMIT)


def _dot(a, b):
    return jnp.dot(a, b, preferred_element_type=F32)


def _dot_nt(a, b):
    return lax.dot_general(a, b, _NT, preferred_element_type=F32)


def _dot_tn(a, b):
    return lax.dot_general(a, b, _TN, preferred_element_type=F32)


def _rmsnorm_kernel(x_ref, g_ref, o_ref):
    x = x_ref[...]
    y = x * lax.rsqrt(jnp.mean(x * x, axis=-1, keepdims=True) + EPS)
    o_ref[...] = (y * g_ref[...]).astype(o_ref.dtype)


def rmsnorm_bf16(x, g):
    m, d = x.shape
    tm = min(m, 256)
    return pl.pallas_call(
        _rmsnorm_kernel,
        grid=(m // tm,),
        in_specs=[pl.BlockSpec((tm, d), lambda i: (i, 0)),
                  pl.BlockSpec((1, d), lambda i: (0, 0))],
        out_specs=pl.BlockSpec((tm, d), lambda i: (i, 0)),
        out_shape=jax.ShapeDtypeStruct((m, d), BF16),
        compiler_params=_cparams(("parallel",)),
        name="rmsnorm",
    )(x, g.reshape(1, d))


def _mm_kernel(*refs, nk, epilogue):
    x_ref, w_ref = refs[0], refs[1]
    pos = 2
    res_ref = p_ref = wpe_ref = None
    if epilogue in ("residual", "pe_gate"):
        res_ref = refs[pos]
        pos += 1
    if epilogue == "pe_gate":
        p_ref, wpe_ref = refs[pos], refs[pos + 1]
        pos += 2
    o_ref = refs[pos]
    acc_ref = refs[pos + 1] if nk > 1 else None

    def finish(acc):
        if epilogue == "none":
            out = acc
        elif epilogue == "relu2":
            r = jnp.maximum(acc, 0.0)
            out = r * r
        elif epilogue == "residual":
            out = res_ref[...] + acc
        else:
            pe = _dot(p_ref[...], wpe_ref[...])
            out = res_ref[...] + pe * jax.nn.sigmoid(acc)
        o_ref[...] = out.astype(o_ref.dtype)

    if nk == 1:
        finish(_dot(x_ref[...], w_ref[...]))
    else:
        k = pl.program_id(2)

        @pl.when(k == 0)
        def _():
            acc_ref[...] = jnp.zeros_like(acc_ref)

        acc_ref[...] += _dot(x_ref[...], w_ref[...])

        @pl.when(k == nk - 1)
        def _():
            finish(acc_ref[...])


def _mm_tiles(m, n, k):
    tm = min(m, 1024)
    if m <= 64:
        tn = min(n, 2048)
    else:
        tn = min(n, 512)
    while n % tn:
        tn //= 2
    tk = min(k, 4096)
    return tm, tn, tk


def matmul(x, w, *, layer=None, epilogue="none", res=None, p=None, w_pe=None, out_dtype=F32):
    m, k = x.shape
    n = w.shape[-1]
    tm, tn, tk = _mm_tiles(m, n, k)
    nk = k // tk
    if nk == 1:
        grid = (m // tm, n // tn)
        xmap = lambda i, j: (i, 0)
        wmap = lambda i, j: (0, j)
        omap = lambda i, j: (i, j)
        pmap = lambda i, j: (i, 0)
        sem = ("parallel", "parallel")
    else:
        grid = (m // tm, n // tn, nk)
        xmap = lambda i, j, kk: (i, kk)
        wmap = lambda i, j, kk: (kk, j)
        omap = lambda i, j, kk: (i, j)
        pmap = lambda i, j, kk: (i, 0)
        sem = ("parallel", "parallel", "arbitrary")
    if layer is None:
        wspec = pl.BlockSpec((tk, tn), wmap)
    else:
        wspec = pl.BlockSpec((None, tk, tn), lambda *g: (layer,) + wmap(*g))
    in_specs = [pl.BlockSpec((tm, tk), xmap), wspec]
    args = [x, w]
    if epilogue in ("residual", "pe_gate"):
        in_specs.append(pl.BlockSpec((tm, tn), omap))
        args.append(res)
    if epilogue == "pe_gate":
        pe_dim = p.shape[1]
        in_specs.append(pl.BlockSpec((tm, pe_dim), pmap))
        in_specs.append(pl.BlockSpec((pe_dim, tn), wmap if nk == 1 else (lambda i, j, kk: (0, j))))
        args += [p, w_pe]
    scratch = [pltpu.VMEM((tm, tn), F32)] if nk > 1 else []
    return pl.pallas_call(
        functools.partial(_mm_kernel, nk=nk, epilogue=epilogue),
        grid=grid,
        in_specs=in_specs,
        out_specs=pl.BlockSpec((tm, tn), omap),
        out_shape=jax.ShapeDtypeStruct((m, n), out_dtype),
        scratch_shapes=scratch,
        compiler_params=_cparams(sem),
        name="mm_" + epilogue,
    )(*args)


def _mm_rw_kernel(*refs, epilogue):
    x_ref, w_ref = refs[0], refs[1]
    pos = 2
    res_ref = p_ref = wpe_ref = wpeb_ref = None
    if epilogue in ("residual", "pe_gate"):
        res_ref = refs[pos]
        pos += 1
    if epilogue == "pe_gate":
        p_ref, wpe_ref = refs[pos], refs[pos + 1]
        pos += 2
    o_ref, wb_ref = refs[pos], refs[pos + 1]
    if epilogue == "pe_gate":
        wpeb_ref = refs[pos + 2]

    @pl.when(pl.program_id(1) == 0)
    def _():
        wb_ref[...] = w_ref[...].astype(BF16)
        if epilogue == "pe_gate":
            wpeb_ref[...] = wpe_ref[...].astype(BF16)

    acc = _dot(x_ref[...], wb_ref[...])
    if epilogue == "none":
        out = acc
    elif epilogue == "relu2":
        r = jnp.maximum(acc, 0.0)
        out = r * r
    elif epilogue == "residual":
        out = res_ref[...] + acc
    else:
        out = res_ref[...] + _dot(p_ref[...], wpeb_ref[...]) * jax.nn.sigmoid(acc)
    o_ref[...] = out.astype(o_ref.dtype)


def matmul_rw(x, w_stack, layer, *, n_out=None, col0=0, epilogue="none", res=None, p=None, wpe_stack=None,
              out_dtype=F32):
    m, k = x.shape
    n = w_stack.shape[2] if n_out is None else n_out
    tm = min(m, 1024)
    tn = min(n, 512)
    assert n % tn == 0 and m % tm == 0 and col0 % tn == 0
    jb = col0 // tn
    in_specs = [pl.BlockSpec((tm, k), lambda j, i: (i, 0)),
                pl.BlockSpec((None, k, tn), lambda j, i: (layer, 0, jb + j))]
    args = [x, w_stack]
    out_specs = [pl.BlockSpec((tm, tn), lambda j, i: (i, j)), pl.BlockSpec((k, tn), lambda j, i: (0, j))]
    out_shape = [jax.ShapeDtypeStruct((m, n), out_dtype), jax.ShapeDtypeStruct((k, n), BF16)]
    if epilogue in ("residual", "pe_gate"):
        in_specs.append(pl.BlockSpec((tm, tn), lambda j, i: (i, j)))
        args.append(res)
    if epilogue == "pe_gate":
        pe_dim = p.shape[1]
        in_specs.append(pl.BlockSpec((tm, pe_dim), lambda j, i: (i, 0)))
        in_specs.append(pl.BlockSpec((None, pe_dim, tn), lambda j, i: (layer, 0, j)))
        args += [p, wpe_stack]
        out_specs.append(pl.BlockSpec((pe_dim, tn), lambda j, i: (0, j)))
        out_shape.append(jax.ShapeDtypeStruct((pe_dim, n), BF16))
    return pl.pallas_call(
        functools.partial(_mm_rw_kernel, epilogue=epilogue),
        grid=(n // tn, m // tm),
        in_specs=in_specs,
        out_specs=out_specs,
        out_shape=out_shape,
        compiler_params=_cparams(("parallel", "arbitrary")),
        name="mmrw_" + epilogue,
    )(*args)


def _hgrn_level_matrices(r):
    idx = np.arange(r)
    t = idx[:, None]
    u = idx[None, :]
    mats = []
    lvl = 2
    while lvl <= r:
        h = lvl // 2
        pos = t % lvl
        mid = t - pos + h
        upper = pos >= h
        m_up = (u >= mid) & (u <= t)
        m_lo = (u > t) & (u <= mid - 1)
        mats.append(np.where(upper, m_up, m_lo))
        lvl *= 2
    mats.append(u <= t)
    mats.append(u > t)
    return np.concatenate(mats, axis=0).astype(np.float32)


def _hgrn_kernel(q_ref, f_ref, i_ref, ga_ref, lbl_ref, an_ref, s0_ref, mall_ref,
                 o_ref, s_ref, st_ref, *, r, nhb, layer_j, t_valid):
    c = pl.program_id(2)
    nc = pl.num_programs(2)
    nlev = int(math.log2(r))

    @pl.when(c == 0)
    def _():
        for hh in range(nhb):
            st_ref[hh] = s0_ref[0, hh].T

    lbl = lbl_ref[...]
    e = jnp.exp(lbl - jnp.max(lbl, axis=0, keepdims=True))
    soft = e / jnp.sum(e, axis=0, keepdims=True)
    lb_all = jnp.zeros((1, nhb * HEAD), F32)
    for jj in range(1, layer_j + 1):
        lb_all = lb_all + soft[jj:jj + 1, :]

    z = f_ref[...]
    logf = jnp.log1p(lb_all * jnp.exp(-z)) - _softplus(-z)
    kk_all = (1.0 - lb_all) * jax.nn.sigmoid(-z)
    qq_all = jax.nn.silu(q_ref[...])
    vv_all = i_ref[...]
    if t_valid is not None:
        live = lax.broadcasted_iota(jnp.int32, (r, 1), 0) < t_valid
        logf = jnp.where(live, logf, 0.0)
        kk_all = jnp.where(live, kk_all, 0.0)
        qq_all = jnp.where(live, qq_all, 0.0)
        vv_all = jnp.where(live, vv_all, 0.0)

    hi = logf.astype(BF16)
    mid = (logf - hi.astype(F32)).astype(BF16)
    xs_all = _dot(mall_ref[...], hi) + _dot(mall_ref[...], mid)

    ti = lax.broadcasted_iota(jnp.int32, (r, r), 0)
    si = lax.broadcasted_iota(jnp.int32, (r, r), 1)
    diag = ti == si
    pairs = []
    for lv in range(nlev):
        blk = 2 << lv
        h = blk // 2
        pairs.append(((ti >> (lv + 1)) == (si >> (lv + 1))) & ((ti & (blk - 1)) >= h) & ((si & (blk - 1)) < h))

    ga = ga_ref[...]
    an = an_ref[...]
    sls = [slice(hh * HEAD, (hh + 1) * HEAD) for hh in range(nhb)]
    xc0, xe0 = nlev * r, (nlev + 1) * r
    sts = [st_ref[hh] for hh in range(nhb)]
    vbs = [vv_all[:, sl].astype(BF16) for sl in sls]
    o_st = [_dot_nt((qq_all[:, sl] * jnp.exp(xs_all[xc0:xc0 + r, sl])).astype(BF16), sts[hh].astype(BF16))
            for hh, sl in enumerate(sls)]
    upd = [_dot_tn(vbs[hh], (kk_all[:, sl] * jnp.exp(xs_all[xe0:xe0 + r, sl])).astype(BF16))
           for hh, sl in enumerate(sls)]
    atts = []
    for hh, sl in enumerate(sls):
        qq, kk = qq_all[:, sl], kk_all[:, sl]
        att = jnp.where(diag, _dot_nt(qq.astype(BF16), kk.astype(BF16)), 0.0)
        for lv in range(nlev):
            w = jnp.exp(xs_all[lv * r:(lv + 1) * r, sl])
            att = att + jnp.where(pairs[lv], _dot_nt((qq * w).astype(BF16), (kk * w).astype(BF16)), 0.0)
        atts.append(att.astype(BF16))
    for hh, sl in enumerate(sls):
        o = _dot(atts[hh], vbs[hh]) + o_st[hh]
        g_end = jnp.exp(xs_all[xc0 + r - 1:xc0 + r, sl])
        st_ref[hh] = g_end * sts[hh] + upd[hh]
        on = o * lax.rsqrt(jnp.mean(o * o, axis=-1, keepdims=True) + EPS) * an[:, sl]
        o_ref[:, sl] = (on * jax.nn.silu(ga[:, sl])).astype(o_ref.dtype)

    @pl.when(c == nc - 1)
    def _():
        for hh in range(nhb):
            s_ref[0, hh] = st_ref[hh].T


def hgrn(proj, lb_logits, a_norm, s0, *, bsz, t, layer_j, r, t_valid=None):
    width = a_norm.shape[0]
    nh = width // HEAD
    nhb = HGRN_HEADS_PER_STEP
    ng = nh // nhb
    n_ab = lb_logits.shape[0]
    nchunk = t // r
    mall = jnp.asarray(_hgrn_level_matrices(r), BF16)
    bw = nhb * HEAD
    col = lambda off: (lambda b, h, c: (b * nchunk + c, off * ng + h))
    kern = functools.partial(_hgrn_kernel, r=r, nhb=nhb, layer_j=layer_j, t_valid=t_valid)
    return pl.pallas_call(
        kern,
        grid=(bsz, ng, nchunk),
        in_specs=[pl.BlockSpec((r, bw), col(0)),
                  pl.BlockSpec((r, bw), col(1)),
                  pl.BlockSpec((r, bw), col(2)),
                  pl.BlockSpec((r, bw), col(3)),
                  pl.BlockSpec((n_ab, bw), lambda b, h, c: (0, h)),
                  pl.BlockSpec((1, bw), lambda b, h, c: (0, h)),
                  pl.BlockSpec((1, nhb, HEAD, HEAD), lambda b, h, c: (b, h, 0, 0)),
                  pl.BlockSpec(mall.shape, lambda b, h, c: (0, 0))],
        out_specs=[pl.BlockSpec((r, bw), lambda b, h, c: (b * nchunk + c, h)),
                   pl.BlockSpec((1, nhb, HEAD, HEAD), lambda b, h, c: (b, h, 0, 0))],
        out_shape=[jax.ShapeDtypeStruct((bsz * t, width), BF16),
                   jax.ShapeDtypeStruct((bsz, nh, HEAD, HEAD), F32)],
        scratch_shapes=[pltpu.VMEM((nhb, HEAD, HEAD), F32)],
        compiler_params=_cparams(("parallel", "parallel", "arbitrary")),
        name="hgrn",
    )(proj, proj, proj, proj, lb_logits, a_norm.reshape(1, width), s0, mall)


def _softplus(x):
    return jnp.maximum(x, 0.0) + jnp.log1p(jnp.exp(-jnp.abs(x)))


def _rglru_kernel(x_ref, gb_ref, c0_ref, h0_ref, cw_ref, cb_ref, wa_ref, ba_ref, wx_ref, bx_ref,
                  lam_ref, o_ref, hl_ref, cn_ref, xp_ref, a_ref, b_ref, hc_ref,
                  *, r, nblk, first_pos_zero, t_valid):
    i = pl.program_id(1)
    ni = pl.num_programs(1)
    pad = V7X_SUBLANES

    @pl.when(i == 0)
    def _():
        xp_ref[0:pad, :] = c0_ref[0]
        hc_ref[...] = h0_ref[0]

    x = x_ref[...]
    xp_ref[pad:pad + r, :] = x
    cw = cw_ref[...]
    y = cw[0:1, :] * xp_ref[pad - 3:pad - 3 + r, :]
    y = y + cw[1:2, :] * xp_ref[pad - 2:pad - 2 + r, :]
    y = y + cw[2:3, :] * xp_ref[pad - 1:pad - 1 + r, :]
    y = cb_ref[...] + (y + cw[3:4, :] * x)

    last = r if t_valid is None else t_valid
    cn_ref[0] = xp_ref[pad + last - 3:pad + last, :]
    xp_ref[0:pad, :] = xp_ref[r:r + pad, :]

    ra = []
    rx = []
    for n in range(nblk):
        ys = y[:, n * HEAD:(n + 1) * HEAD].astype(BF16)
        ra.append(_dot(ys, wa_ref[n]))
        rx.append(_dot(ys, wx_ref[n]))
    rg = jax.nn.sigmoid(jnp.concatenate(ra, axis=1) + ba_ref[...])
    gi = jax.nn.sigmoid(jnp.concatenate(rx, axis=1) + bx_ref[...])
    log_a = (-RG_C * rg) * _softplus(-lam_ref[...])
    a = jnp.exp(log_a)
    th = jnp.tanh(log_a)
    mult = jnp.sqrt((-2.0 * th) / (1.0 - th))
    rowi = lax.broadcasted_iota(jnp.int32, (r, 1), 0)
    if first_pos_zero:
        mult = jnp.where((rowi == 0) & (i == 0), 1.0, mult)
    bterm = mult * gi * y
    if t_valid is not None:
        live = rowi < t_valid
        a = jnp.where(live, a, 1.0)
        bterm = jnp.where(live, bterm, 0.0)
    a_ref[...] = a
    b_ref[...] = bterm

    def step(tt, h):
        h = a_ref[pl.ds(tt, 1), :] * h + b_ref[pl.ds(tt, 1), :]
        b_ref[pl.ds(tt, 1), :] = h
        return h

    h_last = lax.fori_loop(0, r, step, hc_ref[...], unroll=8)
    hc_ref[...] = h_last
    o_ref[...] = (jax.nn.gelu(gb_ref[...]) * b_ref[...]).astype(o_ref.dtype)

    @pl.when(i == ni - 1)
    def _():
        hl_ref[0] = h_last


def rglru(proj, conv0, h0, conv_w, conv_b, wa, ba, wx, bx, lam, *, bsz, t, r, gb_blk, x_blk,
          first_pos_zero, t_valid=None):
    w = lam.shape[0]
    nblk = wa.shape[0]
    nrb = t // r
    pad = V7X_SUBLANES
    c0 = jnp.pad(conv0, ((0, 0), (pad - (CONV_W - 1), 0), (0, 0)))
    row = lambda v: v.reshape(1, w)
    kern = functools.partial(_rglru_kernel, r=r, nblk=nblk, first_pos_zero=first_pos_zero, t_valid=t_valid)
    full = lambda shape: pl.BlockSpec(shape, lambda b, i: (0,) * len(shape))
    return pl.pallas_call(
        kern,
        grid=(bsz, nrb),
        in_specs=[pl.BlockSpec((r, w), lambda b, i: (b * nrb + i, x_blk)),
                  pl.BlockSpec((r, w), lambda b, i: (b * nrb + i, gb_blk)),
                  pl.BlockSpec((1, pad, w), lambda b, i: (b, 0, 0)),
                  pl.BlockSpec((1, 1, w), lambda b, i: (b, 0, 0)),
                  full((CONV_W, w)), full((1, w)),
                  full(wa.shape), full((1, w)), full(wx.shape), full((1, w)), full((1, w))],
        out_specs=[pl.BlockSpec((r, w), lambda b, i: (b * nrb + i, 0)),
                   pl.BlockSpec((1, 1, w), lambda b, i: (b, 0, 0)),
                   pl.BlockSpec((1, CONV_W - 1, w), lambda b, i: (b, 0, 0))],
        out_shape=[jax.ShapeDtypeStruct((bsz * t, w), BF16),
                   jax.ShapeDtypeStruct((bsz, 1, w), F32),
                   jax.ShapeDtypeStruct((bsz, CONV_W - 1, w), F32)],
        scratch_shapes=[pltpu.VMEM((r + pad, w), F32), pltpu.VMEM((r, w), F32),
                        pltpu.VMEM((r, w), F32), pltpu.VMEM((1, w), F32)],
        compiler_params=_cparams(("parallel", "arbitrary")),
        name="rglru",
    )(proj, proj, c0, h0.reshape(bsz, 1, w), conv_w, row(conv_b), wa.astype(BF16), row(ba),
      wx.astype(BF16), row(bx), row(lam))


def _rope_tables(pos):
    rd = HEAD // 4
    half = rd // 2
    inv = jnp.exp(-math.log(ROPE_THETA) * jnp.arange(half, dtype=F32) * (2.0 / rd))
    ang = pos.astype(F32)[:, None] * inv[None, :]
    cos, sin = jnp.cos(ang), jnp.sin(ang)
    n = pos.shape[0]
    ones = jnp.ones((n, HEAD - rd), F32)
    zeros = jnp.zeros((n, HEAD - rd), F32)
    zh = jnp.zeros((n, half), F32)
    c = jnp.concatenate([cos, cos, ones], axis=1)
    a = jnp.concatenate([-sin, zh, zeros], axis=1)
    b = jnp.concatenate([zh, sin, zeros], axis=1)
    return c, a, b


def _rope(x, c, a, b):
    return x * c + pltpu.roll(x, HEAD - ROPE_HALF, 1) * a + pltpu.roll(x, ROPE_HALF, 1) * b


def _dsa_post_kernel(pm_ref, pt_ref, c_ref, a_ref, b_ref, qn_ref, kn_ref, kg_ref, kb_ref,
                     q_ref, qi_ref, k_ref, v_ref, ki_ref, wi_ref, kbf_ref, vbf_ref, kibf_ref,
                     *, nq, nkv, nidx, wi_scale):
    c, a, b = c_ref[...], a_ref[...], b_ref[...]
    qn, kn = qn_ref[...], kn_ref[...]
    off = 0
    for h in range(nq):
        x = pm_ref[:, off + h * HEAD:off + (h + 1) * HEAD]
        y = x * lax.rsqrt(jnp.mean(x * x, axis=-1, keepdims=True) + EPS) * qn
        q_ref[:, h * HEAD:(h + 1) * HEAD] = _rope(y, c, a, b).astype(q_ref.dtype)
    off += nq * HEAD
    for h in range(nkv):
        x = pm_ref[:, off + h * HEAD:off + (h + 1) * HEAD]
        y = x * lax.rsqrt(jnp.mean(x * x, axis=-1, keepdims=True) + EPS) * kn
        kr = _rope(y, c, a, b)
        k_ref[:, h, :] = kr
        kbf_ref[:, h * HEAD:(h + 1) * HEAD] = kr.astype(BF16)
    off += nkv * HEAD
    vv = pm_ref[:, off:off + nkv * HEAD]
    for h in range(nkv):
        v_ref[:, h, :] = vv[:, h * HEAD:(h + 1) * HEAD]
    vbf_ref[0] = vv.T.astype(BF16)
    off += nkv * HEAD
    for h in range(nidx):
        x = pm_ref[:, off + h * HEAD:off + (h + 1) * HEAD]
        qi_ref[:, h * HEAD:(h + 1) * HEAD] = _rope(x, c, a, b).astype(qi_ref.dtype)
    x = pt_ref[:, 0:HEAD]
    mu = jnp.mean(x, axis=-1, keepdims=True)
    xc = x - mu
    y = xc * lax.rsqrt(jnp.mean(xc * xc, axis=-1, keepdims=True) + EPS)
    kir = _rope(y * kg_ref[...] + kb_ref[...], c, a, b)
    ki_ref[...] = kir
    kibf_ref[...] = kir.astype(BF16)
    wi_ref[...] = pt_ref[:, HEAD:HEAD + nidx] * wi_scale


def dsa_post(proj_main, proj_tail, tables, qn, kn, kg, kb, *, nq, nkv, nidx, t_tab):
    m = proj_main.shape[0]
    tm = min(m, DSA_TQ, t_tab)
    ntab = t_tab // tm
    wm = proj_main.shape[1]
    wt = proj_tail.shape[1]
    rowspec = lambda wdt: pl.BlockSpec((tm, wdt), lambda i: (i, 0))
    tabspec = pl.BlockSpec((tm, HEAD), lambda i: (i % ntab, 0))
    vec = pl.BlockSpec((1, HEAD), lambda i: (0, 0))
    kern = functools.partial(_dsa_post_kernel, nq=nq, nkv=nkv, nidx=nidx,
                             wi_scale=float((nidx * HEAD) ** -0.5))
    kv_spec = pl.BlockSpec((tm, nkv, HEAD), lambda i: (i, 0, 0))
    kv_shape = jax.ShapeDtypeStruct((m, nkv, HEAD), F32)
    vt_spec = pl.BlockSpec((1, nkv * HEAD, tm), lambda i: (i, 0, 0))
    vt_shape = jax.ShapeDtypeStruct((m // tm, nkv * HEAD, tm), BF16)
    flat = lambda wdt, dt: (rowspec(wdt), jax.ShapeDtypeStruct((m, wdt), dt))
    outs = [flat(nq * HEAD, BF16), flat(nidx * HEAD, BF16), (kv_spec, kv_shape), (kv_spec, kv_shape),
            flat(HEAD, F32), flat(nidx, F32), flat(nkv * HEAD, BF16), (vt_spec, vt_shape), flat(HEAD, BF16)]
    return pl.pallas_call(
        kern,
        grid=(m // tm,),
        in_specs=[rowspec(wm), rowspec(wt), tabspec, tabspec, tabspec, vec, vec, vec, vec],
        out_specs=[o[0] for o in outs],
        out_shape=[o[1] for o in outs],
        compiler_params=_cparams(("parallel",)),
        name="dsa_post",
    )(proj_main, proj_tail, *tables, qn.reshape(1, HEAD), kn.reshape(1, HEAD),
      kg.reshape(1, HEAD), kb.reshape(1, HEAD))


def _sort_key(x):
    bits = pltpu.bitcast(x, jnp.int32)
    return jnp.where(bits < 0, bits ^ jnp.int32(0x7FFFFFFF), bits)


def _kth_largest_key(count_ge, rows, k):
    def body(it, thr):
        bit = lax.shift_left(jnp.int32(1), jnp.int32(31) - it)
        trial = thr + bit
        return jnp.where(count_ge(trial) >= k, trial, thr)

    shape = rows if isinstance(rows, tuple) else (rows, 1)
    init = jnp.full(shape, jnp.iinfo(jnp.int32).min, jnp.int32)
    return lax.fori_loop(0, 32, body, init)


def _dsa_prompt_kernel(qi_ref, wit_ref, kib_ref, q_ref, kb_ref, vt_ref, o_ref, key_ref, bias_ref,
                       *, tq, nidx, nq, nkv, topk, scale):
    i = pl.program_id(1)
    ntile = i + 1
    keyi = lax.broadcasted_iota(jnp.int32, (tq, tq), 0)
    qryi = lax.broadcasted_iota(jnp.int32, (tq, tq), 1)
    tri = keyi <= qryi

    def idx_tile(j, carry):
        kt = kib_ref[pl.ds(pl.multiple_of(j * tq, tq), tq), :]
        acc = jnp.zeros((tq, tq), F32)
        for h in range(nidx):
            d = _dot_nt(kt, qi_ref[:, h * HEAD:(h + 1) * HEAD])
            acc = acc + jnp.maximum(d, 0.0) * wit_ref[h:h + 1, :]
        acc = jnp.where(tri | (j < i), acc, NEG_BIG)
        key_ref[j] = _sort_key(acc)
        return carry

    lax.fori_loop(0, ntile, idx_tile, 0)

    def count_ge(trial):
        def tile(j, cnt):
            return cnt + jnp.sum(jnp.where(key_ref[j] >= trial, 1.0, 0.0), axis=0, keepdims=True)

        return lax.fori_loop(0, ntile, tile, jnp.zeros((1, tq), F32))

    thr = _kth_largest_key(count_ge, (1, tq), float(topk))

    def count_gt(j, cnt):
        return cnt + jnp.sum(jnp.where(key_ref[j] > thr, 1.0, 0.0), axis=0, keepdims=True)

    room = float(topk) - lax.fori_loop(0, ntile, count_gt, jnp.zeros((1, tq), F32))
    lower = jnp.where(qryi <= keyi, 1.0, 0.0).astype(BF16)

    def bias_tile(j, seen):
        k = key_ref[j]
        tie = jnp.where(k == thr, 1.0, 0.0)
        rank = seen + _dot(lower, tie.astype(BF16))
        keep = ((k > thr) | ((k == thr) & (rank <= room))) & (tri | (j < i))
        bias_ref[j] = jnp.where(keep, 0.0, NEG_BIG)
        return seen + jnp.sum(tie, axis=0, keepdims=True)

    lax.fori_loop(0, ntile, bias_tile, jnp.zeros((1, tq), F32))

    c2 = scale * math.log2(math.e)
    grp = nq // nkv
    for n in range(nkv):
        qs = [q_ref[:, (n * grp + g) * HEAD:(n * grp + g + 1) * HEAD] for g in range(grp)]

        def att_tile(j, carry, n=n, qs=qs):
            kt = kb_ref[pl.ds(pl.multiple_of(j * tq, tq), tq), n * HEAD:(n + 1) * HEAD]
            vt = vt_ref[j, n * HEAD:(n + 1) * HEAD, :]
            bias = bias_ref[j]
            ss = [_dot_nt(kt, qs[g]) + bias for g in range(grp)]
            mid = []
            for g in range(grp):
                m_i, l_i, _ = carry[g]
                m_new = jnp.maximum(m_i, jnp.max(ss[g], axis=0, keepdims=True))
                p = jnp.exp2((ss[g] - m_new) * c2)
                alpha = jnp.exp2((m_i - m_new) * c2)
                mid.append((m_new, alpha * l_i + jnp.sum(p, axis=0, keepdims=True), alpha, p.astype(BF16)))
            pvs = [_dot(vt, mid[g][3]) for g in range(grp)]
            return tuple((mid[g][0], mid[g][1], mid[g][2] * carry[g][2] + pvs[g]) for g in range(grp))

        one = (jnp.full((1, tq), NEG_BIG, F32), jnp.zeros((1, tq), F32), jnp.zeros((HEAD, tq), F32))
        fin = lax.fori_loop(0, ntile, att_tile, (one,) * grp)
        for g in range(grp):
            h = n * grp + g
            o_ref[:, h * HEAD:(h + 1) * HEAD] = (fin[g][2] / fin[g][1]).T.astype(o_ref.dtype)


def dsa_prompt_attend(q_bf, qi_bf, wi_t, ki_bf, k_bf, vt_bf, *, bsz, t, nq, nkv, nidx):
    tq = DSA_TQ
    nqb = t // tq
    topk = min(TOPK_MAX, t // 4)
    assert topk <= tq and t % tq == 0
    qrow = lambda wdt: pl.BlockSpec((tq, wdt), lambda b, i: (b * nqb + i, 0))
    brow = lambda wdt: pl.BlockSpec((t, wdt), lambda b, i: (b, 0))
    kern = functools.partial(_dsa_prompt_kernel, tq=tq, nidx=nidx, nq=nq, nkv=nkv, topk=topk,
                             scale=float(HEAD ** -0.5))
    return pl.pallas_call(
        kern,
        grid=(bsz, nqb),
        in_specs=[qrow(nidx * HEAD), pl.BlockSpec((nidx, tq), lambda b, i: (0, b * nqb + i)), brow(HEAD),
                  qrow(nq * HEAD), brow(nkv * HEAD),
                  pl.BlockSpec((nqb, nkv * HEAD, tq), lambda b, i: (b, 0, 0))],
        out_specs=qrow(nq * HEAD),
        out_shape=jax.ShapeDtypeStruct((bsz * t, nq * HEAD), BF16),
        scratch_shapes=[pltpu.VMEM((nqb, tq, tq), jnp.int32), pltpu.VMEM((nqb, tq, tq), F32)],
        compiler_params=_cparams(("parallel", "arbitrary")),
        name="dsa_prompt",
    )(qi_bf, wi_t, ki_bf, q_bf, k_bf, vt_bf)


def _dsa_s_scores_kernel(pt_ref, qi_ref, wi_ref, *rest, ts, nidx, nstep, npp):
    page_refs, new_ref, sc_ref = rest[:npp], rest[npp], rest[npp + 1]
    p = pl.program_id(1)
    is_new = p == nstep
    psz = new_ref.shape[1]
    first = jnp.where(is_new, new_ref[0], page_refs[0][...])
    kcat = jnp.concatenate([first] + [page_refs[u][...] for u in range(1, npp)], axis=0).astype(BF16)
    d = _dot_nt(qi_ref[...], kcat)
    r = jnp.maximum(d, 0.0) * wi_ref[...]
    rows = [jnp.sum(r[tt * nidx:(tt + 1) * nidx, :], axis=0, keepdims=True) for tt in range(ts)]
    sc = jnp.concatenate(rows + [jnp.full((V7X_SUBLANES - ts, npp * psz), NEG_BIG, F32)], axis=0)
    rowi = lax.broadcasted_iota(jnp.int32, sc.shape, 0)
    coli = lax.broadcasted_iota(jnp.int32, sc.shape, 1)
    sc_ref[0] = jnp.where(is_new & (coli > rowi), NEG_BIG, sc)


def _dsa_s_select_kernel(sc_ref, keep_ref, *, topk, past):
    key = _sort_key(sc_ref[...])
    rows, ncol = key.shape

    def count_ge(trial):
        return jnp.sum(jnp.where(key >= trial, 1.0, 0.0), axis=1, keepdims=True)

    thr = _kth_largest_key(count_ge, rows, float(topk))
    above = key > thr
    room = float(topk) - jnp.sum(jnp.where(above, 1.0, 0.0), axis=1, keepdims=True)
    rowi = lax.broadcasted_iota(jnp.int32, key.shape, 0)
    coli = lax.broadcasted_iota(jnp.int32, key.shape, 1)
    visible = coli <= past + (rowi % V7X_SUBLANES)
    li = lax.broadcasted_iota(jnp.int32, (V7X_LANES, V7X_LANES), 0)
    lj = lax.broadcasted_iota(jnp.int32, (V7X_LANES, V7X_LANES), 1)
    upper = jnp.where(li <= lj, 1.0, 0.0).astype(BF16)
    seen = jnp.zeros((rows, 1), F32)
    for c in range(ncol // V7X_LANES):
        sl = slice(c * V7X_LANES, (c + 1) * V7X_LANES)
        tie = jnp.where(key[:, sl] == thr, 1.0, 0.0)
        rank = seen + _dot(tie.astype(BF16), upper)
        keep = (above[:, sl] | ((key[:, sl] == thr) & (rank <= room))) & visible[:, sl]
        keep_ref[:, sl] = jnp.where(keep, 1.0, 0.0)
        seen = seen + jnp.sum(tie, axis=1, keepdims=True)


def _dsa_s_attend_kernel(pt_ref, q_ref, keep_ref, expand_ref, struct_ref, *rest, nstep, npp, scale):
    kp_refs, vp_refs = rest[:npp], rest[npp:2 * npp]
    kn_ref, vn_ref, o_ref, m_ref, l_ref, acc_ref = rest[2 * npp:]
    p = pl.program_id(1)
    is_new = p == nstep
    psz = kn_ref.shape[1]

    @pl.when(p == 0)
    def _():
        m_ref[...] = jnp.full_like(m_ref, NEG_BIG)
        l_ref[...] = jnp.zeros_like(l_ref)
        acc_ref[...] = jnp.zeros_like(acc_ref)

    def flat(refs, new_ref, u):
        page = refs[u][...]
        if u == 0:
            page = jnp.where(is_new, new_ref[0], page)
        return page.reshape(-1, HEAD).astype(BF16)

    c2 = scale * math.log2(math.e)
    q = q_ref[0]
    struct = struct_ref[...]
    ss = []
    for u in range(npp):
        seen = _dot(keep_ref[0, :, u * psz:(u + 1) * psz], expand_ref[...])
        ss.append(_dot_nt(q, flat(kp_refs, kn_ref, u)) + jnp.where(seen * struct > 0.5, 0.0, NEG_BIG))
    m_i = m_ref[...]
    m_new = m_i
    for u in range(npp):
        m_new = jnp.maximum(m_new, jnp.max(ss[u], axis=1, keepdims=True))
    alpha = jnp.exp2((m_i - m_new) * c2)
    l_new = alpha * l_ref[...]
    acc = alpha * acc_ref[...]
    for u in range(npp):
        pr = jnp.exp2((ss[u] - m_new) * c2)
        l_new = l_new + jnp.sum(pr, axis=1, keepdims=True)
        acc = acc + _dot(pr.astype(BF16), flat(vp_refs, vn_ref, u))
    m_ref[...], l_ref[...], acc_ref[...] = m_new, l_new, acc

    @pl.when(is_new)
    def _():
        o_ref[0] = acc / l_new


def dsa_sample_attend(q_bf, qi_bf, wi, ki_new, k_new, v_new, ck, cv, cki, page_table,
                      *, layer_j, bsz, ts, nq, nkv, nidx):
    psz = ck.shape[2]
    npages = page_table.shape[1]
    npp = DSA_PAGES_PER_STEP
    assert npages % npp == 0
    nstep = npages // npp
    past = npages * psz
    topk = min(TOPK_MAX, (past + ts) // 4)
    grp = nq // nkv
    ncol = (nstep + 1) * npp * psz
    sub = V7X_SUBLANES
    padrows = lambda x, tail: jnp.pad(x.reshape((bsz, ts) + tail), ((0, 0), (0, psz - ts)) + ((0, 0),) * len(tail))

    def page_idx(u, ndim):
        def index(b, p, pt):
            return (layer_j, pt[b, jnp.minimum(p * npp + u, npages - 1)]) + (0,) * ndim
        return index

    scores = pl.pallas_call(
        functools.partial(_dsa_s_scores_kernel, ts=ts, nidx=nidx, nstep=nstep, npp=npp),
        grid_spec=pltpu.PrefetchScalarGridSpec(
            num_scalar_prefetch=1,
            grid=(bsz, nstep + 1),
            in_specs=[pl.BlockSpec((ts * nidx, HEAD), lambda b, p, pt: (b, 0)),
                      pl.BlockSpec((ts * nidx, 1), lambda b, p, pt: (b, 0))]
                     + [pl.BlockSpec((None, None, psz, HEAD), page_idx(u, 2)) for u in range(npp)]
                     + [pl.BlockSpec((1, psz, HEAD), lambda b, p, pt: (b, 0, 0))],
            out_specs=pl.BlockSpec((1, sub, npp * psz), lambda b, p, pt: (b, 0, p))),
        out_shape=jax.ShapeDtypeStruct((bsz, sub, ncol), F32),
        compiler_params=_cparams(("parallel", "arbitrary")),
        name="dsa_s_scores",
    )(page_table, qi_bf.reshape(bsz * ts * nidx, HEAD), wi.reshape(bsz * ts * nidx, 1),
      *([cki] * npp), padrows(ki_new, (HEAD,)))

    keep = pl.pallas_call(
        functools.partial(_dsa_s_select_kernel, topk=topk, past=past),
        out_shape=jax.ShapeDtypeStruct((bsz * sub, ncol), F32),
        compiler_params=pltpu.CompilerParams(vmem_limit_bytes=VMEM_LIMIT),
        name="dsa_s_select",
    )(scores.reshape(bsz * sub, ncol))
    nrow = nkv * ts * grp
    keep = jnp.broadcast_to(keep.reshape(bsz, 1, sub, 1, ncol)[:, :, :ts], (bsz, nkv, ts, grp, ncol))
    keep = keep.reshape(bsz, nrow, ncol).astype(BF16)
    q_all = q_bf.reshape(bsz, ts, nkv, grp, HEAD).transpose(0, 2, 1, 3, 4).reshape(bsz, nrow, HEAD)
    flat_col = np.arange(psz * nkv)
    expand = jnp.asarray(np.arange(psz)[:, None] == flat_col[None, :] // nkv, BF16)
    struct = jnp.asarray(np.arange(nrow)[:, None] // (ts * grp) == flat_col[None, :] % nkv, F32)
    kv_new_spec = pl.BlockSpec((1, psz, nkv, HEAD), lambda b, p, pt: (b, 0, 0, 0))
    const = lambda shape: pl.BlockSpec(shape, lambda b, p, pt: (0, 0))
    o = pl.pallas_call(
        functools.partial(_dsa_s_attend_kernel, nstep=nstep, npp=npp, scale=float(HEAD ** -0.5)),
        grid_spec=pltpu.PrefetchScalarGridSpec(
            num_scalar_prefetch=1,
            grid=(bsz, nstep + 1),
            in_specs=[pl.BlockSpec((1, nrow, HEAD), lambda b, p, pt: (b, 0, 0)),
                      pl.BlockSpec((1, nrow, npp * psz), lambda b, p, pt: (b, 0, p)),
                      const(expand.shape), const(struct.shape)]
                     + [pl.BlockSpec((None, None, psz, nkv, HEAD), page_idx(u, 3)) for u in range(npp)] * 2
                     + [kv_new_spec, kv_new_spec],
            out_specs=pl.BlockSpec((1, nrow, HEAD), lambda b, p, pt: (b, 0, 0)),
            scratch_shapes=[pltpu.VMEM((nrow, 1), F32), pltpu.VMEM((nrow, 1), F32),
                            pltpu.VMEM((nrow, HEAD), F32)]),
        out_shape=jax.ShapeDtypeStruct((bsz, nrow, HEAD), F32),
        compiler_params=_cparams(("parallel", "arbitrary")),
        name="dsa_s_attend",
    )(page_table, q_all, keep, expand, struct, *([ck] * npp), *([cv] * npp),
      padrows(k_new, (nkv, HEAD)), padrows(v_new, (nkv, HEAD)))
    o = o.reshape(bsz, nkv, ts, grp, HEAD).transpose(0, 2, 1, 3, 4).reshape(bsz * ts, nq * HEAD)
    return o.astype(BF16)


def kernel(x_prompt, x_sample, state_hgrn, state_rglru_h, state_rglru_conv, cache_k, cache_v, cache_kidx,
           page_table, p_prompt, p_sample, norm_mix, norm_mlp, norm_pe, w_in_ab, w_out_ab, hgrn_lb_logits,
           hgrn_out_norm, rg_conv_w, rg_conv_b, rg_wa, rg_ba, rg_wx, rg_bx, rg_lambda, w_in_c, w_out_c,
           c_q_norm, c_k_norm, idx_k_ln_g, idx_k_ln_b, w_up, w_down, w_pe, w_pg):
    bp, tp, d = x_prompt.shape
    bs, ts, _ = x_sample.shape
    depth = norm_mix.shape[0]
    a_width = hgrn_out_norm.shape[1]
    b_width = rg_lambda.shape[1]
    nh_a = a_width // HEAD
    nkv = cache_k.shape[3]
    nq = w_out_c.shape[1] // HEAD
    nidx = (w_in_c.shape[2] - nq * HEAD - 2 * nkv * HEAD - HEAD) // (HEAD + 1)
    past = page_table.shape[1] * cache_k.shape[2]
    c_main = nq * HEAD + 2 * nkv * HEAD + nidx * HEAD
    pe_dim = p_prompt.shape[-1]

    hp = x_prompt.reshape(bp * tp, d)
    hs = x_sample.reshape(bs * ts, d)
    tab_p = _rope_tables(jnp.arange(tp, dtype=jnp.int32))
    tab_s = _rope_tables(past + jnp.arange(ts, dtype=jnp.int32))
    tab_s = tuple(jnp.tile(tb, (bs, 1)) for tb in tab_s)
    ts_pad = 16
    pad_s = lambda x: jnp.pad(x.reshape(bs, ts, -1), ((0, 0), (0, ts_pad - ts), (0, 0))).reshape(bs * ts_pad, -1)
    unpad_s = lambda x: x.reshape(bs, ts_pad, -1)[:, :ts].reshape(bs * ts, -1)

    outs = {k: [] for k in ("hg_p", "rh_p", "rc_p", "k_p", "v_p", "ki_p",
                            "hg_s", "rh_s", "rc_s", "k_s", "v_s", "ki_s")}
    wd_all = w_down.astype(BF16)
    for layer in range(depth):
        j = layer // 2
        xpn = rmsnorm_bf16(hp, norm_mix[layer])
        xsn = rmsnorm_bf16(hs, norm_mix[layer])
        if layer % 2 == 0:
            gb_blk = 4 * a_width // b_width
            x_blk = gb_blk + 1
            rg_args = (rg_conv_w[j], rg_conv_b[j], rg_wa[j], rg_ba[j], rg_wx[j], rg_bx[j], rg_lambda[j])

            proj, w_in = matmul_rw(xpn, w_in_ab, j)
            oa, sa = hgrn(proj, hgrn_lb_logits, hgrn_out_norm[j], jnp.zeros((bp, nh_a, HEAD, HEAD), F32),
                          bsz=bp, t=tp, layer_j=j, r=min(128, tp))
            ob, hl, cn = rglru(proj, jnp.zeros((bp, CONV_W - 1, b_width), F32), jnp.zeros((bp, b_width), F32),
                               *rg_args, bsz=bp, t=tp, r=min(256, tp), gb_blk=gb_blk, x_blk=x_blk,
                               first_pos_zero=True)
            hp, w_out = matmul_rw(jnp.concatenate([oa, ob], axis=1), w_out_ab, j, epilogue="residual", res=hp)
            outs["hg_p"].append(sa)
            outs["rh_p"].append(hl.reshape(bp, b_width))
            outs["rc_p"].append(cn)

            proj = pad_s(matmul(xsn, w_in))
            oa, sa = hgrn(proj, hgrn_lb_logits, hgrn_out_norm[j], state_hgrn[j],
                          bsz=bs, t=ts_pad, layer_j=j, r=ts_pad, t_valid=ts)
            ob, hl, cn = rglru(proj, state_rglru_conv[j], state_rglru_h[j], *rg_args, bsz=bs, t=ts_pad,
                               r=ts_pad, gb_blk=gb_blk, x_blk=x_blk, first_pos_zero=False, t_valid=ts)
            mix = unpad_s(jnp.concatenate([oa, ob], axis=1))
            hs = matmul(mix, w_out, epilogue="residual", res=hs)
            outs["hg_s"].append(sa)
            outs["rh_s"].append(hl.reshape(bs, b_width))
            outs["rc_s"].append(cn)
        else:
            norms = (c_q_norm[j], c_k_norm[j], idx_k_ln_g[j], idx_k_ln_b[j])

            pm, w_main = matmul_rw(xpn, w_in_c, j, n_out=c_main)
            ptl, w_tail = matmul_rw(xpn, w_in_c, j, n_out=2 * HEAD, col0=c_main)
            post = dsa_post(pm, ptl, tab_p, *norms, nq=nq, nkv=nkv, nidx=nidx, t_tab=tp)
            q_bf, qi_bf, kk, vv, ki, wi, k_bf, vt_bf, ki_bf = post
            o = dsa_prompt_attend(q_bf, qi_bf, wi.T, ki_bf, k_bf, vt_bf, bsz=bp, t=tp, nq=nq, nkv=nkv, nidx=nidx)
            hp, w_out = matmul_rw(o, w_out_c, j, epilogue="residual", res=hp)
            outs["k_p"].append(kk.reshape(bp, tp, nkv, HEAD))
            outs["v_p"].append(vv.reshape(bp, tp, nkv, HEAD))
            outs["ki_p"].append(ki.reshape(bp, tp, HEAD))

            post = dsa_post(matmul(xsn, w_main), matmul(xsn, w_tail), tab_s, *norms,
                            nq=nq, nkv=nkv, nidx=nidx, t_tab=bs * ts)
            q_bf, qi_bf, kk, vv, ki, wi, _, _, _ = post
            o = dsa_sample_attend(q_bf, qi_bf, wi, ki, kk, vv, cache_k, cache_v, cache_kidx,
                                  page_table, layer_j=j, bsz=bs, ts=ts, nq=nq, nkv=nkv, nidx=nidx)
            hs = matmul(o, w_out, epilogue="residual", res=hs)
            outs["k_s"].append(kk.reshape(bs, ts, nkv, HEAD))
            outs["v_s"].append(vv.reshape(bs, ts, nkv, HEAD))
            outs["ki_s"].append(ki.reshape(bs, ts, HEAD))

        pp = p_prompt[layer].reshape(-1, pe_dim).astype(BF16)
        ps = p_sample[layer].reshape(-1, pe_dim).astype(BF16)
        mid, wu = matmul_rw(rmsnorm_bf16(hp, norm_mlp[layer]), w_up, layer, epilogue="relu2", out_dtype=BF16)
        hp = matmul(mid, wd_all, layer=layer, epilogue="residual", res=hp)
        hp, wpg, wpe = matmul_rw(rmsnorm_bf16(hp, norm_pe[layer]), w_pg, layer, epilogue="pe_gate", res=hp,
                                 p=pp, wpe_stack=w_pe)
        mid = matmul(rmsnorm_bf16(hs, norm_mlp[layer]), wu, epilogue="relu2", out_dtype=BF16)
        hs = matmul(mid, wd_all, layer=layer, epilogue="residual", res=hs)
        hs = matmul(rmsnorm_bf16(hs, norm_pe[layer]), wpg, epilogue="pe_gate", res=hs, p=ps, w_pe=wpe)

    st = lambda name: jnp.stack(outs[name])
    return (hp.reshape(bp, tp, d), hs.reshape(bs, ts, d),
            st("hg_p"), st("rh_p"), st("rc_p"), st("k_p"), st("v_p"), st("ki_p"),
            st("hg_s"), st("rh_s"), st("rc_s"), st("k_s"), st("v_s"), st("ki_s"))
```

```python
import functools
import math

import jax
import jax.numpy as jnp
import numpy as np
from jax import lax
from jax.experimental import pallas as pl
from jax.experimental.pallas import tpu as pltpu

F32 = jnp.float32
BF16 = jnp.bfloat16

V7X_LANES = 128
V7X_SUBLANES = 8
V7X_VMEM_BYTES = 64 * 1024 * 1024
VMEM_LIMIT = 56 * 1024 * 1024

EPS = 1e-6
NEG_BIG = -1e30
RG_C = 8.0
ROPE_THETA = 500000.0
CONV_W = 4
TOPK_MAX = 256

HEAD = 128
ROPE_HALF = HEAD // 8
DSA_KV_GROUPS_PER_BODY = 2
DSA_PAGES_PER_STEP = 4
DSA_TQ = 256
HGRN_HEADS_PER_STEP = 4

_NT = (((1,), (1,)), ((), ()))
_TN = (((0,), (0,)), ((), ()))


def _cparams(sem):
    return pltpu.CompilerParams(dimension_semantics=sem, vmem_limit_bytes=VMEM_LIMIT)


def _dot(a, b):
    return jnp.dot(a, b, preferred_element_type=F32)


def _dot_nt(a, b):
    return lax.dot_general(a, b, _NT, preferred_element_type=F32)


def _dot_tn(a, b):
    return lax.dot_general(a, b, _TN, preferred_element_type=F32)


def _rmsnorm_kernel(x_ref, g_ref, o_ref):
    x = x_ref[...]
    y = x * lax.rsqrt(jnp.mean(x * x, axis=-1, keepdims=True) + EPS)
    o_ref[...] = (y * g_ref[...]).astype(o_ref.dtype)


def rmsnorm_bf16(x, g):
    m, d = x.shape
    tm = min(m, 256)
    return pl.pallas_call(
        _rmsnorm_kernel,
        grid=(m // tm,),
        in_specs=[pl.BlockSpec((tm, d), lambda i: (i, 0)),
                  pl.BlockSpec((1, d), lambda i: (0, 0))],
        out_specs=pl.BlockSpec((tm, d), lambda i: (i, 0)),
        out_shape=jax.ShapeDtypeStruct((m, d), BF16),
        compiler_params=_cparams(("parallel",)),
        name="rmsnorm",
    )(x, g.reshape(1, d))


def _mm_kernel(*refs, nk, epilogue):
    x_ref, w_ref = refs[0], refs[1]
    pos = 2
    res_ref = p_ref = wpe_ref = None
    if epilogue in ("residual", "pe_gate"):
        res_ref = refs[pos]
        pos += 1
    if epilogue == "pe_gate":
        p_ref, wpe_ref = refs[pos], refs[pos + 1]
        pos += 2
    o_ref = refs[pos]
    acc_ref = refs[pos + 1] if nk > 1 else None

    def finish(acc):
        if epilogue == "none":
            out = acc
        elif epilogue == "relu2":
            r = jnp.maximum(acc, 0.0)
            out = r * r
        elif epilogue == "residual":
            out = res_ref[...] + acc
        else:
            pe = _dot(p_ref[...], wpe_ref[...])
            out = res_ref[...] + pe * jax.nn.sigmoid(acc)
        o_ref[...] = out.astype(o_ref.dtype)

    if nk == 1:
        finish(_dot(x_ref[...], w_ref[...]))
    else:
        k = pl.program_id(2)

        @pl.when(k == 0)
        def _():
            acc_ref[...] = jnp.zeros_like(acc_ref)

        acc_ref[...] += _dot(x_ref[...], w_ref[...])

        @pl.when(k == nk - 1)
        def _():
            finish(acc_ref[...])


def _mm_tiles(m, n, k, epilogue):
    tk = min(k, 4096)
    if m <= 64:
        return m, min(n, 2048), tk
    if k > tk:
        return 1024, 512, tk
    if epilogue in ("none", "relu2"):
        return min(m, 2048), min(n, 512), tk
    if epilogue == "pe_gate":
        return min(m, 1024), min(n, 512), tk
    return min(m, 1024), min(n, 1024), tk


def matmul(x, w, *, layer=None, n_out=None, col0=0, epilogue="none", res=None, p=None, w_pe=None, pe_layer=None,
           out_dtype=F32):
    m, k = x.shape
    n = w.shape[-1] if n_out is None else n_out
    tm, tn, tk = _mm_tiles(m, n, k, epilogue)
    while n % tn:
        tn //= 2
    assert m % tm == 0 and k % tk == 0 and col0 % tn == 0
    nk = k // tk
    jb = col0 // tn
    lead = () if layer is None else (layer,)
    pe_lead = () if pe_layer is None else (pe_layer,)
    if nk == 1:
        grid = (m // tm, n // tn)
        xmap = lambda i, j: (i, 0)
        wmap = lambda i, j: lead + (0, jb + j)
        omap = lambda i, j: (i, j)
        pmap = lambda i, j: (i, 0)
        pemap = lambda i, j: pe_lead + (0, j)
        sem = ("parallel", "parallel")
    else:
        grid = (m // tm, n // tn, nk)
        xmap = lambda i, j, kk: (i, kk)
        wmap = lambda i, j, kk: lead + (kk, jb + j)
        omap = lambda i, j, kk: (i, j)
        pmap = lambda i, j, kk: (i, 0)
        pemap = lambda i, j, kk: pe_lead + (0, j)
        sem = ("parallel", "parallel", "arbitrary")
    in_specs = [pl.BlockSpec((tm, tk), xmap), pl.BlockSpec((None,) * len(lead) + (tk, tn), wmap)]
    args = [x, w]
    if epilogue in ("residual", "pe_gate"):
        in_specs.append(pl.BlockSpec((tm, tn), omap))
        args.append(res)
    if epilogue == "pe_gate":
        pe_dim = p.shape[1]
        in_specs.append(pl.BlockSpec((tm, pe_dim), pmap))
        in_specs.append(pl.BlockSpec((None,) * len(pe_lead) + (pe_dim, tn), pemap))
        args += [p, w_pe]
    scratch = [pltpu.VMEM((tm, tn), F32)] if nk > 1 else []
    return pl.pallas_call(
        functools.partial(_mm_kernel, nk=nk, epilogue=epilogue),
        grid=grid,
        in_specs=in_specs,
        out_specs=pl.BlockSpec((tm, tn), omap),
        out_shape=jax.ShapeDtypeStruct((m, n), out_dtype),
        scratch_shapes=scratch,
        compiler_params=_cparams(sem),
        name="mm_" + epilogue,
    )(*args)


def _hgrn_level_matrices(r):
    idx = np.arange(r)
    t = idx[:, None]
    u = idx[None, :]
    mats = []
    lvl = 2
    while lvl <= r:
        h = lvl // 2
        pos = t % lvl
        mid = t - pos + h
        upper = pos >= h
        m_up = (u >= mid) & (u <= t)
        m_lo = (u > t) & (u <= mid - 1)
        mats.append(np.where(upper, m_up, m_lo))
        lvl *= 2
    mats.append(u <= t)
    mats.append(u > t)
    return np.concatenate(mats, axis=0).astype(np.float32)


def _hgrn_kernel(q_ref, f_ref, i_ref, ga_ref, lbl_ref, an_ref, s0_ref, mall_ref,
                 o_ref, s_ref, st_ref, *, r, nhb, layer_j, t_valid):
    c = pl.program_id(2)
    nc = pl.num_programs(2)
    nlev = int(math.log2(r))

    @pl.when(c == 0)
    def _():
        for hh in range(nhb):
            st_ref[hh] = s0_ref[0, hh].T

    lbl = lbl_ref[...]
    e = jnp.exp(lbl - jnp.max(lbl, axis=0, keepdims=True))
    soft = e / jnp.sum(e, axis=0, keepdims=True)
    lb_all = jnp.zeros((1, nhb * HEAD), F32)
    for jj in range(1, layer_j + 1):
        lb_all = lb_all + soft[jj:jj + 1, :]

    z = f_ref[...]
    logf = jnp.log1p(lb_all * jnp.exp(-z)) - _softplus(-z)
    kk_all = (1.0 - lb_all) * jax.nn.sigmoid(-z)
    qq_all = jax.nn.silu(q_ref[...])
    vv_all = i_ref[...]
    if t_valid is not None:
        live = lax.broadcasted_iota(jnp.int32, (r, 1), 0) < t_valid
        logf = jnp.where(live, logf, 0.0)
        kk_all = jnp.where(live, kk_all, 0.0)
        qq_all = jnp.where(live, qq_all, 0.0)
        vv_all = jnp.where(live, vv_all, 0.0)

    hi = logf.astype(BF16)
    mid = (logf - hi.astype(F32)).astype(BF16)
    xs_all = _dot(mall_ref[...], hi) + _dot(mall_ref[...], mid)

    ti = lax.broadcasted_iota(jnp.int32, (r, r), 0)
    si = lax.broadcasted_iota(jnp.int32, (r, r), 1)
    diag = ti == si
    rowi = lax.broadcasted_iota(jnp.int32, (r, 1), 0)
    ups, pairs = [], []
    for lv in range(nlev):
        blk = 2 << lv
        h = blk // 2
        ups.append((rowi & (blk - 1)) >= h)
        pairs.append(((ti >> (lv + 1)) == (si >> (lv + 1))) & ((ti & (blk - 1)) >= h) & ((si & (blk - 1)) < h))

    ga = ga_ref[...]
    an = an_ref[...]
    sls = [slice(hh * HEAD, (hh + 1) * HEAD) for hh in range(nhb)]
    xc0, xe0 = nlev * r, (nlev + 1) * r
    sts = [st_ref[hh] for hh in range(nhb)]
    vbs = [vv_all[:, sl].astype(BF16) for sl in sls]
    o_st = [_dot_nt((qq_all[:, sl] * jnp.exp(xs_all[xc0:xc0 + r, sl])).astype(BF16), sts[hh].astype(BF16))
            for hh, sl in enumerate(sls)]
    upd = [_dot_tn(vbs[hh], (kk_all[:, sl] * jnp.exp(xs_all[xe0:xe0 + r, sl])).astype(BF16))
           for hh, sl in enumerate(sls)]
    atts = []
    for hh, sl in enumerate(sls):
        qq, kk = qq_all[:, sl], kk_all[:, sl]
        att = jnp.where(diag, _dot_nt(qq.astype(BF16), kk.astype(BF16)), 0.0)
        for lv in range(nlev):
            w = jnp.exp(xs_all[lv * r:(lv + 1) * r, sl])
            qt = jnp.where(ups[lv], qq * w, 0.0).astype(BF16)
            kt = jnp.where(ups[lv], 0.0, kk * w).astype(BF16)
            att = att + jnp.where(pairs[lv], _dot_nt(qt, kt), 0.0)
        atts.append(att.astype(BF16))
    for hh, sl in enumerate(sls):
        o = _dot(atts[hh], vbs[hh]) + o_st[hh]
        g_end = jnp.exp(xs_all[xc0 + r - 1:xc0 + r, sl])
        st_ref[hh] = g_end * sts[hh] + upd[hh]
        on = o * lax.rsqrt(jnp.mean(o * o, axis=-1, keepdims=True) + EPS) * an[:, sl]
        o_ref[:, sl] = (on * jax.nn.silu(ga[:, sl])).astype(o_ref.dtype)

    @pl.when(c == nc - 1)
    def _():
        for hh in range(nhb):
            s_ref[0, hh] = st_ref[hh].T


def hgrn(proj, lb_logits, a_norm, s0, *, bsz, t, layer_j, r, t_valid=None):
    width = a_norm.shape[0]
    nh = width // HEAD
    nhb = HGRN_HEADS_PER_STEP
    ng = nh // nhb
    n_ab = lb_logits.shape[0]
    nchunk = t // r
    mall = jnp.asarray(_hgrn_level_matrices(r), BF16)
    bw = nhb * HEAD
    col = lambda off: (lambda b, h, c: (b * nchunk + c, off * ng + h))
    kern = functools.partial(_hgrn_kernel, r=r, nhb=nhb, layer_j=layer_j, t_valid=t_valid)
    return pl.pallas_call(
        kern,
        grid=(bsz, ng, nchunk),
        in_specs=[pl.BlockSpec((r, bw), col(0)),
                  pl.BlockSpec((r, bw), col(1)),
                  pl.BlockSpec((r, bw), col(2)),
                  pl.BlockSpec((r, bw), col(3)),
                  pl.BlockSpec((n_ab, bw), lambda b, h, c: (0, h)),
                  pl.BlockSpec((1, bw), lambda b, h, c: (0, h)),
                  pl.BlockSpec((1, nhb, HEAD, HEAD), lambda b, h, c: (b, h, 0, 0)),
                  pl.BlockSpec(mall.shape, lambda b, h, c: (0, 0))],
        out_specs=[pl.BlockSpec((r, bw), lambda b, h, c: (b * nchunk + c, h)),
                   pl.BlockSpec((1, nhb, HEAD, HEAD), lambda b, h, c: (b, h, 0, 0))],
        out_shape=[jax.ShapeDtypeStruct((bsz * t, width), BF16),
                   jax.ShapeDtypeStruct((bsz, nh, HEAD, HEAD), F32)],
        scratch_shapes=[pltpu.VMEM((nhb, HEAD, HEAD), F32)],
        compiler_params=_cparams(("parallel", "parallel", "arbitrary")),
        name="hgrn",
    )(proj, proj, proj, proj, lb_logits, a_norm.reshape(1, width), s0, mall)


def _softplus(x):
    return jnp.maximum(x, 0.0) + jnp.log1p(jnp.exp(-jnp.abs(x)))


def _rglru_kernel(x_ref, gb_ref, c0_ref, h0_ref, cw_ref, cb_ref, wa_ref, ba_ref, wx_ref, bx_ref,
                  lam_ref, o_ref, hl_ref, cn_ref, xp_ref, a_ref, b_ref, hc_ref,
                  *, r, nblk, first_pos_zero, t_valid):
    i = pl.program_id(1)
    ni = pl.num_programs(1)
    pad = V7X_SUBLANES

    @pl.when(i == 0)
    def _():
        xp_ref[0:pad, :] = c0_ref[0]
        hc_ref[...] = h0_ref[0]

    x = x_ref[...]
    xp_ref[pad:pad + r, :] = x
    cw = cw_ref[...]
    y = cw[0:1, :] * xp_ref[pad - 3:pad - 3 + r, :]
    y = y + cw[1:2, :] * xp_ref[pad - 2:pad - 2 + r, :]
    y = y + cw[2:3, :] * xp_ref[pad - 1:pad - 1 + r, :]
    y = cb_ref[...] + (y + cw[3:4, :] * x)

    last = r if t_valid is None else t_valid
    cn_ref[0] = xp_ref[pad + last - 3:pad + last, :]
    xp_ref[0:pad, :] = xp_ref[r:r + pad, :]

    ra = []
    rx = []
    for n in range(nblk):
        ys = y[:, n * HEAD:(n + 1) * HEAD].astype(BF16)
        ra.append(_dot(ys, wa_ref[n]))
        rx.append(_dot(ys, wx_ref[n]))
    rg = jax.nn.sigmoid(jnp.concatenate(ra, axis=1) + ba_ref[...])
    gi = jax.nn.sigmoid(jnp.concatenate(rx, axis=1) + bx_ref[...])
    log_a = (-RG_C * rg) * _softplus(-lam_ref[...])
    a = jnp.exp(log_a)
    th = jnp.tanh(log_a)
    mult = jnp.sqrt((-2.0 * th) / (1.0 - th))
    rowi = lax.broadcasted_iota(jnp.int32, (r, 1), 0)
    if first_pos_zero:
        mult = jnp.where((rowi == 0) & (i == 0), 1.0, mult)
    bterm = mult * gi * y
    if t_valid is not None:
        live = rowi < t_valid
        a = jnp.where(live, a, 1.0)
        bterm = jnp.where(live, bterm, 0.0)
    a_ref[...] = a
    b_ref[...] = bterm

    def step(tt, h):
        h = a_ref[pl.ds(tt, 1), :] * h + b_ref[pl.ds(tt, 1), :]
        b_ref[pl.ds(tt, 1), :] = h
        return h

    h_last = lax.fori_loop(0, r, step, hc_ref[...], unroll=8)
    hc_ref[...] = h_last
    o_ref[...] = (jax.nn.gelu(gb_ref[...]) * b_ref[...]).astype(o_ref.dtype)

    @pl.when(i == ni - 1)
    def _():
        hl_ref[0] = h_last


def rglru(proj, conv0, h0, conv_w, conv_b, wa, ba, wx, bx, lam, *, bsz, t, r, gb_blk, x_blk,
          first_pos_zero, t_valid=None):
    w = lam.shape[0]
    nblk = wa.shape[0]
    nrb = t // r
    pad = V7X_SUBLANES
    c0 = jnp.pad(conv0, ((0, 0), (pad - (CONV_W - 1), 0), (0, 0)))
    row = lambda v: v.reshape(1, w)
    kern = functools.partial(_rglru_kernel, r=r, nblk=nblk, first_pos_zero=first_pos_zero, t_valid=t_valid)
    full = lambda shape: pl.BlockSpec(shape, lambda b, i: (0,) * len(shape))
    return pl.pallas_call(
        kern,
        grid=(bsz, nrb),
        in_specs=[pl.BlockSpec((r, w), lambda b, i: (b * nrb + i, x_blk)),
                  pl.BlockSpec((r, w), lambda b, i: (b * nrb + i, gb_blk)),
                  pl.BlockSpec((1, pad, w), lambda b, i: (b, 0, 0)),
                  pl.BlockSpec((1, 1, w), lambda b, i: (b, 0, 0)),
                  full((CONV_W, w)), full((1, w)),
                  full(wa.shape), full((1, w)), full(wx.shape), full((1, w)), full((1, w))],
        out_specs=[pl.BlockSpec((r, w), lambda b, i: (b * nrb + i, 0)),
                   pl.BlockSpec((1, 1, w), lambda b, i: (b, 0, 0)),
                   pl.BlockSpec((1, CONV_W - 1, w), lambda b, i: (b, 0, 0))],
        out_shape=[jax.ShapeDtypeStruct((bsz * t, w), BF16),
                   jax.ShapeDtypeStruct((bsz, 1, w), F32),
                   jax.ShapeDtypeStruct((bsz, CONV_W - 1, w), F32)],
        scratch_shapes=[pltpu.VMEM((r + pad, w), F32), pltpu.VMEM((r, w), F32),
                        pltpu.VMEM((r, w), F32), pltpu.VMEM((1, w), F32)],
        compiler_params=_cparams(("parallel", "arbitrary")),
        name="rglru",
    )(proj, proj, c0, h0.reshape(bsz, 1, w), conv_w, row(conv_b), wa.astype(BF16), row(ba),
      wx.astype(BF16), row(bx), row(lam))


def _rope_tables(pos):
    rd = HEAD // 4
    half = rd // 2
    inv = jnp.exp(-math.log(ROPE_THETA) * jnp.arange(half, dtype=F32) * (2.0 / rd))
    ang = pos.astype(F32)[:, None] * inv[None, :]
    cos, sin = jnp.cos(ang), jnp.sin(ang)
    n = pos.shape[0]
    ones = jnp.ones((n, HEAD - rd), F32)
    zeros = jnp.zeros((n, HEAD - rd), F32)
    zh = jnp.zeros((n, half), F32)
    c = jnp.concatenate([cos, cos, ones], axis=1)
    a = jnp.concatenate([-sin, zh, zeros], axis=1)
    b = jnp.concatenate([zh, sin, zeros], axis=1)
    return c, a, b


def _rope(x, c, a, b):
    return x * c + pltpu.roll(x, HEAD - ROPE_HALF, 1) * a + pltpu.roll(x, ROPE_HALF, 1) * b


def _dsa_post_kernel(pm_ref, pt_ref, c_ref, a_ref, b_ref, qn_ref, kn_ref, kg_ref, kb_ref,
                     q_ref, qi_ref, k_ref, v_ref, ki_ref, wi_ref, kbf_ref, vbf_ref, kibf_ref,
                     *, nq, nkv, nidx, wi_scale):
    c, a, b = c_ref[...], a_ref[...], b_ref[...]
    qn, kn = qn_ref[...], kn_ref[...]
    off = 0
    for h in range(nq):
        x = pm_ref[:, off + h * HEAD:off + (h + 1) * HEAD]
        y = x * lax.rsqrt(jnp.mean(x * x, axis=-1, keepdims=True) + EPS) * qn
        q_ref[:, h * HEAD:(h + 1) * HEAD] = _rope(y, c, a, b).astype(q_ref.dtype)
    off += nq * HEAD
    for h in range(nkv):
        x = pm_ref[:, off + h * HEAD:off + (h + 1) * HEAD]
        y = x * lax.rsqrt(jnp.mean(x * x, axis=-1, keepdims=True) + EPS) * kn
        kr = _rope(y, c, a, b)
        k_ref[:, h, :] = kr
        kbf_ref[:, h * HEAD:(h + 1) * HEAD] = kr.astype(BF16)
    off += nkv * HEAD
    vv = pm_ref[:, off:off + nkv * HEAD]
    for h in range(nkv):
        v_ref[:, h, :] = vv[:, h * HEAD:(h + 1) * HEAD]
    vbf_ref[0] = vv.T.astype(BF16)
    off += nkv * HEAD
    for h in range(nidx):
        x = pm_ref[:, off + h * HEAD:off + (h + 1) * HEAD]
        qi_ref[:, h * HEAD:(h + 1) * HEAD] = _rope(x, c, a, b).astype(qi_ref.dtype)
    x = pt_ref[:, 0:HEAD]
    mu = jnp.mean(x, axis=-1, keepdims=True)
    xc = x - mu
    y = xc * lax.rsqrt(jnp.mean(xc * xc, axis=-1, keepdims=True) + EPS)
    kir = _rope(y * kg_ref[...] + kb_ref[...], c, a, b)
    ki_ref[...] = kir
    kibf_ref[...] = kir.astype(BF16)
    wi_ref[...] = pt_ref[:, HEAD:HEAD + nidx] * wi_scale


def dsa_post(proj_main, proj_tail, tables, qn, kn, kg, kb, *, nq, nkv, nidx, t_tab):
    m = proj_main.shape[0]
    tm = min(m, DSA_TQ, t_tab)
    ntab = t_tab // tm
    wm = proj_main.shape[1]
    wt = proj_tail.shape[1]
    rowspec = lambda wdt: pl.BlockSpec((tm, wdt), lambda i: (i, 0))
    tabspec = pl.BlockSpec((tm, HEAD), lambda i: (i % ntab, 0))
    vec = pl.BlockSpec((1, HEAD), lambda i: (0, 0))
    kern = functools.partial(_dsa_post_kernel, nq=nq, nkv=nkv, nidx=nidx,
                             wi_scale=float((nidx * HEAD) ** -0.5))
    kv_spec = pl.BlockSpec((tm, nkv, HEAD), lambda i: (i, 0, 0))
    kv_shape = jax.ShapeDtypeStruct((m, nkv, HEAD), F32)
    vt_spec = pl.BlockSpec((1, nkv * HEAD, tm), lambda i: (i, 0, 0))
    vt_shape = jax.ShapeDtypeStruct((m // tm, nkv * HEAD, tm), BF16)
    flat = lambda wdt, dt: (rowspec(wdt), jax.ShapeDtypeStruct((m, wdt), dt))
    outs = [flat(nq * HEAD, BF16), flat(nidx * HEAD, BF16), (kv_spec, kv_shape), (kv_spec, kv_shape),
            flat(HEAD, F32), flat(nidx, F32), flat(nkv * HEAD, BF16), (vt_spec, vt_shape), flat(HEAD, BF16)]
    return pl.pallas_call(
        kern,
        grid=(m // tm,),
        in_specs=[rowspec(wm), rowspec(wt), tabspec, tabspec, tabspec, vec, vec, vec, vec],
        out_specs=[o[0] for o in outs],
        out_shape=[o[1] for o in outs],
        compiler_params=_cparams(("parallel",)),
        name="dsa_post",
    )(proj_main, proj_tail, *tables, qn.reshape(1, HEAD), kn.reshape(1, HEAD),
      kg.reshape(1, HEAD), kb.reshape(1, HEAD))


def _sort_key(x):
    bits = pltpu.bitcast(x, jnp.int32)
    return jnp.where(bits < 0, bits ^ jnp.int32(0x7FFFFFFF), bits)


def _kth_largest_key(count_ge, rows, k):
    def body(it, thr):
        bit = lax.shift_left(jnp.int32(1), jnp.int32(31) - it)
        trial = thr + bit
        return jnp.where(count_ge(trial) >= k, trial, thr)

    shape = rows if isinstance(rows, tuple) else (rows, 1)
    init = jnp.full(shape, jnp.iinfo(jnp.int32).min, jnp.int32)
    return lax.fori_loop(0, 32, body, init)


def _dsa_prompt_kernel(qi_ref, wit_ref, kib_ref, q_ref, kb_ref, vt_ref, o_ref, key_ref, bias_ref,
                       *, tq, nidx, nq, nkv, topk, scale):
    i = pl.program_id(1)
    ntile = i + 1
    keyi = lax.broadcasted_iota(jnp.int32, (tq, tq), 0)
    qryi = lax.broadcasted_iota(jnp.int32, (tq, tq), 1)
    tri = keyi <= qryi

    def idx_tile(j, carry):
        kt = kib_ref[pl.ds(pl.multiple_of(j * tq, tq), tq), :]
        acc = jnp.zeros((tq, tq), F32)
        for h in range(nidx):
            d = _dot_nt(kt, qi_ref[:, h * HEAD:(h + 1) * HEAD])
            acc = acc + jnp.maximum(d, 0.0) * wit_ref[h:h + 1, :]
        acc = jnp.where(tri | (j < i), acc, NEG_BIG)
        key_ref[j] = _sort_key(acc)
        return carry

    lax.fori_loop(0, ntile, idx_tile, 0)

    def count_ge(trial):
        def tile(j, cnt):
            return cnt + jnp.sum(jnp.where(key_ref[j] >= trial, 1.0, 0.0), axis=0, keepdims=True)

        return lax.fori_loop(0, ntile, tile, jnp.zeros((1, tq), F32))

    thr = _kth_largest_key(count_ge, (1, tq), float(topk))

    def count_gt(j, cnt):
        return cnt + jnp.sum(jnp.where(key_ref[j] > thr, 1.0, 0.0), axis=0, keepdims=True)

    room = float(topk) - lax.fori_loop(0, ntile, count_gt, jnp.zeros((1, tq), F32))
    lower = jnp.where(qryi <= keyi, 1.0, 0.0).astype(BF16)

    def bias_tile(j, seen):
        k = key_ref[j]
        tie = jnp.where(k == thr, 1.0, 0.0)
        rank = seen + _dot(lower, tie.astype(BF16))
        keep = ((k > thr) | ((k == thr) & (rank <= room))) & (tri | (j < i))
        bias_ref[j] = jnp.where(keep, 0.0, NEG_BIG)
        return seen + jnp.sum(tie, axis=0, keepdims=True)

    lax.fori_loop(0, ntile, bias_tile, jnp.zeros((1, tq), F32))

    c2 = scale * math.log2(math.e)
    grp = nq // nkv
    ngb = DSA_KV_GROUPS_PER_BODY
    for n0 in range(0, nkv, ngb):
        heads = [(n, n * grp + g) for n in range(n0, n0 + ngb) for g in range(grp)]
        qs = [q_ref[:, h * HEAD:(h + 1) * HEAD] for _, h in heads]

        def att_tile(j, carry, n0=n0, heads=heads, qs=qs):
            r0 = pl.multiple_of(j * tq, tq)
            kts = {n: kb_ref[pl.ds(r0, tq), n * HEAD:(n + 1) * HEAD] for n in range(n0, n0 + ngb)}
            vts = {n: vt_ref[j, n * HEAD:(n + 1) * HEAD, :] for n in range(n0, n0 + ngb)}
            bias = bias_ref[j]
            ss = [_dot_nt(kts[n], qs[x]) + bias for x, (n, _) in enumerate(heads)]
            mid = []
            for x in range(len(heads)):
                m_i, l_i, _ = carry[x]
                m_new = jnp.maximum(m_i, jnp.max(ss[x], axis=0, keepdims=True))
                p = jnp.exp2((ss[x] - m_new) * c2)
                alpha = jnp.exp2((m_i - m_new) * c2)
                mid.append((m_new, alpha * l_i + jnp.sum(p, axis=0, keepdims=True), alpha, p.astype(BF16)))
            pvs = [_dot(vts[n], mid[x][3]) for x, (n, _) in enumerate(heads)]
            return tuple((mid[x][0], mid[x][1], mid[x][2] * carry[x][2] + pvs[x]) for x in range(len(heads)))

        one = (jnp.full((1, tq), NEG_BIG, F32), jnp.zeros((1, tq), F32), jnp.zeros((HEAD, tq), F32))
        fin = lax.fori_loop(0, ntile, att_tile, (one,) * len(heads))
        for x, (_, h) in enumerate(heads):
            o_ref[:, h * HEAD:(h + 1) * HEAD] = (fin[x][2] / fin[x][1]).T.astype(o_ref.dtype)


def dsa_prompt_attend(q_bf, qi_bf, wi_t, ki_bf, k_bf, vt_bf, *, bsz, t, nq, nkv, nidx):
    tq = DSA_TQ
    nqb = t // tq
    topk = min(TOPK_MAX, t // 4)
    assert topk <= tq and t % tq == 0
    qrow = lambda wdt: pl.BlockSpec((tq, wdt), lambda b, i: (b * nqb + i, 0))
    brow = lambda wdt: pl.BlockSpec((t, wdt), lambda b, i: (b, 0))
    kern = functools.partial(_dsa_prompt_kernel, tq=tq, nidx=nidx, nq=nq, nkv=nkv, topk=topk,
                             scale=float(HEAD ** -0.5))
    return pl.pallas_call(
        kern,
        grid=(bsz, nqb),
        in_specs=[qrow(nidx * HEAD), pl.BlockSpec((nidx, tq), lambda b, i: (0, b * nqb + i)), brow(HEAD),
                  qrow(nq * HEAD), brow(nkv * HEAD),
                  pl.BlockSpec((nqb, nkv * HEAD, tq), lambda b, i: (b, 0, 0))],
        out_specs=qrow(nq * HEAD),
        out_shape=jax.ShapeDtypeStruct((bsz * t, nq * HEAD), BF16),
        scratch_shapes=[pltpu.VMEM((nqb, tq, tq), jnp.int32), pltpu.VMEM((nqb, tq, tq), F32)],
        compiler_params=_cparams(("parallel", "arbitrary")),
        name="dsa_prompt",
    )(qi_bf, wi_t, ki_bf, q_bf, k_bf, vt_bf)


def _dsa_s_scores_kernel(pt_ref, qi_ref, wi_ref, *rest, ts, nidx, nstep, npp):
    page_refs, new_ref, sc_ref = rest[:npp], rest[npp], rest[npp + 1]
    p = pl.program_id(1)
    is_new = p == nstep
    psz = new_ref.shape[1]
    first = jnp.where(is_new, new_ref[0], page_refs[0][...])
    kcat = jnp.concatenate([first] + [page_refs[u][...] for u in range(1, npp)], axis=0).astype(BF16)
    d = _dot_nt(qi_ref[...], kcat)
    r = jnp.maximum(d, 0.0) * wi_ref[...]
    rows = [jnp.sum(r[tt * nidx:(tt + 1) * nidx, :], axis=0, keepdims=True) for tt in range(ts)]
    sc = jnp.concatenate(rows + [jnp.full((V7X_SUBLANES - ts, npp * psz), NEG_BIG, F32)], axis=0)
    rowi = lax.broadcasted_iota(jnp.int32, sc.shape, 0)
    coli = lax.broadcasted_iota(jnp.int32, sc.shape, 1)
    sc_ref[0] = jnp.where(is_new & (coli > rowi), NEG_BIG, sc)


def _dsa_s_select_kernel(sc_ref, keep_ref, *, topk, past):
    key = _sort_key(sc_ref[...])
    rows, ncol = key.shape

    def count_ge(trial):
        return jnp.sum(jnp.where(key >= trial, 1.0, 0.0), axis=1, keepdims=True)

    thr = _kth_largest_key(count_ge, rows, float(topk))
    above = key > thr
    room = float(topk) - jnp.sum(jnp.where(above, 1.0, 0.0), axis=1, keepdims=True)
    rowi = lax.broadcasted_iota(jnp.int32, key.shape, 0)
    coli = lax.broadcasted_iota(jnp.int32, key.shape, 1)
    visible = coli <= past + (rowi % V7X_SUBLANES)
    li = lax.broadcasted_iota(jnp.int32, (V7X_LANES, V7X_LANES), 0)
    lj = lax.broadcasted_iota(jnp.int32, (V7X_LANES, V7X_LANES), 1)
    upper = jnp.where(li <= lj, 1.0, 0.0).astype(BF16)
    seen = jnp.zeros((rows, 1), F32)
    for c in range(ncol // V7X_LANES):
        sl = slice(c * V7X_LANES, (c + 1) * V7X_LANES)
        tie = jnp.where(key[:, sl] == thr, 1.0, 0.0)
        rank = seen + _dot(tie.astype(BF16), upper)
        keep = (above[:, sl] | ((key[:, sl] == thr) & (rank <= room))) & visible[:, sl]
        keep_ref[:, sl] = jnp.where(keep, 1.0, 0.0)
        seen = seen + jnp.sum(tie, axis=1, keepdims=True)


def _dsa_s_attend_kernel(pt_ref, q_ref, keep_ref, expand_ref, struct_ref, *rest, nstep, npp, scale):
    kp_refs, vp_refs = rest[:npp], rest[npp:2 * npp]
    kn_ref, vn_ref, o_ref, m_ref, l_ref, acc_ref = rest[2 * npp:]
    p = pl.program_id(1)
    is_new = p == nstep
    psz = kn_ref.shape[1]

    @pl.when(p == 0)
    def _():
        m_ref[...] = jnp.full_like(m_ref, NEG_BIG)
        l_ref[...] = jnp.zeros_like(l_ref)
        acc_ref[...] = jnp.zeros_like(acc_ref)

    def flat(refs, new_ref, u):
        page = refs[u][...]
        if u == 0:
            page = jnp.where(is_new, new_ref[0], page)
        return page.reshape(-1, HEAD).astype(BF16)

    c2 = scale * math.log2(math.e)
    q = q_ref[0]
    struct = struct_ref[...]
    ss = []
    for u in range(npp):
        seen = _dot(keep_ref[0, :, u * psz:(u + 1) * psz], expand_ref[...])
        ss.append(_dot_nt(q, flat(kp_refs, kn_ref, u)) + jnp.where(seen * struct > 0.5, 0.0, NEG_BIG))
    m_i = m_ref[...]
    m_new = m_i
    for u in range(npp):
        m_new = jnp.maximum(m_new, jnp.max(ss[u], axis=1, keepdims=True))
    alpha = jnp.exp2((m_i - m_new) * c2)
    l_new = alpha * l_ref[...]
    acc = alpha * acc_ref[...]
    for u in range(npp):
        pr = jnp.exp2((ss[u] - m_new) * c2)
        l_new = l_new + jnp.sum(pr, axis=1, keepdims=True)
        acc = acc + _dot(pr.astype(BF16), flat(vp_refs, vn_ref, u))
    m_ref[...], l_ref[...], acc_ref[...] = m_new, l_new, acc

    @pl.when(is_new)
    def _():
        o_ref[0] = acc / l_new


def dsa_sample_attend(q_bf, qi_bf, wi, ki_new, k_new, v_new, ck, cv, cki, page_table,
                      *, layer_j, bsz, ts, nq, nkv, nidx):
    psz = ck.shape[2]
    npages = page_table.shape[1]
    npp = DSA_PAGES_PER_STEP
    assert npages % npp == 0
    nstep = npages // npp
    past = npages * psz
    topk = min(TOPK_MAX, (past + ts) // 4)
    grp = nq // nkv
    ncol = (nstep + 1) * npp * psz
    sub = V7X_SUBLANES
    padrows = lambda x, tail: jnp.pad(x.reshape((bsz, ts) + tail), ((0, 0), (0, psz - ts)) + ((0, 0),) * len(tail))

    def page_idx(u, ndim):
        def index(b, p, pt):
            return (layer_j, pt[b, jnp.minimum(p * npp + u, npages - 1)]) + (0,) * ndim
        return index

    scores = pl.pallas_call(
        functools.partial(_dsa_s_scores_kernel, ts=ts, nidx=nidx, nstep=nstep, npp=npp),
        grid_spec=pltpu.PrefetchScalarGridSpec(
            num_scalar_prefetch=1,
            grid=(bsz, nstep + 1),
            in_specs=[pl.BlockSpec((ts * nidx, HEAD), lambda b, p, pt: (b, 0)),
                      pl.BlockSpec((ts * nidx, 1), lambda b, p, pt: (b, 0))]
                     + [pl.BlockSpec((None, None, psz, HEAD), page_idx(u, 2)) for u in range(npp)]
                     + [pl.BlockSpec((1, psz, HEAD), lambda b, p, pt: (b, 0, 0))],
            out_specs=pl.BlockSpec((1, sub, npp * psz), lambda b, p, pt: (b, 0, p))),
        out_shape=jax.ShapeDtypeStruct((bsz, sub, ncol), F32),
        compiler_params=_cparams(("parallel", "arbitrary")),
        name="dsa_s_scores",
    )(page_table, qi_bf.reshape(bsz * ts * nidx, HEAD), wi.reshape(bsz * ts * nidx, 1),
      *([cki] * npp), padrows(ki_new, (HEAD,)))

    keep = pl.pallas_call(
        functools.partial(_dsa_s_select_kernel, topk=topk, past=past),
        out_shape=jax.ShapeDtypeStruct((bsz * sub, ncol), F32),
        compiler_params=pltpu.CompilerParams(vmem_limit_bytes=VMEM_LIMIT),
        name="dsa_s_select",
    )(scores.reshape(bsz * sub, ncol))
    nrow = nkv * ts * grp
    keep = jnp.broadcast_to(keep.reshape(bsz, 1, sub, 1, ncol)[:, :, :ts], (bsz, nkv, ts, grp, ncol))
    keep = keep.reshape(bsz, nrow, ncol).astype(BF16)
    q_all = q_bf.reshape(bsz, ts, nkv, grp, HEAD).transpose(0, 2, 1, 3, 4).reshape(bsz, nrow, HEAD)
    flat_col = np.arange(psz * nkv)
    expand = jnp.asarray(np.arange(psz)[:, None] == flat_col[None, :] // nkv, BF16)
    struct = jnp.asarray(np.arange(nrow)[:, None] // (ts * grp) == flat_col[None, :] % nkv, F32)
    kv_new_spec = pl.BlockSpec((1, psz, nkv, HEAD), lambda b, p, pt: (b, 0, 0, 0))
    const = lambda shape: pl.BlockSpec(shape, lambda b, p, pt: (0, 0))
    o = pl.pallas_call(
        functools.partial(_dsa_s_attend_kernel, nstep=nstep, npp=npp, scale=float(HEAD ** -0.5)),
        grid_spec=pltpu.PrefetchScalarGridSpec(
            num_scalar_prefetch=1,
            grid=(bsz, nstep + 1),
            in_specs=[pl.BlockSpec((1, nrow, HEAD), lambda b, p, pt: (b, 0, 0)),
                      pl.BlockSpec((1, nrow, npp * psz), lambda b, p, pt: (b, 0, p)),
                      const(expand.shape), const(struct.shape)]
                     + [pl.BlockSpec((None, None, psz, nkv, HEAD), page_idx(u, 3)) for u in range(npp)] * 2
                     + [kv_new_spec, kv_new_spec],
            out_specs=pl.BlockSpec((1, nrow, HEAD), lambda b, p, pt: (b, 0, 0)),
            scratch_shapes=[pltpu.VMEM((nrow, 1), F32), pltpu.VMEM((nrow, 1), F32),
                            pltpu.VMEM((nrow, HEAD), F32)]),
        out_shape=jax.ShapeDtypeStruct((bsz, nrow, HEAD), F32),
        compiler_params=_cparams(("parallel", "arbitrary")),
        name="dsa_s_attend",
    )(page_table, q_all, keep, expand, struct, *([ck] * npp), *([cv] * npp),
      padrows(k_new, (nkv, HEAD)), padrows(v_new, (nkv, HEAD)))
    o = o.reshape(bsz, nkv, ts, grp, HEAD).transpose(0, 2, 1, 3, 4).reshape(bsz * ts, nq * HEAD)
    return o.astype(BF16)


def kernel(x_prompt, x_sample, state_hgrn, state_rglru_h, state_rglru_conv, cache_k, cache_v, cache_kidx,
           page_table, p_prompt, p_sample, norm_mix, norm_mlp, norm_pe, w_in_ab, w_out_ab, hgrn_lb_logits,
           hgrn_out_norm, rg_conv_w, rg_conv_b, rg_wa, rg_ba, rg_wx, rg_bx, rg_lambda, w_in_c, w_out_c,
           c_q_norm, c_k_norm, idx_k_ln_g, idx_k_ln_b, w_up, w_down, w_pe, w_pg):
    bp, tp, d = x_prompt.shape
    bs, ts, _ = x_sample.shape
    depth = norm_mix.shape[0]
    a_width = hgrn_out_norm.shape[1]
    b_width = rg_lambda.shape[1]
    nh_a = a_width // HEAD
    nkv = cache_k.shape[3]
    nq = w_out_c.shape[1] // HEAD
    nidx = (w_in_c.shape[2] - nq * HEAD - 2 * nkv * HEAD - HEAD) // (HEAD + 1)
    past = page_table.shape[1] * cache_k.shape[2]
    c_main = nq * HEAD + 2 * nkv * HEAD + nidx * HEAD
    pe_dim = p_prompt.shape[-1]

    hp = x_prompt.reshape(bp * tp, d)
    hs = x_sample.reshape(bs * ts, d)
    tab_p = _rope_tables(jnp.arange(tp, dtype=jnp.int32))
    tab_s = _rope_tables(past + jnp.arange(ts, dtype=jnp.int32))
    tab_s = tuple(jnp.tile(tb, (bs, 1)) for tb in tab_s)
    ts_pad = 16
    pad_s = lambda x: jnp.pad(x.reshape(bs, ts, -1), ((0, 0), (0, ts_pad - ts), (0, 0))).reshape(bs * ts_pad, -1)
    unpad_s = lambda x: x.reshape(bs, ts_pad, -1)[:, :ts].reshape(bs * ts, -1)

    outs = {k: [] for k in ("hg_p", "rh_p", "rc_p", "k_p", "v_p", "ki_p",
                            "hg_s", "rh_s", "rc_s", "k_s", "v_s", "ki_s")}
    wb = {name: w.astype(BF16) for name, w in (("in_ab", w_in_ab), ("out_ab", w_out_ab), ("in_c", w_in_c),
                                               ("out_c", w_out_c), ("up", w_up), ("down", w_down),
                                               ("pe", w_pe), ("pg", w_pg))}
    pp_all = p_prompt.reshape(depth, -1, pe_dim).astype(BF16)
    ps_all = p_sample.reshape(depth, -1, pe_dim).astype(BF16)
    for layer in range(depth):
        j = layer // 2
        xpn = rmsnorm_bf16(hp, norm_mix[layer])
        xsn = rmsnorm_bf16(hs, norm_mix[layer])
        if layer % 2 == 0:
            gb_blk = 4 * a_width // b_width
            x_blk = gb_blk + 1
            rg_args = (rg_conv_w[j], rg_conv_b[j], rg_wa[j], rg_ba[j], rg_wx[j], rg_bx[j], rg_lambda[j])

            proj = matmul(xpn, wb["in_ab"], layer=j)
            oa, sa = hgrn(proj, hgrn_lb_logits, hgrn_out_norm[j], jnp.zeros((bp, nh_a, HEAD, HEAD), F32),
                          bsz=bp, t=tp, layer_j=j, r=min(128, tp))
            ob, hl, cn = rglru(proj, jnp.zeros((bp, CONV_W - 1, b_width), F32), jnp.zeros((bp, b_width), F32),
                               *rg_args, bsz=bp, t=tp, r=min(256, tp), gb_blk=gb_blk, x_blk=x_blk,
                               first_pos_zero=True)
            hp = matmul(jnp.concatenate([oa, ob], axis=1), wb["out_ab"], layer=j, epilogue="residual", res=hp)
            outs["hg_p"].append(sa)
            outs["rh_p"].append(hl.reshape(bp, b_width))
            outs["rc_p"].append(cn)

            proj = pad_s(matmul(xsn, wb["in_ab"], layer=j))
            oa, sa = hgrn(proj, hgrn_lb_logits, hgrn_out_norm[j], state_hgrn[j],
                          bsz=bs, t=ts_pad, layer_j=j, r=ts_pad, t_valid=ts)
            ob, hl, cn = rglru(proj, state_rglru_conv[j], state_rglru_h[j], *rg_args, bsz=bs, t=ts_pad,
                               r=ts_pad, gb_blk=gb_blk, x_blk=x_blk, first_pos_zero=False, t_valid=ts)
            mix = unpad_s(jnp.concatenate([oa, ob], axis=1))
            hs = matmul(mix, wb["out_ab"], layer=j, epilogue="residual", res=hs)
            outs["hg_s"].append(sa)
            outs["rh_s"].append(hl.reshape(bs, b_width))
            outs["rc_s"].append(cn)
        else:
            norms = (c_q_norm[j], c_k_norm[j], idx_k_ln_g[j], idx_k_ln_b[j])
            proj_c = lambda xn: (matmul(xn, wb["in_c"], layer=j, n_out=c_main),
                                 matmul(xn, wb["in_c"], layer=j, n_out=2 * HEAD, col0=c_main))

            post = dsa_post(*proj_c(xpn), tab_p, *norms, nq=nq, nkv=nkv, nidx=nidx, t_tab=tp)
            q_bf, qi_bf, kk, vv, ki, wi, k_bf, vt_bf, ki_bf = post
            o = dsa_prompt_attend(q_bf, qi_bf, wi.T, ki_bf, k_bf, vt_bf, bsz=bp, t=tp, nq=nq, nkv=nkv, nidx=nidx)
            hp = matmul(o, wb["out_c"], layer=j, epilogue="residual", res=hp)
            outs["k_p"].append(kk.reshape(bp, tp, nkv, HEAD))
            outs["v_p"].append(vv.reshape(bp, tp, nkv, HEAD))
            outs["ki_p"].append(ki.reshape(bp, tp, HEAD))

            post = dsa_post(*proj_c(xsn), tab_s, *norms, nq=nq, nkv=nkv, nidx=nidx, t_tab=bs * ts)
            q_bf, qi_bf, kk, vv, ki, wi, _, _, _ = post
            o = dsa_sample_attend(q_bf, qi_bf, wi, ki, kk, vv, cache_k, cache_v, cache_kidx,
                                  page_table, layer_j=j, bsz=bs, ts=ts, nq=nq, nkv=nkv, nidx=nidx)
            hs = matmul(o, wb["out_c"], layer=j, epilogue="residual", res=hs)
            outs["k_s"].append(kk.reshape(bs, ts, nkv, HEAD))
            outs["v_s"].append(vv.reshape(bs, ts, nkv, HEAD))
            outs["ki_s"].append(ki.reshape(bs, ts, HEAD))

        def ffn_pe(h, p_all):
            mid = matmul(rmsnorm_bf16(h, norm_mlp[layer]), wb["up"], layer=layer, epilogue="relu2", out_dtype=BF16)
            h = matmul(mid, wb["down"], layer=layer, epilogue="residual", res=h)
            return matmul(rmsnorm_bf16(h, norm_pe[layer]), wb["pg"], layer=layer, epilogue="pe_gate", res=h,
                          p=p_all[layer], w_pe=wb["pe"], pe_layer=layer)

        hp = ffn_pe(hp, pp_all)
        hs = ffn_pe(hs, ps_all)

    st = lambda name: jnp.stack(outs[name])
    return (hp.reshape(bp, tp, d), hs.reshape(bs, ts, d),
            st("hg_p"), st("rh_p"), st("rc_p"), st("k_p"), st("v_p"), st("ki_p"),
            st("hg_s"), st("rh_s"), st("rc_s"), st("k_s"), st("v_s"), st("ki_s"))
```

```python
import functools
import math

import jax
import jax.numpy as jnp
import numpy as np
from jax import lax
from jax.experimental import pallas as pl
from jax.experimental.pallas import tpu as pltpu

F32 = jnp.float32
BF16 = jnp.bfloat16

V7X_LANES = 128
V7X_SUBLANES = 8
V7X_VMEM_BYTES = 64 * 1024 * 1024
VMEM_LIMIT = 56 * 1024 * 1024

EPS = 1e-6
NEG_BIG = -1e30
RG_C = 8.0
ROPE_THETA = 500000.0
CONV_W = 4
TOPK_MAX = 256

HEAD = 128
ROPE_HALF = HEAD // 8
DSA_KV_GROUPS_PER_BODY = 2
DSA_PAGES_PER_STEP = 4
DSA_TQ = 256
HGRN_HEADS_PER_STEP = 4

_NT = (((1,), (1,)), ((), ()))
_TN = (((0,), (0,)), ((), ()))


def _cparams(sem):
    return pltpu.CompilerParams(dimension_semantics=sem, vmem_limit_bytes=VMEM_LIMIT)


def _dot(a, b):
    return jnp.dot(a, b, preferred_element_type=F32)


def _dot_nt(a, b):
    return lax.dot_general(a, b, _NT, preferred_element_type=F32)


def _dot_tn(a, b):
    return lax.dot_general(a, b, _TN, preferred_element_type=F32)


def _rmsnorm_kernel(x_ref, g_ref, o_ref):
    x = x_ref[...]
    y = x * lax.rsqrt(jnp.mean(x * x, axis=-1, keepdims=True) + EPS)
    o_ref[...] = (y * g_ref[...]).astype(o_ref.dtype)


def rmsnorm_bf16(x, g):
    m, d = x.shape
    tm = min(m, 256)
    return pl.pallas_call(
        _rmsnorm_kernel,
        grid=(m // tm,),
        in_specs=[pl.BlockSpec((tm, d), lambda i: (i, 0)),
                  pl.BlockSpec((1, d), lambda i: (0, 0))],
        out_specs=pl.BlockSpec((tm, d), lambda i: (i, 0)),
        out_shape=jax.ShapeDtypeStruct((m, d), BF16),
        compiler_params=_cparams(("parallel",)),
        name="rmsnorm",
    )(x, g.reshape(1, d))


def _mm_kernel(*refs, nk, epilogue, w_t, emit):
    x_ref, w_ref = refs[0], refs[1]
    pos = 2
    res_ref = p_ref = wpe_ref = None
    if epilogue in ("residual", "pe_gate"):
        res_ref = refs[pos]
        pos += 1
    if epilogue == "pe_gate":
        p_ref, wpe_ref = refs[pos], refs[pos + 1]
        pos += 2
    o_ref = refs[pos]
    pos += 1
    wb_ref = None
    if emit:
        wb_ref = refs[pos]
        pos += 1
    acc_ref = refs[pos] if nk > 1 else None

    def finish(acc):
        if epilogue == "none":
            out = acc
        elif epilogue == "relu2":
            r = jnp.maximum(acc, 0.0)
            out = r * r
        elif epilogue == "residual":
            out = res_ref[...] + acc
        else:
            pe = _dot(p_ref[...], wpe_ref[...].astype(BF16))
            out = res_ref[...] + pe * jax.nn.sigmoid(acc)
        o_ref[...] = out.astype(o_ref.dtype)

    wb = w_ref[...].astype(BF16)
    if emit:
        wb_ref[...] = wb
    prod = _dot_nt(x_ref[...], wb) if w_t else _dot(x_ref[...], wb)
    if nk == 1:
        finish(prod)
    else:
        k = pl.program_id(2)

        @pl.when(k == 0)
        def _():
            acc_ref[...] = jnp.zeros_like(acc_ref)

        acc_ref[...] += prod

        @pl.when(k == nk - 1)
        def _():
            finish(acc_ref[...])


def _mm_tiles(m, n, k, epilogue, w_f32):
    tk = min(k, 4096)
    if m <= 64:
        return m, min(n, 512 if w_f32 else 2048), tk
    if k > tk:
        return 1024, 512, tk
    if w_f32:
        return min(m, 2048), min(n, 256), tk
    if epilogue in ("none", "relu2"):
        return min(m, 2048), min(n, 512), tk
    if epilogue == "pe_gate":
        return min(m, 1024), min(n, 512), tk
    return min(m, 1024), min(n, 1024), tk


def matmul(x, w, *, layer=None, n_out=None, col0=0, w_t=False, emit_bf16=False, epilogue="none", res=None,
           p=None, w_pe=None, pe_layer=None, out_dtype=F32):
    m, k = x.shape
    n = (w.shape[-2] if w_t else w.shape[-1]) if n_out is None else n_out
    tm, tn, tk = _mm_tiles(m, n, k, epilogue, w.dtype == F32)
    while n % tn:
        tn //= 2
    assert m % tm == 0 and k % tk == 0 and col0 % tn == 0
    nk = k // tk
    nj = n // tn
    jb = col0 // tn
    lead = () if layer is None else (layer,)
    pe_lead = () if pe_layer is None else (pe_layer,)
    none = lambda t: (None,) * len(t)
    wtile = (tn, tk) if w_t else (tk, tn)
    worder = (lambda kk, jj: (jj, kk)) if w_t else (lambda kk, jj: (kk, jj))
    if nk == 1:
        grid = (m // tm, nj)
        xmap = lambda i, j: (i, 0)
        wmap = lambda i, j: lead + worder(0, jb + j)
        omap = lambda i, j: (i, j)
        pmap = lambda i, j: (i, 0)
        pemap = lambda i, j: pe_lead + (0, j)
        emap = lambda i, j: worder(0, jnp.where(i == 0, j, nj))
        sem = ("arbitrary", "arbitrary") if emit_bf16 else ("parallel", "parallel")
    else:
        assert not emit_bf16
        grid = (m // tm, nj, nk)
        xmap = lambda i, j, kk: (i, kk)
        wmap = lambda i, j, kk: lead + worder(kk, jb + j)
        omap = lambda i, j, kk: (i, j)
        pmap = lambda i, j, kk: (i, 0)
        pemap = lambda i, j, kk: pe_lead + (0, j)
        sem = ("parallel", "parallel", "arbitrary")
    xspec = (pl.BlockSpec((tm, tk), xmap, pipeline_mode=pl.Buffered(1)) if tm * tk * 2 >= 16 * 2**20
             else pl.BlockSpec((tm, tk), xmap))
    in_specs = [xspec, pl.BlockSpec(none(lead) + wtile, wmap)]
    args = [x, w]
    if epilogue in ("residual", "pe_gate"):
        in_specs.append(pl.BlockSpec((tm, tn), omap))
        args.append(res)
    if epilogue == "pe_gate":
        pe_dim = p.shape[1]
        in_specs.append(pl.BlockSpec((tm, pe_dim), pmap))
        in_specs.append(pl.BlockSpec(none(pe_lead) + (pe_dim, tn), pemap))
        args += [p, w_pe]
    out_specs = [pl.BlockSpec((tm, tn), omap)]
    out_shape = [jax.ShapeDtypeStruct((m, n), out_dtype)]
    if emit_bf16:
        out_specs.append(pl.BlockSpec(wtile, emap))
        out_shape.append(jax.ShapeDtypeStruct((n + tn, k) if w_t else (k, n + tn), BF16))
    scratch = [pltpu.VMEM((tm, tn), F32)] if nk > 1 else []
    outs = pl.pallas_call(
        functools.partial(_mm_kernel, nk=nk, epilogue=epilogue, w_t=w_t, emit=emit_bf16),
        grid=grid,
        in_specs=in_specs,
        out_specs=out_specs,
        out_shape=out_shape,
        scratch_shapes=scratch,
        compiler_params=_cparams(sem),
        name="mm_" + epilogue,
    )(*args)
    return outs if emit_bf16 else outs[0]


def _hgrn_level_matrices(r):
    idx = np.arange(r)
    t = idx[:, None]
    u = idx[None, :]
    mats = []
    lvl = 2
    while lvl <= r:
        h = lvl // 2
        pos = t % lvl
        mid = t - pos + h
        upper = pos >= h
        m_up = (u >= mid) & (u <= t)
        m_lo = (u > t) & (u <= mid - 1)
        mats.append(np.where(upper, m_up, m_lo))
        lvl *= 2
    mats.append(u <= t)
    mats.append(u > t)
    return np.concatenate(mats, axis=0).astype(np.float32)


def _hgrn_kernel(q_ref, f_ref, i_ref, ga_ref, lbl_ref, an_ref, s0_ref, mall_ref,
                 o_ref, s_ref, st_ref, *, r, nhb, layer_j, t_valid):
    c = pl.program_id(2)
    nc = pl.num_programs(2)
    nlev = int(math.log2(r))

    @pl.when(c == 0)
    def _():
        for hh in range(nhb):
            st_ref[hh] = s0_ref[0, hh].T

    lbl = lbl_ref[...]
    e = jnp.exp(lbl - jnp.max(lbl, axis=0, keepdims=True))
    soft = e / jnp.sum(e, axis=0, keepdims=True)
    lb_all = jnp.zeros((1, nhb * HEAD), F32)
    for jj in range(1, layer_j + 1):
        lb_all = lb_all + soft[jj:jj + 1, :]

    z = f_ref[...]
    logf = jnp.log1p(lb_all * jnp.exp(-z)) - _softplus(-z)
    kk_all = (1.0 - lb_all) * jax.nn.sigmoid(-z)
    qq_all = jax.nn.silu(q_ref[...])
    vv_all = i_ref[...]
    if t_valid is not None:
        live = lax.broadcasted_iota(jnp.int32, (r, 1), 0) < t_valid
        logf = jnp.where(live, logf, 0.0)
        kk_all = jnp.where(live, kk_all, 0.0)
        qq_all = jnp.where(live, qq_all, 0.0)
        vv_all = jnp.where(live, vv_all, 0.0)

    hi = logf.astype(BF16)
    mid = (logf - hi.astype(F32)).astype(BF16)
    xs_all = _dot(mall_ref[...], hi) + _dot(mall_ref[...], mid)

    ti = lax.broadcasted_iota(jnp.int32, (r, r), 0)
    si = lax.broadcasted_iota(jnp.int32, (r, r), 1)
    diag = ti == si
    rowi = lax.broadcasted_iota(jnp.int32, (r, 1), 0)
    ups, pairs = [], []
    for lv in range(nlev):
        blk = 2 << lv
        h = blk // 2
        ups.append((rowi & (blk - 1)) >= h)
        pairs.append(((ti >> (lv + 1)) == (si >> (lv + 1))) & ((ti & (blk - 1)) >= h) & ((si & (blk - 1)) < h))

    ga = ga_ref[...]
    an = an_ref[...]
    sls = [slice(hh * HEAD, (hh + 1) * HEAD) for hh in range(nhb)]
    xc0, xe0 = nlev * r, (nlev + 1) * r
    sts = [st_ref[hh] for hh in range(nhb)]
    vbs = [vv_all[:, sl].astype(BF16) for sl in sls]
    o_st = [_dot_nt((qq_all[:, sl] * jnp.exp(xs_all[xc0:xc0 + r, sl])).astype(BF16), sts[hh].astype(BF16))
            for hh, sl in enumerate(sls)]
    upd = [_dot_tn(vbs[hh], (kk_all[:, sl] * jnp.exp(xs_all[xe0:xe0 + r, sl])).astype(BF16))
           for hh, sl in enumerate(sls)]
    atts = []
    for hh, sl in enumerate(sls):
        qq, kk = qq_all[:, sl], kk_all[:, sl]
        att = jnp.where(diag, _dot_nt(qq.astype(BF16), kk.astype(BF16)), 0.0)
        for lv in range(nlev):
            w = jnp.exp(xs_all[lv * r:(lv + 1) * r, sl])
            qt = jnp.where(ups[lv], qq * w, 0.0).astype(BF16)
            kt = jnp.where(ups[lv], 0.0, kk * w).astype(BF16)
            att = att + jnp.where(pairs[lv], _dot_nt(qt, kt), 0.0)
        atts.append(att.astype(BF16))
    for hh, sl in enumerate(sls):
        o = _dot(atts[hh], vbs[hh]) + o_st[hh]
        g_end = jnp.exp(xs_all[xc0 + r - 1:xc0 + r, sl])
        st_ref[hh] = g_end * sts[hh] + upd[hh]
        on = o * lax.rsqrt(jnp.mean(o * o, axis=-1, keepdims=True) + EPS) * an[:, sl]
        o_ref[:, sl] = (on * jax.nn.silu(ga[:, sl])).astype(o_ref.dtype)

    @pl.when(c == nc - 1)
    def _():
        for hh in range(nhb):
            s_ref[0, hh] = st_ref[hh].T


def hgrn(proj, lb_logits, a_norm, s0, *, bsz, t, layer_j, r, t_valid=None):
    width = a_norm.shape[0]
    nh = width // HEAD
    nhb = HGRN_HEADS_PER_STEP
    ng = nh // nhb
    n_ab = lb_logits.shape[0]
    nchunk = t // r
    mall = jnp.asarray(_hgrn_level_matrices(r), BF16)
    bw = nhb * HEAD
    col = lambda off: (lambda b, h, c: (b * nchunk + c, off * ng + h))
    kern = functools.partial(_hgrn_kernel, r=r, nhb=nhb, layer_j=layer_j, t_valid=t_valid)
    return pl.pallas_call(
        kern,
        grid=(bsz, ng, nchunk),
        in_specs=[pl.BlockSpec((r, bw), col(0)),
                  pl.BlockSpec((r, bw), col(1)),
                  pl.BlockSpec((r, bw), col(2)),
                  pl.BlockSpec((r, bw), col(3)),
                  pl.BlockSpec((n_ab, bw), lambda b, h, c: (0, h)),
                  pl.BlockSpec((1, bw), lambda b, h, c: (0, h)),
                  pl.BlockSpec((1, nhb, HEAD, HEAD), lambda b, h, c: (b, h, 0, 0)),
                  pl.BlockSpec(mall.shape, lambda b, h, c: (0, 0))],
        out_specs=[pl.BlockSpec((r, bw), lambda b, h, c: (b * nchunk + c, h)),
                   pl.BlockSpec((1, nhb, HEAD, HEAD), lambda b, h, c: (b, h, 0, 0))],
        out_shape=[jax.ShapeDtypeStruct((bsz * t, width), BF16),
                   jax.ShapeDtypeStruct((bsz, nh, HEAD, HEAD), F32)],
        scratch_shapes=[pltpu.VMEM((nhb, HEAD, HEAD), F32)],
        compiler_params=_cparams(("parallel", "parallel", "arbitrary")),
        name="hgrn",
    )(proj, proj, proj, proj, lb_logits, a_norm.reshape(1, width), s0, mall)


def _softplus(x):
    return jnp.maximum(x, 0.0) + jnp.log1p(jnp.exp(-jnp.abs(x)))


def _rglru_kernel(x_ref, gb_ref, c0_ref, h0_ref, cw_ref, cb_ref, wa_ref, ba_ref, wx_ref, bx_ref,
                  lam_ref, o_ref, hl_ref, cn_ref, xp_ref, a_ref, b_ref, hc_ref,
                  *, r, nblk, first_pos_zero, t_valid):
    i = pl.program_id(1)
    ni = pl.num_programs(1)
    pad = V7X_SUBLANES

    @pl.when(i == 0)
    def _():
        xp_ref[0:pad, :] = c0_ref[0]
        hc_ref[...] = h0_ref[0]

    x = x_ref[...]
    xp_ref[pad:pad + r, :] = x
    cw = cw_ref[...]
    y = cw[0:1, :] * xp_ref[pad - 3:pad - 3 + r, :]
    y = y + cw[1:2, :] * xp_ref[pad - 2:pad - 2 + r, :]
    y = y + cw[2:3, :] * xp_ref[pad - 1:pad - 1 + r, :]
    y = cb_ref[...] + (y + cw[3:4, :] * x)

    last = r if t_valid is None else t_valid
    cn_ref[0] = xp_ref[pad + last - 3:pad + last, :]
    xp_ref[0:pad, :] = xp_ref[r:r + pad, :]

    ra = []
    rx = []
    for n in range(nblk):
        ys = y[:, n * HEAD:(n + 1) * HEAD].astype(BF16)
        ra.append(_dot(ys, wa_ref[n]))
        rx.append(_dot(ys, wx_ref[n]))
    rg = jax.nn.sigmoid(jnp.concatenate(ra, axis=1) + ba_ref[...])
    gi = jax.nn.sigmoid(jnp.concatenate(rx, axis=1) + bx_ref[...])
    log_a = (-RG_C * rg) * _softplus(-lam_ref[...])
    a = jnp.exp(log_a)
    th = jnp.tanh(log_a)
    mult = jnp.sqrt((-2.0 * th) / (1.0 - th))
    rowi = lax.broadcasted_iota(jnp.int32, (r, 1), 0)
    if first_pos_zero:
        mult = jnp.where((rowi == 0) & (i == 0), 1.0, mult)
    bterm = mult * gi * y
    if t_valid is not None:
        live = rowi < t_valid
        a = jnp.where(live, a, 1.0)
        bterm = jnp.where(live, bterm, 0.0)
    a_ref[...] = a
    b_ref[...] = bterm

    def step(tt, h):
        h = a_ref[pl.ds(tt, 1), :] * h + b_ref[pl.ds(tt, 1), :]
        b_ref[pl.ds(tt, 1), :] = h
        return h

    h_last = lax.fori_loop(0, r, step, hc_ref[...], unroll=8)
    hc_ref[...] = h_last
    o_ref[...] = (jax.nn.gelu(gb_ref[...]) * b_ref[...]).astype(o_ref.dtype)

    @pl.when(i == ni - 1)
    def _():
        hl_ref[0] = h_last


def rglru(proj, conv0, h0, conv_w, conv_b, wa, ba, wx, bx, lam, *, bsz, t, r, gb_blk, x_blk,
          first_pos_zero, t_valid=None):
    w = lam.shape[0]
    nblk = wa.shape[0]
    nrb = t // r
    pad = V7X_SUBLANES
    c0 = jnp.pad(conv0, ((0, 0), (pad - (CONV_W - 1), 0), (0, 0)))
    row = lambda v: v.reshape(1, w)
    kern = functools.partial(_rglru_kernel, r=r, nblk=nblk, first_pos_zero=first_pos_zero, t_valid=t_valid)
    full = lambda shape: pl.BlockSpec(shape, lambda b, i: (0,) * len(shape))
    return pl.pallas_call(
        kern,
        grid=(bsz, nrb),
        in_specs=[pl.BlockSpec((r, w), lambda b, i: (b * nrb + i, x_blk)),
                  pl.BlockSpec((r, w), lambda b, i: (b * nrb + i, gb_blk)),
                  pl.BlockSpec((1, pad, w), lambda b, i: (b, 0, 0)),
                  pl.BlockSpec((1, 1, w), lambda b, i: (b, 0, 0)),
                  full((CONV_W, w)), full((1, w)),
                  full(wa.shape), full((1, w)), full(wx.shape), full((1, w)), full((1, w))],
        out_specs=[pl.BlockSpec((r, w), lambda b, i: (b * nrb + i, 0)),
                   pl.BlockSpec((1, 1, w), lambda b, i: (b, 0, 0)),
                   pl.BlockSpec((1, CONV_W - 1, w), lambda b, i: (b, 0, 0))],
        out_shape=[jax.ShapeDtypeStruct((bsz * t, w), BF16),
                   jax.ShapeDtypeStruct((bsz, 1, w), F32),
                   jax.ShapeDtypeStruct((bsz, CONV_W - 1, w), F32)],
        scratch_shapes=[pltpu.VMEM((r + pad, w), F32), pltpu.VMEM((r, w), F32),
                        pltpu.VMEM((r, w), F32), pltpu.VMEM((1, w), F32)],
        compiler_params=_cparams(("parallel", "arbitrary")),
        name="rglru",
    )(proj, proj, c0, h0.reshape(bsz, 1, w), conv_w, row(conv_b), wa.astype(BF16), row(ba),
      wx.astype(BF16), row(bx), row(lam))


def _rope_tables(pos):
    rd = HEAD // 4
    half = rd // 2
    inv = jnp.exp(-math.log(ROPE_THETA) * jnp.arange(half, dtype=F32) * (2.0 / rd))
    ang = pos.astype(F32)[:, None] * inv[None, :]
    cos, sin = jnp.cos(ang), jnp.sin(ang)
    n = pos.shape[0]
    ones = jnp.ones((n, HEAD - rd), F32)
    zeros = jnp.zeros((n, HEAD - rd), F32)
    zh = jnp.zeros((n, half), F32)
    c = jnp.concatenate([cos, cos, ones], axis=1)
    a = jnp.concatenate([-sin, zh, zeros], axis=1)
    b = jnp.concatenate([zh, sin, zeros], axis=1)
    return c, a, b


def _rope(x, c, a, b):
    return x * c + pltpu.roll(x, HEAD - ROPE_HALF, 1) * a + pltpu.roll(x, ROPE_HALF, 1) * b


def _dsa_post_kernel(pm_ref, pt_ref, c_ref, a_ref, b_ref, qn_ref, kn_ref, kg_ref, kb_ref,
                     q_ref, qi_ref, k_ref, v_ref, ki_ref, wi_ref, kbf_ref, vbf_ref, kibf_ref,
                     *, nq, nkv, nidx, wi_scale):
    c, a, b = c_ref[...], a_ref[...], b_ref[...]
    qn, kn = qn_ref[...], kn_ref[...]
    off = 0
    for h in range(nq):
        x = pm_ref[:, off + h * HEAD:off + (h + 1) * HEAD]
        y = x * lax.rsqrt(jnp.mean(x * x, axis=-1, keepdims=True) + EPS) * qn
        q_ref[:, h * HEAD:(h + 1) * HEAD] = _rope(y, c, a, b).astype(q_ref.dtype)
    off += nq * HEAD
    for h in range(nkv):
        x = pm_ref[:, off + h * HEAD:off + (h + 1) * HEAD]
        y = x * lax.rsqrt(jnp.mean(x * x, axis=-1, keepdims=True) + EPS) * kn
        kr = _rope(y, c, a, b)
        k_ref[:, h, :] = kr
        kbf_ref[:, h * HEAD:(h + 1) * HEAD] = kr.astype(BF16)
    off += nkv * HEAD
    vv = pm_ref[:, off:off + nkv * HEAD]
    for h in range(nkv):
        v_ref[:, h, :] = vv[:, h * HEAD:(h + 1) * HEAD]
    vbf_ref[0] = vv.T.astype(BF16)
    off += nkv * HEAD
    for h in range(nidx):
        x = pm_ref[:, off + h * HEAD:off + (h + 1) * HEAD]
        qi_ref[:, h * HEAD:(h + 1) * HEAD] = _rope(x, c, a, b).astype(qi_ref.dtype)
    x = pt_ref[:, 0:HEAD]
    mu = jnp.mean(x, axis=-1, keepdims=True)
    xc = x - mu
    y = xc * lax.rsqrt(jnp.mean(xc * xc, axis=-1, keepdims=True) + EPS)
    kir = _rope(y * kg_ref[...] + kb_ref[...], c, a, b)
    ki_ref[...] = kir
    kibf_ref[...] = kir.astype(BF16)
    wi_ref[...] = pt_ref[:, HEAD:HEAD + nidx] * wi_scale


def dsa_post(proj_main, proj_tail, tables, qn, kn, kg, kb, *, nq, nkv, nidx, t_tab):
    m = proj_main.shape[0]
    tm = min(m, DSA_TQ, t_tab)
    ntab = t_tab // tm
    wm = proj_main.shape[1]
    wt = proj_tail.shape[1]
    rowspec = lambda wdt: pl.BlockSpec((tm, wdt), lambda i: (i, 0))
    tabspec = pl.BlockSpec((tm, HEAD), lambda i: (i % ntab, 0))
    vec = pl.BlockSpec((1, HEAD), lambda i: (0, 0))
    kern = functools.partial(_dsa_post_kernel, nq=nq, nkv=nkv, nidx=nidx,
                             wi_scale=float((nidx * HEAD) ** -0.5))
    kv_spec = pl.BlockSpec((tm, nkv, HEAD), lambda i: (i, 0, 0))
    kv_shape = jax.ShapeDtypeStruct((m, nkv, HEAD), F32)
    vt_spec = pl.BlockSpec((1, nkv * HEAD, tm), lambda i: (i, 0, 0))
    vt_shape = jax.ShapeDtypeStruct((m // tm, nkv * HEAD, tm), BF16)
    flat = lambda wdt, dt: (rowspec(wdt), jax.ShapeDtypeStruct((m, wdt), dt))
    outs = [flat(nq * HEAD, BF16), flat(nidx * HEAD, BF16), (kv_spec, kv_shape), (kv_spec, kv_shape),
            flat(HEAD, F32), flat(nidx, F32), flat(nkv * HEAD, BF16), (vt_spec, vt_shape), flat(HEAD, BF16)]
    return pl.pallas_call(
        kern,
        grid=(m // tm,),
        in_specs=[rowspec(wm), rowspec(wt), tabspec, tabspec, tabspec, vec, vec, vec, vec],
        out_specs=[o[0] for o in outs],
        out_shape=[o[1] for o in outs],
        compiler_params=_cparams(("parallel",)),
        name="dsa_post",
    )(proj_main, proj_tail, *tables, qn.reshape(1, HEAD), kn.reshape(1, HEAD),
      kg.reshape(1, HEAD), kb.reshape(1, HEAD))


def _sort_key(x):
    bits = pltpu.bitcast(x, jnp.int32)
    return jnp.where(bits < 0, bits ^ jnp.int32(0x7FFFFFFF), bits)


def _kth_largest_key(count_ge, rows, k):
    def body(it, thr):
        bit = lax.shift_left(jnp.int32(1), jnp.int32(31) - it)
        trial = thr + bit
        return jnp.where(count_ge(trial) >= k, trial, thr)

    shape = rows if isinstance(rows, tuple) else (rows, 1)
    init = jnp.full(shape, jnp.iinfo(jnp.int32).min, jnp.int32)
    return lax.fori_loop(0, 32, body, init)


def _dsa_prompt_kernel(qi_ref, wit_ref, kib_ref, q_ref, kb_ref, vt_ref, o_ref, key_ref, bias_ref,
                       *, tq, nidx, nq, nkv, topk, scale):
    i = pl.program_id(1)
    ntile = i + 1
    keyi = lax.broadcasted_iota(jnp.int32, (tq, tq), 0)
    qryi = lax.broadcasted_iota(jnp.int32, (tq, tq), 1)
    tri = keyi <= qryi

    def idx_tile(j, carry):
        kt = kib_ref[pl.ds(pl.multiple_of(j * tq, tq), tq), :]
        acc = jnp.zeros((tq, tq), F32)
        for h in range(nidx):
            d = _dot_nt(kt, qi_ref[:, h * HEAD:(h + 1) * HEAD])
            acc = acc + jnp.maximum(d, 0.0) * wit_ref[h:h + 1, :]
        acc = jnp.where(tri | (j < i), acc, NEG_BIG)
        key_ref[j] = _sort_key(acc)
        return carry

    lax.fori_loop(0, ntile, idx_tile, 0)

    def count_ge(trial):
        def tile(j, cnt):
            return cnt + jnp.sum(jnp.where(key_ref[j] >= trial, 1.0, 0.0), axis=0, keepdims=True)

        return lax.fori_loop(0, ntile, tile, jnp.zeros((1, tq), F32))

    thr = _kth_largest_key(count_ge, (1, tq), float(topk))

    def count_gt(j, cnt):
        return cnt + jnp.sum(jnp.where(key_ref[j] > thr, 1.0, 0.0), axis=0, keepdims=True)

    room = float(topk) - lax.fori_loop(0, ntile, count_gt, jnp.zeros((1, tq), F32))
    lower = jnp.where(qryi <= keyi, 1.0, 0.0).astype(BF16)

    def bias_tile(j, seen):
        k = key_ref[j]
        tie = jnp.where(k == thr, 1.0, 0.0)
        rank = seen + _dot(lower, tie.astype(BF16))
        keep = ((k > thr) | ((k == thr) & (rank <= room))) & (tri | (j < i))
        bias_ref[j] = jnp.where(keep, 0.0, NEG_BIG)
        return seen + jnp.sum(tie, axis=0, keepdims=True)

    lax.fori_loop(0, ntile, bias_tile, jnp.zeros((1, tq), F32))

    c2 = scale * math.log2(math.e)
    grp = nq // nkv
    ngb = DSA_KV_GROUPS_PER_BODY
    for n0 in range(0, nkv, ngb):
        heads = [(n, n * grp + g) for n in range(n0, n0 + ngb) for g in range(grp)]
        qs = [q_ref[:, h * HEAD:(h + 1) * HEAD] for _, h in heads]

        def att_tile(j, carry, n0=n0, heads=heads, qs=qs):
            r0 = pl.multiple_of(j * tq, tq)
            kts = {n: kb_ref[pl.ds(r0, tq), n * HEAD:(n + 1) * HEAD] for n in range(n0, n0 + ngb)}
            vts = {n: vt_ref[j, n * HEAD:(n + 1) * HEAD, :] for n in range(n0, n0 + ngb)}
            bias = bias_ref[j]
            ss = [_dot_nt(kts[n], qs[x]) + bias for x, (n, _) in enumerate(heads)]
            mid = []
            for x in range(len(heads)):
                m_i, l_i, _ = carry[x]
                m_new = jnp.maximum(m_i, jnp.max(ss[x], axis=0, keepdims=True))
                p = jnp.exp2((ss[x] - m_new) * c2)
                alpha = jnp.exp2((m_i - m_new) * c2)
                mid.append((m_new, alpha * l_i + jnp.sum(p, axis=0, keepdims=True), alpha, p.astype(BF16)))
            pvs = [_dot(vts[n], mid[x][3]) for x, (n, _) in enumerate(heads)]
            return tuple((mid[x][0], mid[x][1], mid[x][2] * carry[x][2] + pvs[x]) for x in range(len(heads)))

        one = (jnp.full((1, tq), NEG_BIG, F32), jnp.zeros((1, tq), F32), jnp.zeros((HEAD, tq), F32))
        fin = lax.fori_loop(0, ntile, att_tile, (one,) * len(heads))
        for x, (_, h) in enumerate(heads):
            o_ref[:, h * HEAD:(h + 1) * HEAD] = (fin[x][2] / fin[x][1]).T.astype(o_ref.dtype)


def dsa_prompt_attend(q_bf, qi_bf, wi_t, ki_bf, k_bf, vt_bf, *, bsz, t, nq, nkv, nidx):
    tq = DSA_TQ
    nqb = t // tq
    topk = min(TOPK_MAX, t // 4)
    assert topk <= tq and t % tq == 0
    qrow = lambda wdt: pl.BlockSpec((tq, wdt), lambda b, i: (b * nqb + i, 0))
    brow = lambda wdt: pl.BlockSpec((t, wdt), lambda b, i: (b, 0))
    kern = functools.partial(_dsa_prompt_kernel, tq=tq, nidx=nidx, nq=nq, nkv=nkv, topk=topk,
                             scale=float(HEAD ** -0.5))
    return pl.pallas_call(
        kern,
        grid=(bsz, nqb),
        in_specs=[qrow(nidx * HEAD), pl.BlockSpec((nidx, tq), lambda b, i: (0, b * nqb + i)), brow(HEAD),
                  qrow(nq * HEAD), brow(nkv * HEAD),
                  pl.BlockSpec((nqb, nkv * HEAD, tq), lambda b, i: (b, 0, 0))],
        out_specs=qrow(nq * HEAD),
        out_shape=jax.ShapeDtypeStruct((bsz * t, nq * HEAD), BF16),
        scratch_shapes=[pltpu.VMEM((nqb, tq, tq), jnp.int32), pltpu.VMEM((nqb, tq, tq), F32)],
        compiler_params=_cparams(("parallel", "arbitrary")),
        name="dsa_prompt",
    )(qi_bf, wi_t, ki_bf, q_bf, k_bf, vt_bf)


def _dsa_s_scores_kernel(pt_ref, qi_ref, wi_ref, *rest, ts, nidx, nstep, npp):
    page_refs, new_ref, sc_ref = rest[:npp], rest[npp], rest[npp + 1]
    p = pl.program_id(1)
    is_new = p == nstep
    psz = new_ref.shape[1]
    first = jnp.where(is_new, new_ref[0], page_refs[0][...])
    kcat = jnp.concatenate([first] + [page_refs[u][...] for u in range(1, npp)], axis=0).astype(BF16)
    d = _dot_nt(qi_ref[...], kcat)
    r = jnp.maximum(d, 0.0) * wi_ref[...]
    rows = [jnp.sum(r[tt * nidx:(tt + 1) * nidx, :], axis=0, keepdims=True) for tt in range(ts)]
    sc = jnp.concatenate(rows + [jnp.full((V7X_SUBLANES - ts, npp * psz), NEG_BIG, F32)], axis=0)
    rowi = lax.broadcasted_iota(jnp.int32, sc.shape, 0)
    coli = lax.broadcasted_iota(jnp.int32, sc.shape, 1)
    sc_ref[0] = jnp.where(is_new & (coli > rowi), NEG_BIG, sc)


def _dsa_s_select_kernel(sc_ref, keep_ref, *, topk, past):
    key = _sort_key(sc_ref[...])
    rows, ncol = key.shape

    def count_ge(trial):
        return jnp.sum(jnp.where(key >= trial, 1.0, 0.0), axis=1, keepdims=True)

    thr = _kth_largest_key(count_ge, rows, float(topk))
    above = key > thr
    room = float(topk) - jnp.sum(jnp.where(above, 1.0, 0.0), axis=1, keepdims=True)
    rowi = lax.broadcasted_iota(jnp.int32, key.shape, 0)
    coli = lax.broadcasted_iota(jnp.int32, key.shape, 1)
    visible = coli <= past + (rowi % V7X_SUBLANES)
    li = lax.broadcasted_iota(jnp.int32, (V7X_LANES, V7X_LANES), 0)
    lj = lax.broadcasted_iota(jnp.int32, (V7X_LANES, V7X_LANES), 1)
    upper = jnp.where(li <= lj, 1.0, 0.0).astype(BF16)
    seen = jnp.zeros((rows, 1), F32)
    for c in range(ncol // V7X_LANES):
        sl = slice(c * V7X_LANES, (c + 1) * V7X_LANES)
        tie = jnp.where(key[:, sl] == thr, 1.0, 0.0)
        rank = seen + _dot(tie.astype(BF16), upper)
        keep = (above[:, sl] | ((key[:, sl] == thr) & (rank <= room))) & visible[:, sl]
        keep_ref[:, sl] = jnp.where(keep, 1.0, 0.0)
        seen = seen + jnp.sum(tie, axis=1, keepdims=True)


def _dsa_s_attend_kernel(pt_ref, q_ref, keep_ref, expand_ref, struct_ref, *rest, nstep, npp, scale):
    kp_refs, vp_refs = rest[:npp], rest[npp:2 * npp]
    kn_ref, vn_ref, o_ref, m_ref, l_ref, acc_ref = rest[2 * npp:]
    p = pl.program_id(1)
    is_new = p == nstep
    psz = kn_ref.shape[1]

    @pl.when(p == 0)
    def _():
        m_ref[...] = jnp.full_like(m_ref, NEG_BIG)
        l_ref[...] = jnp.zeros_like(l_ref)
        acc_ref[...] = jnp.zeros_like(acc_ref)

    def flat(refs, new_ref, u):
        page = refs[u][...]
        if u == 0:
            page = jnp.where(is_new, new_ref[0], page)
        return page.reshape(-1, HEAD).astype(BF16)

    c2 = scale * math.log2(math.e)
    q = q_ref[0]
    struct = struct_ref[...]
    ss = []
    for u in range(npp):
        seen = _dot(keep_ref[0, :, u * psz:(u + 1) * psz], expand_ref[...])
        ss.append(_dot_nt(q, flat(kp_refs, kn_ref, u)) + jnp.where(seen * struct > 0.5, 0.0, NEG_BIG))
    m_i = m_ref[...]
    m_new = m_i
    for u in range(npp):
        m_new = jnp.maximum(m_new, jnp.max(ss[u], axis=1, keepdims=True))
    alpha = jnp.exp2((m_i - m_new) * c2)
    l_new = alpha * l_ref[...]
    acc = alpha * acc_ref[...]
    for u in range(npp):
        pr = jnp.exp2((ss[u] - m_new) * c2)
        l_new = l_new + jnp.sum(pr, axis=1, keepdims=True)
        acc = acc + _dot(pr.astype(BF16), flat(vp_refs, vn_ref, u))
    m_ref[...], l_ref[...], acc_ref[...] = m_new, l_new, acc

    @pl.when(is_new)
    def _():
        o_ref[0] = acc / l_new


def dsa_sample_attend(q_bf, qi_bf, wi, ki_new, k_new, v_new, ck, cv, cki, page_table,
                      *, layer_j, bsz, ts, nq, nkv, nidx):
    psz = ck.shape[2]
    npages = page_table.shape[1]
    npp = DSA_PAGES_PER_STEP
    assert npages % npp == 0
    nstep = npages // npp
    past = npages * psz
    topk = min(TOPK_MAX, (past + ts) // 4)
    grp = nq // nkv
    ncol = (nstep + 1) * npp * psz
    sub = V7X_SUBLANES
    padrows = lambda x, tail: jnp.pad(x.reshape((bsz, ts) + tail), ((0, 0), (0, psz - ts)) + ((0, 0),) * len(tail))

    def page_idx(u, ndim):
        def index(b, p, pt):
            return (layer_j, pt[b, jnp.minimum(p * npp + u, npages - 1)]) + (0,) * ndim
        return index

    scores = pl.pallas_call(
        functools.partial(_dsa_s_scores_kernel, ts=ts, nidx=nidx, nstep=nstep, npp=npp),
        grid_spec=pltpu.PrefetchScalarGridSpec(
            num_scalar_prefetch=1,
            grid=(bsz, nstep + 1),
            in_specs=[pl.BlockSpec((ts * nidx, HEAD), lambda b, p, pt: (b, 0)),
                      pl.BlockSpec((ts * nidx, 1), lambda b, p, pt: (b, 0))]
                     + [pl.BlockSpec((None, None, psz, HEAD), page_idx(u, 2)) for u in range(npp)]
                     + [pl.BlockSpec((1, psz, HEAD), lambda b, p, pt: (b, 0, 0))],
            out_specs=pl.BlockSpec((1, sub, npp * psz), lambda b, p, pt: (b, 0, p))),
        out_shape=jax.ShapeDtypeStruct((bsz, sub, ncol), F32),
        compiler_params=_cparams(("parallel", "arbitrary")),
        name="dsa_s_scores",
    )(page_table, qi_bf.reshape(bsz * ts * nidx, HEAD), wi.reshape(bsz * ts * nidx, 1),
      *([cki] * npp), padrows(ki_new, (HEAD,)))

    keep = pl.pallas_call(
        functools.partial(_dsa_s_select_kernel, topk=topk, past=past),
        out_shape=jax.ShapeDtypeStruct((bsz * sub, ncol), F32),
        compiler_params=pltpu.CompilerParams(vmem_limit_bytes=VMEM_LIMIT),
        name="dsa_s_select",
    )(scores.reshape(bsz * sub, ncol))
    nrow = nkv * ts * grp
    keep = jnp.broadcast_to(keep.reshape(bsz, 1, sub, 1, ncol)[:, :, :ts], (bsz, nkv, ts, grp, ncol))
    keep = keep.reshape(bsz, nrow, ncol).astype(BF16)
    q_all = q_bf.reshape(bsz, ts, nkv, grp, HEAD).transpose(0, 2, 1, 3, 4).reshape(bsz, nrow, HEAD)
    flat_col = np.arange(psz * nkv)
    expand = jnp.asarray(np.arange(psz)[:, None] == flat_col[None, :] // nkv, BF16)
    struct = jnp.asarray(np.arange(nrow)[:, None] // (ts * grp) == flat_col[None, :] % nkv, F32)
    kv_new_spec = pl.BlockSpec((1, psz, nkv, HEAD), lambda b, p, pt: (b, 0, 0, 0))
    const = lambda shape: pl.BlockSpec(shape, lambda b, p, pt: (0, 0))
    o = pl.pallas_call(
        functools.partial(_dsa_s_attend_kernel, nstep=nstep, npp=npp, scale=float(HEAD ** -0.5)),
        grid_spec=pltpu.PrefetchScalarGridSpec(
            num_scalar_prefetch=1,
            grid=(bsz, nstep + 1),
            in_specs=[pl.BlockSpec((1, nrow, HEAD), lambda b, p, pt: (b, 0, 0)),
                      pl.BlockSpec((1, nrow, npp * psz), lambda b, p, pt: (b, 0, p)),
                      const(expand.shape), const(struct.shape)]
                     + [pl.BlockSpec((None, None, psz, nkv, HEAD), page_idx(u, 3)) for u in range(npp)] * 2
                     + [kv_new_spec, kv_new_spec],
            out_specs=pl.BlockSpec((1, nrow, HEAD), lambda b, p, pt: (b, 0, 0)),
            scratch_shapes=[pltpu.VMEM((nrow, 1), F32), pltpu.VMEM((nrow, 1), F32),
                            pltpu.VMEM((nrow, HEAD), F32)]),
        out_shape=jax.ShapeDtypeStruct((bsz, nrow, HEAD), F32),
        compiler_params=_cparams(("parallel", "arbitrary")),
        name="dsa_s_attend",
    )(page_table, q_all, keep, expand, struct, *([ck] * npp), *([cv] * npp),
      padrows(k_new, (nkv, HEAD)), padrows(v_new, (nkv, HEAD)))
    o = o.reshape(bsz, nkv, ts, grp, HEAD).transpose(0, 2, 1, 3, 4).reshape(bsz * ts, nq * HEAD)
    return o.astype(BF16)


def kernel(x_prompt, x_sample, state_hgrn, state_rglru_h, state_rglru_conv, cache_k, cache_v, cache_kidx,
           page_table, p_prompt, p_sample, norm_mix, norm_mlp, norm_pe, w_in_ab, w_out_ab, hgrn_lb_logits,
           hgrn_out_norm, rg_conv_w, rg_conv_b, rg_wa, rg_ba, rg_wx, rg_bx, rg_lambda, w_in_c, w_out_c,
           c_q_norm, c_k_norm, idx_k_ln_g, idx_k_ln_b, w_up, w_down, w_pe, w_pg):
    bp, tp, d = x_prompt.shape
    bs, ts, _ = x_sample.shape
    depth = norm_mix.shape[0]
    a_width = hgrn_out_norm.shape[1]
    b_width = rg_lambda.shape[1]
    nh_a = a_width // HEAD
    nkv = cache_k.shape[3]
    nq = w_out_c.shape[1] // HEAD
    nidx = (w_in_c.shape[2] - nq * HEAD - 2 * nkv * HEAD - HEAD) // (HEAD + 1)
    past = page_table.shape[1] * cache_k.shape[2]
    c_main = nq * HEAD + 2 * nkv * HEAD + nidx * HEAD
    pe_dim = p_prompt.shape[-1]

    hp = x_prompt.reshape(bp * tp, d)
    hs = x_sample.reshape(bs * ts, d)
    tab_p = _rope_tables(jnp.arange(tp, dtype=jnp.int32))
    tab_s = _rope_tables(past + jnp.arange(ts, dtype=jnp.int32))
    tab_s = tuple(jnp.tile(tb, (bs, 1)) for tb in tab_s)
    ts_pad = 16
    pad_s = lambda x: jnp.pad(x.reshape(bs, ts, -1), ((0, 0), (0, ts_pad - ts), (0, 0))).reshape(bs * ts_pad, -1)
    unpad_s = lambda x: x.reshape(bs, ts_pad, -1)[:, :ts].reshape(bs * ts, -1)

    outs = {k: [] for k in ("hg_p", "rh_p", "rc_p", "k_p", "v_p", "ki_p",
                            "hg_s", "rh_s", "rc_s", "k_s", "v_s", "ki_s")}
    wd_bf = w_down.astype(BF16)
    w_in_c_t = w_in_c.transpose(0, 2, 1)
    pp_all = p_prompt.reshape(depth, -1, pe_dim).astype(BF16)
    ps_all = p_sample.reshape(depth, -1, pe_dim).astype(BF16)
    for layer in range(depth):
        j = layer // 2
        xpn = rmsnorm_bf16(hp, norm_mix[layer])
        xsn = rmsnorm_bf16(hs, norm_mix[layer])
        if layer % 2 == 0:
            gb_blk = 4 * a_width // b_width
            x_blk = gb_blk + 1
            rg_args = (rg_conv_w[j], rg_conv_b[j], rg_wa[j], rg_ba[j], rg_wx[j], rg_bx[j], rg_lambda[j])
            n_in = w_in_ab.shape[2]

            proj, w_in = matmul(xpn, w_in_ab, layer=j, emit_bf16=True)
            oa, sa = hgrn(proj, hgrn_lb_logits, hgrn_out_norm[j], jnp.zeros((bp, nh_a, HEAD, HEAD), F32),
                          bsz=bp, t=tp, layer_j=j, r=min(128, tp))
            ob, hl, cn = rglru(proj, jnp.zeros((bp, CONV_W - 1, b_width), F32), jnp.zeros((bp, b_width), F32),
                               *rg_args, bsz=bp, t=tp, r=min(256, tp), gb_blk=gb_blk, x_blk=x_blk,
                               first_pos_zero=True)
            hp, w_out = matmul(jnp.concatenate([oa, ob], axis=1), w_out_ab, layer=j, emit_bf16=True,
                               epilogue="residual", res=hp)
            outs["hg_p"].append(sa)
            outs["rh_p"].append(hl.reshape(bp, b_width))
            outs["rc_p"].append(cn)

            proj = pad_s(matmul(xsn, w_in, n_out=n_in))
            oa, sa = hgrn(proj, hgrn_lb_logits, hgrn_out_norm[j], state_hgrn[j],
                          bsz=bs, t=ts_pad, layer_j=j, r=ts_pad, t_valid=ts)
            ob, hl, cn = rglru(proj, state_rglru_conv[j], state_rglru_h[j], *rg_args, bsz=bs, t=ts_pad,
                               r=ts_pad, gb_blk=gb_blk, x_blk=x_blk, first_pos_zero=False, t_valid=ts)
            mix = unpad_s(jnp.concatenate([oa, ob], axis=1))
            hs = matmul(mix, w_out, n_out=d, epilogue="residual", res=hs)
            outs["hg_s"].append(sa)
            outs["rh_s"].append(hl.reshape(bs, b_width))
            outs["rc_s"].append(cn)
        else:
            norms = (c_q_norm[j], c_k_norm[j], idx_k_ln_g[j], idx_k_ln_b[j])
            pm, w_main = matmul(xpn, w_in_c_t, layer=j, w_t=True, n_out=c_main, emit_bf16=True)
            ptl, w_tail = matmul(xpn, w_in_c_t, layer=j, w_t=True, n_out=2 * HEAD, col0=c_main, emit_bf16=True)
            post = dsa_post(pm, ptl, tab_p, *norms, nq=nq, nkv=nkv, nidx=nidx, t_tab=tp)
            q_bf, qi_bf, kk, vv, ki, wi, k_bf, vt_bf, ki_bf = post
            o = dsa_prompt_attend(q_bf, qi_bf, wi.T, ki_bf, k_bf, vt_bf, bsz=bp, t=tp, nq=nq, nkv=nkv, nidx=nidx)
            hp, w_out = matmul(o, w_out_c, layer=j, emit_bf16=True, epilogue="residual", res=hp)
            outs["k_p"].append(kk.reshape(bp, tp, nkv, HEAD))
            outs["v_p"].append(vv.reshape(bp, tp, nkv, HEAD))
            outs["ki_p"].append(ki.reshape(bp, tp, HEAD))

            post = dsa_post(matmul(xsn, w_main, w_t=True, n_out=c_main), matmul(xsn, w_tail, w_t=True, n_out=2 * HEAD),
                            tab_s, *norms, nq=nq, nkv=nkv, nidx=nidx, t_tab=bs * ts)
            q_bf, qi_bf, kk, vv, ki, wi, _, _, _ = post
            o = dsa_sample_attend(q_bf, qi_bf, wi, ki, kk, vv, cache_k, cache_v, cache_kidx,
                                  page_table, layer_j=j, bsz=bs, ts=ts, nq=nq, nkv=nkv, nidx=nidx)
            hs = matmul(o, w_out, n_out=d, epilogue="residual", res=hs)
            outs["k_s"].append(kk.reshape(bs, ts, nkv, HEAD))
            outs["v_s"].append(vv.reshape(bs, ts, nkv, HEAD))
            outs["ki_s"].append(ki.reshape(bs, ts, HEAD))

        d_ff = w_up.shape[2]
        mid, wu = matmul(rmsnorm_bf16(hp, norm_mlp[layer]), w_up, layer=layer, emit_bf16=True,
                         epilogue="relu2", out_dtype=BF16)
        hp = matmul(mid, wd_bf, layer=layer, epilogue="residual", res=hp)
        hp, wpg = matmul(rmsnorm_bf16(hp, norm_pe[layer]), w_pg, layer=layer, emit_bf16=True, epilogue="pe_gate",
                         res=hp, p=pp_all[layer], w_pe=w_pe, pe_layer=layer)
        mid = matmul(rmsnorm_bf16(hs, norm_mlp[layer]), wu, n_out=d_ff, epilogue="relu2", out_dtype=BF16)
        hs = matmul(mid, wd_bf, layer=layer, epilogue="residual", res=hs)
        hs = matmul(rmsnorm_bf16(hs, norm_pe[layer]), wpg, n_out=d, epilogue="pe_gate", res=hs,
                    p=ps_all[layer], w_pe=w_pe, pe_layer=layer)

    st = lambda name: jnp.stack(outs[name])
    return (hp.reshape(bp, tp, d), hs.reshape(bs, ts, d),
            st("hg_p"), st("rh_p"), st("rc_p"), st("k_p"), st("v_p"), st("ki_p"),
            st("hg_s"), st("rh_s"), st("rc_s"), st("k_s"), st("v_s"), st("ki_s"))
```

```python
import functools
import math

import jax
import jax.numpy as jnp
import numpy as np
from jax import lax
from jax.experimental import pallas as pl
from jax.experimental.pallas import tpu as pltpu

F32 = jnp.float32
BF16 = jnp.bfloat16

V7X_LANES = 128
V7X_SUBLANES = 8
V7X_VMEM_BYTES = 64 * 1024 * 1024
VMEM_LIMIT = 56 * 1024 * 1024

EPS = 1e-6
NEG_BIG = -1e30
RG_C = 8.0
ROPE_THETA = 500000.0
CONV_W = 4
TOPK_MAX = 256

HEAD = 128
ROPE_HALF = HEAD // 8
DSA_KV_GROUPS_PER_BODY = 2
DSA_PAGES_PER_STEP = 8
DSA_TQ = 256
HGRN_HEADS_PER_STEP = 8

_NT = (((1,), (1,)), ((), ()))
_TN = (((0,), (0,)), ((), ()))


def _cparams(sem):
    return pltpu.CompilerParams(dimension_semantics=sem, vmem_limit_bytes=VMEM_LIMIT)


def _dot(a, b):
    return jnp.dot(a, b, preferred_element_type=F32)


def _dot_nt(a, b):
    return lax.dot_general(a, b, _NT, preferred_element_type=F32)


def _dot_tn(a, b):
    return lax.dot_general(a, b, _TN, preferred_element_type=F32)


def _rmsnorm_kernel(x_ref, g_ref, o_ref):
    x = x_ref[...]
    y = x * lax.rsqrt(jnp.mean(x * x, axis=-1, keepdims=True) + EPS)
    o_ref[...] = (y * g_ref[...]).astype(o_ref.dtype)


def rmsnorm_bf16(x, g):
    m, d = x.shape
    tm = min(m, 512)
    return pl.pallas_call(
        _rmsnorm_kernel,
        grid=(m // tm,),
        in_specs=[pl.BlockSpec((tm, d), lambda i: (i, 0)),
                  pl.BlockSpec((1, d), lambda i: (0, 0))],
        out_specs=pl.BlockSpec((tm, d), lambda i: (i, 0)),
        out_shape=jax.ShapeDtypeStruct((m, d), BF16),
        compiler_params=_cparams(("parallel",)),
        name="rmsnorm",
    )(x, g.reshape(1, d))


def _mm_kernel(*refs, nk, epilogue, w_t, emit):
    x_ref, w_ref = refs[0], refs[1]
    pos = 2
    res_ref = p_ref = wpe_ref = None
    if epilogue in ("residual", "pe_gate"):
        res_ref = refs[pos]
        pos += 1
    if epilogue == "pe_gate":
        p_ref, wpe_ref = refs[pos], refs[pos + 1]
        pos += 2
    o_ref = refs[pos]
    pos += 1
    wb_ref = None
    if emit:
        wb_ref = refs[pos]
        pos += 1
    acc_ref = refs[pos] if nk > 1 else None

    def finish(acc):
        if epilogue == "none":
            out = acc
        elif epilogue == "relu2":
            r = jnp.maximum(acc, 0.0)
            out = r * r
        elif epilogue == "residual":
            out = res_ref[...] + acc
        else:
            pe = _dot(p_ref[...], wpe_ref[...].astype(BF16))
            out = res_ref[...] + pe * jax.nn.sigmoid(acc)
        o_ref[...] = out.astype(o_ref.dtype)

    wb = w_ref[...].astype(BF16)
    if emit:
        wb_ref[...] = wb
    prod = _dot_nt(x_ref[...], wb) if w_t else _dot(x_ref[...], wb)
    if nk == 1:
        finish(prod)
    else:
        k = pl.program_id(2)

        @pl.when(k == 0)
        def _():
            acc_ref[...] = jnp.zeros_like(acc_ref)

        acc_ref[...] += prod

        @pl.when(k == nk - 1)
        def _():
            finish(acc_ref[...])


def _mm_tiles(m, n, k, epilogue, w_f32):
    tk = min(k, 4096)
    if m <= 64:
        return m, min(n, 512 if w_f32 else 2048), tk
    if k > tk:
        return 1024, 512, tk
    if w_f32:
        return min(m, 2048), min(n, 256), tk
    if epilogue in ("none", "relu2"):
        return min(m, 2048), min(n, 512), tk
    if epilogue == "pe_gate":
        return min(m, 1024), min(n, 512), tk
    return min(m, 1024), min(n, 1024), tk


def matmul(x, w, *, layer=None, n_out=None, col0=0, w_t=False, emit_bf16=False, epilogue="none", res=None,
           p=None, w_pe=None, pe_layer=None, out_dtype=F32):
    m, k = x.shape
    n = (w.shape[-2] if w_t else w.shape[-1]) if n_out is None else n_out
    tm, tn, tk = _mm_tiles(m, n, k, epilogue, w.dtype == F32)
    while n % tn:
        tn //= 2
    assert m % tm == 0 and k % tk == 0 and col0 % tn == 0
    nk = k // tk
    nj = n // tn
    jb = col0 // tn
    lead = () if layer is None else (layer,)
    pe_lead = () if pe_layer is None else (pe_layer,)
    none = lambda t: (None,) * len(t)
    wtile = (tn, tk) if w_t else (tk, tn)
    worder = (lambda kk, jj: (jj, kk)) if w_t else (lambda kk, jj: (kk, jj))
    if nk == 1:
        grid = (m // tm, nj)
        xmap = lambda i, j: (i, 0)
        wmap = lambda i, j: lead + worder(0, jb + j)
        omap = lambda i, j: (i, j)
        pmap = lambda i, j: (i, 0)
        pemap = lambda i, j: pe_lead + (0, j)
        emap = lambda i, j: worder(0, jnp.where(i == 0, j, nj))
        sem = ("arbitrary", "arbitrary") if emit_bf16 else ("parallel", "parallel")
    else:
        assert not emit_bf16
        grid = (m // tm, nj, nk)
        xmap = lambda i, j, kk: (i, kk)
        wmap = lambda i, j, kk: lead + worder(kk, jb + j)
        omap = lambda i, j, kk: (i, j)
        pmap = lambda i, j, kk: (i, 0)
        pemap = lambda i, j, kk: pe_lead + (0, j)
        sem = ("parallel", "parallel", "arbitrary")
    xspec = (pl.BlockSpec((tm, tk), xmap, pipeline_mode=pl.Buffered(1)) if tm * tk * 2 >= 16 * 2**20
             else pl.BlockSpec((tm, tk), xmap))
    in_specs = [xspec, pl.BlockSpec(none(lead) + wtile, wmap)]
    args = [x, w]
    if epilogue in ("residual", "pe_gate"):
        in_specs.append(pl.BlockSpec((tm, tn), omap))
        args.append(res)
    if epilogue == "pe_gate":
        pe_dim = p.shape[1]
        in_specs.append(pl.BlockSpec((tm, pe_dim), pmap))
        in_specs.append(pl.BlockSpec(none(pe_lead) + (pe_dim, tn), pemap))
        args += [p, w_pe]
    out_specs = [pl.BlockSpec((tm, tn), omap)]
    out_shape = [jax.ShapeDtypeStruct((m, n), out_dtype)]
    if emit_bf16:
        out_specs.append(pl.BlockSpec(wtile, emap))
        out_shape.append(jax.ShapeDtypeStruct((n + tn, k) if w_t else (k, n + tn), BF16))
    scratch = [pltpu.VMEM((tm, tn), F32)] if nk > 1 else []
    outs = pl.pallas_call(
        functools.partial(_mm_kernel, nk=nk, epilogue=epilogue, w_t=w_t, emit=emit_bf16),
        grid=grid,
        in_specs=in_specs,
        out_specs=out_specs,
        out_shape=out_shape,
        scratch_shapes=scratch,
        compiler_params=_cparams(sem),
        name="mm_" + epilogue,
    )(*args)
    return outs if emit_bf16 else outs[0]


def _hgrn_level_matrices(r):
    idx = np.arange(r)
    t = idx[:, None]
    u = idx[None, :]
    mats = []
    lvl = 2
    while lvl <= r:
        h = lvl // 2
        pos = t % lvl
        mid = t - pos + h
        upper = pos >= h
        m_up = (u >= mid) & (u <= t)
        m_lo = (u > t) & (u <= mid - 1)
        mats.append(np.where(upper, m_up, m_lo))
        lvl *= 2
    mats.append(u <= t)
    mats.append(u > t)
    return np.concatenate(mats, axis=0).astype(np.float32)


def _hgrn_kernel(q_ref, f_ref, i_ref, ga_ref, lbl_ref, an_ref, s0_ref, mall_ref,
                 o_ref, s_ref, st_ref, *, r, nhb, layer_j, t_valid):
    c = pl.program_id(2)
    nc = pl.num_programs(2)
    nlev = int(math.log2(r))

    @pl.when(c == 0)
    def _():
        for hh in range(nhb):
            st_ref[hh] = s0_ref[0, hh].T

    lbl = lbl_ref[...]
    e = jnp.exp(lbl - jnp.max(lbl, axis=0, keepdims=True))
    soft = e / jnp.sum(e, axis=0, keepdims=True)
    lb_all = jnp.zeros((1, nhb * HEAD), F32)
    for jj in range(1, layer_j + 1):
        lb_all = lb_all + soft[jj:jj + 1, :]

    z = f_ref[...]
    logf = jnp.log1p(lb_all * jnp.exp(-z)) - _softplus(-z)
    kk_all = (1.0 - lb_all) * jax.nn.sigmoid(-z)
    qq_all = jax.nn.silu(q_ref[...])
    vv_all = i_ref[...]
    if t_valid is not None:
        live = lax.broadcasted_iota(jnp.int32, (r, 1), 0) < t_valid
        logf = jnp.where(live, logf, 0.0)
        kk_all = jnp.where(live, kk_all, 0.0)
        qq_all = jnp.where(live, qq_all, 0.0)
        vv_all = jnp.where(live, vv_all, 0.0)

    hi = logf.astype(BF16)
    mid = (logf - hi.astype(F32)).astype(BF16)
    xs_all = _dot(mall_ref[...], hi) + _dot(mall_ref[...], mid)

    ti = lax.broadcasted_iota(jnp.int32, (r, r), 0)
    si = lax.broadcasted_iota(jnp.int32, (r, r), 1)
    diag = ti == si
    rowi = lax.broadcasted_iota(jnp.int32, (r, 1), 0)
    ups, pairs = [], []
    for lv in range(nlev):
        blk = 2 << lv
        h = blk // 2
        ups.append((rowi & (blk - 1)) >= h)
        pairs.append(((ti >> (lv + 1)) == (si >> (lv + 1))) & ((ti & (blk - 1)) >= h) & ((si & (blk - 1)) < h))

    ga = ga_ref[...]
    an = an_ref[...]
    sls = [slice(hh * HEAD, (hh + 1) * HEAD) for hh in range(nhb)]
    xc0, xe0 = nlev * r, (nlev + 1) * r
    sts = [st_ref[hh] for hh in range(nhb)]
    vbs = [vv_all[:, sl].astype(BF16) for sl in sls]
    o_st = [_dot_nt((qq_all[:, sl] * jnp.exp(xs_all[xc0:xc0 + r, sl])).astype(BF16), sts[hh].astype(BF16))
            for hh, sl in enumerate(sls)]
    upd = [_dot_tn(vbs[hh], (kk_all[:, sl] * jnp.exp(xs_all[xe0:xe0 + r, sl])).astype(BF16))
           for hh, sl in enumerate(sls)]
    atts = []
    for hh, sl in enumerate(sls):
        qq, kk = qq_all[:, sl], kk_all[:, sl]
        att = jnp.where(diag, _dot_nt(qq.astype(BF16), kk.astype(BF16)), 0.0)
        for lv in range(nlev):
            w = jnp.exp(xs_all[lv * r:(lv + 1) * r, sl])
            qt = jnp.where(ups[lv], qq * w, 0.0).astype(BF16)
            kt = jnp.where(ups[lv], 0.0, kk * w).astype(BF16)
            att = att + jnp.where(pairs[lv], _dot_nt(qt, kt), 0.0)
        atts.append(att.astype(BF16))
    for hh, sl in enumerate(sls):
        o = _dot(atts[hh], vbs[hh]) + o_st[hh]
        g_end = jnp.exp(xs_all[xc0 + r - 1:xc0 + r, sl])
        st_ref[hh] = g_end * sts[hh] + upd[hh]
        on = o * lax.rsqrt(jnp.mean(o * o, axis=-1, keepdims=True) + EPS) * an[:, sl]
        o_ref[:, sl] = (on * jax.nn.silu(ga[:, sl])).astype(o_ref.dtype)

    @pl.when(c == nc - 1)
    def _():
        for hh in range(nhb):
            s_ref[0, hh] = st_ref[hh].T


def hgrn(proj, lb_logits, a_norm, s0, *, bsz, t, layer_j, r, t_valid=None):
    width = a_norm.shape[0]
    nh = width // HEAD
    nhb = HGRN_HEADS_PER_STEP
    ng = nh // nhb
    n_ab = lb_logits.shape[0]
    nchunk = t // r
    mall = jnp.asarray(_hgrn_level_matrices(r), BF16)
    bw = nhb * HEAD
    col = lambda off: (lambda b, h, c: (b * nchunk + c, off * ng + h))
    kern = functools.partial(_hgrn_kernel, r=r, nhb=nhb, layer_j=layer_j, t_valid=t_valid)
    return pl.pallas_call(
        kern,
        grid=(bsz, ng, nchunk),
        in_specs=[pl.BlockSpec((r, bw), col(0)),
                  pl.BlockSpec((r, bw), col(1)),
                  pl.BlockSpec((r, bw), col(2)),
                  pl.BlockSpec((r, bw), col(3)),
                  pl.BlockSpec((n_ab, bw), lambda b, h, c: (0, h)),
                  pl.BlockSpec((1, bw), lambda b, h, c: (0, h)),
                  pl.BlockSpec((1, nhb, HEAD, HEAD), lambda b, h, c: (b, h, 0, 0)),
                  pl.BlockSpec(mall.shape, lambda b, h, c: (0, 0))],
        out_specs=[pl.BlockSpec((r, bw), lambda b, h, c: (b * nchunk + c, h)),
                   pl.BlockSpec((1, nhb, HEAD, HEAD), lambda b, h, c: (b, h, 0, 0))],
        out_shape=[jax.ShapeDtypeStruct((bsz * t, width), BF16),
                   jax.ShapeDtypeStruct((bsz, nh, HEAD, HEAD), F32)],
        scratch_shapes=[pltpu.VMEM((nhb, HEAD, HEAD), F32)],
        compiler_params=_cparams(("parallel", "parallel", "arbitrary")),
        name="hgrn",
    )(proj, proj, proj, proj, lb_logits, a_norm.reshape(1, width), s0, mall)


def _softplus(x):
    return jnp.maximum(x, 0.0) + jnp.log1p(jnp.exp(-jnp.abs(x)))


def _rglru_kernel(x_ref, gb_ref, c0_ref, h0_ref, cw_ref, cb_ref, wa_ref, ba_ref, wx_ref, bx_ref,
                  lam_ref, o_ref, hl_ref, cn_ref, xp_ref, a_ref, b_ref, hc_ref,
                  *, r, nblk, first_pos_zero, t_valid):
    i = pl.program_id(1)
    ni = pl.num_programs(1)
    pad = V7X_SUBLANES

    @pl.when(i == 0)
    def _():
        xp_ref[0:pad, :] = c0_ref[0]
        hc_ref[...] = h0_ref[0]

    x = x_ref[...]
    xp_ref[pad:pad + r, :] = x
    cw = cw_ref[...]
    y = cw[0:1, :] * xp_ref[pad - 3:pad - 3 + r, :]
    y = y + cw[1:2, :] * xp_ref[pad - 2:pad - 2 + r, :]
    y = y + cw[2:3, :] * xp_ref[pad - 1:pad - 1 + r, :]
    y = cb_ref[...] + (y + cw[3:4, :] * x)

    last = r if t_valid is None else t_valid
    cn_ref[0] = xp_ref[pad + last - 3:pad + last, :]
    xp_ref[0:pad, :] = xp_ref[r:r + pad, :]

    ra = []
    rx = []
    for n in range(nblk):
        ys = y[:, n * HEAD:(n + 1) * HEAD].astype(BF16)
        ra.append(_dot(ys, wa_ref[n]))
        rx.append(_dot(ys, wx_ref[n]))
    rg = jax.nn.sigmoid(jnp.concatenate(ra, axis=1) + ba_ref[...])
    gi = jax.nn.sigmoid(jnp.concatenate(rx, axis=1) + bx_ref[...])
    log_a = (-RG_C * rg) * _softplus(-lam_ref[...])
    a = jnp.exp(log_a)
    th = jnp.tanh(log_a)
    mult = jnp.sqrt((-2.0 * th) / (1.0 - th))
    rowi = lax.broadcasted_iota(jnp.int32, (r, 1), 0)
    if first_pos_zero:
        mult = jnp.where((rowi == 0) & (i == 0), 1.0, mult)
    bterm = mult * gi * y
    if t_valid is not None:
        live = rowi < t_valid
        a = jnp.where(live, a, 1.0)
        bterm = jnp.where(live, bterm, 0.0)
    a_ref[...] = a
    b_ref[...] = bterm

    def step(tt, h):
        h = a_ref[pl.ds(tt, 1), :] * h + b_ref[pl.ds(tt, 1), :]
        b_ref[pl.ds(tt, 1), :] = h
        return h

    h_last = lax.fori_loop(0, r, step, hc_ref[...], unroll=8)
    hc_ref[...] = h_last
    o_ref[...] = (jax.nn.gelu(gb_ref[...]) * b_ref[...]).astype(o_ref.dtype)

    @pl.when(i == ni - 1)
    def _():
        hl_ref[0] = h_last


def rglru(proj, conv0, h0, conv_w, conv_b, wa, ba, wx, bx, lam, *, bsz, t, r, gb_blk, x_blk,
          first_pos_zero, t_valid=None):
    w = lam.shape[0]
    nblk = wa.shape[0]
    nrb = t // r
    pad = V7X_SUBLANES
    c0 = jnp.pad(conv0, ((0, 0), (pad - (CONV_W - 1), 0), (0, 0)))
    row = lambda v: v.reshape(1, w)
    kern = functools.partial(_rglru_kernel, r=r, nblk=nblk, first_pos_zero=first_pos_zero, t_valid=t_valid)
    full = lambda shape: pl.BlockSpec(shape, lambda b, i: (0,) * len(shape))
    return pl.pallas_call(
        kern,
        grid=(bsz, nrb),
        in_specs=[pl.BlockSpec((r, w), lambda b, i: (b * nrb + i, x_blk)),
                  pl.BlockSpec((r, w), lambda b, i: (b * nrb + i, gb_blk)),
                  pl.BlockSpec((1, pad, w), lambda b, i: (b, 0, 0)),
                  pl.BlockSpec((1, 1, w), lambda b, i: (b, 0, 0)),
                  full((CONV_W, w)), full((1, w)),
                  full(wa.shape), full((1, w)), full(wx.shape), full((1, w)), full((1, w))],
        out_specs=[pl.BlockSpec((r, w), lambda b, i: (b * nrb + i, 0)),
                   pl.BlockSpec((1, 1, w), lambda b, i: (b, 0, 0)),
                   pl.BlockSpec((1, CONV_W - 1, w), lambda b, i: (b, 0, 0))],
        out_shape=[jax.ShapeDtypeStruct((bsz * t, w), BF16),
                   jax.ShapeDtypeStruct((bsz, 1, w), F32),
                   jax.ShapeDtypeStruct((bsz, CONV_W - 1, w), F32)],
        scratch_shapes=[pltpu.VMEM((r + pad, w), F32), pltpu.VMEM((r, w), F32),
                        pltpu.VMEM((r, w), F32), pltpu.VMEM((1, w), F32)],
        compiler_params=_cparams(("parallel", "arbitrary")),
        name="rglru",
    )(proj, proj, c0, h0.reshape(bsz, 1, w), conv_w, row(conv_b), wa.astype(BF16), row(ba),
      wx.astype(BF16), row(bx), row(lam))


def _rope_tables(pos):
    rd = HEAD // 4
    half = rd // 2
    inv = jnp.exp(-math.log(ROPE_THETA) * jnp.arange(half, dtype=F32) * (2.0 / rd))
    ang = pos.astype(F32)[:, None] * inv[None, :]
    cos, sin = jnp.cos(ang), jnp.sin(ang)
    n = pos.shape[0]
    ones = jnp.ones((n, HEAD - rd), F32)
    zeros = jnp.zeros((n, HEAD - rd), F32)
    zh = jnp.zeros((n, half), F32)
    c = jnp.concatenate([cos, cos, ones], axis=1)
    a = jnp.concatenate([-sin, zh, zeros], axis=1)
    b = jnp.concatenate([zh, sin, zeros], axis=1)
    return c, a, b


def _rope(x, c, a, b):
    return x * c + pltpu.roll(x, HEAD - ROPE_HALF, 1) * a + pltpu.roll(x, ROPE_HALF, 1) * b


def _dsa_post_kernel(pm_ref, pt_ref, c_ref, a_ref, b_ref, qn_ref, kn_ref, kg_ref, kb_ref,
                     q_ref, qi_ref, k_ref, v_ref, ki_ref, wi_ref, kbf_ref, vbf_ref, kibf_ref,
                     *, nq, nkv, nidx, wi_scale):
    c, a, b = c_ref[...], a_ref[...], b_ref[...]
    qn, kn = qn_ref[...], kn_ref[...]
    off = 0
    for h in range(nq):
        x = pm_ref[:, off + h * HEAD:off + (h + 1) * HEAD]
        y = x * lax.rsqrt(jnp.mean(x * x, axis=-1, keepdims=True) + EPS) * qn
        q_ref[:, h * HEAD:(h + 1) * HEAD] = _rope(y, c, a, b).astype(q_ref.dtype)
    off += nq * HEAD
    for h in range(nkv):
        x = pm_ref[:, off + h * HEAD:off + (h + 1) * HEAD]
        y = x * lax.rsqrt(jnp.mean(x * x, axis=-1, keepdims=True) + EPS) * kn
        kr = _rope(y, c, a, b)
        k_ref[:, h, :] = kr
        kbf_ref[:, h * HEAD:(h + 1) * HEAD] = kr.astype(BF16)
    off += nkv * HEAD
    vv = pm_ref[:, off:off + nkv * HEAD]
    for h in range(nkv):
        v_ref[:, h, :] = vv[:, h * HEAD:(h + 1) * HEAD]
    vbf_ref[0] = vv.T.astype(BF16)
    off += nkv * HEAD
    for h in range(nidx):
        x = pm_ref[:, off + h * HEAD:off + (h + 1) * HEAD]
        qi_ref[:, h * HEAD:(h + 1) * HEAD] = _rope(x, c, a, b).astype(qi_ref.dtype)
    x = pt_ref[:, 0:HEAD]
    mu = jnp.mean(x, axis=-1, keepdims=True)
    xc = x - mu
    y = xc * lax.rsqrt(jnp.mean(xc * xc, axis=-1, keepdims=True) + EPS)
    kir = _rope(y * kg_ref[...] + kb_ref[...], c, a, b)
    ki_ref[...] = kir
    kibf_ref[...] = kir.astype(BF16)
    wi_ref[...] = pt_ref[:, HEAD:HEAD + nidx] * wi_scale


def dsa_post(proj_main, proj_tail, tables, qn, kn, kg, kb, *, nq, nkv, nidx, t_tab):
    m = proj_main.shape[0]
    tm = min(m, DSA_TQ, t_tab)
    ntab = t_tab // tm
    wm = proj_main.shape[1]
    wt = proj_tail.shape[1]
    rowspec = lambda wdt: pl.BlockSpec((tm, wdt), lambda i: (i, 0))
    tabspec = pl.BlockSpec((tm, HEAD), lambda i: (i % ntab, 0))
    vec = pl.BlockSpec((1, HEAD), lambda i: (0, 0))
    kern = functools.partial(_dsa_post_kernel, nq=nq, nkv=nkv, nidx=nidx,
                             wi_scale=float((nidx * HEAD) ** -0.5))
    kv_spec = pl.BlockSpec((tm, nkv, HEAD), lambda i: (i, 0, 0))
    kv_shape = jax.ShapeDtypeStruct((m, nkv, HEAD), F32)
    vt_spec = pl.BlockSpec((1, nkv * HEAD, tm), lambda i: (i, 0, 0))
    vt_shape = jax.ShapeDtypeStruct((m // tm, nkv * HEAD, tm), BF16)
    flat = lambda wdt, dt: (rowspec(wdt), jax.ShapeDtypeStruct((m, wdt), dt))
    outs = [flat(nq * HEAD, BF16), flat(nidx * HEAD, BF16), (kv_spec, kv_shape), (kv_spec, kv_shape),
            flat(HEAD, F32), flat(nidx, F32), flat(nkv * HEAD, BF16), (vt_spec, vt_shape), flat(HEAD, BF16)]
    return pl.pallas_call(
        kern,
        grid=(m // tm,),
        in_specs=[rowspec(wm), rowspec(wt), tabspec, tabspec, tabspec, vec, vec, vec, vec],
        out_specs=[o[0] for o in outs],
        out_shape=[o[1] for o in outs],
        compiler_params=_cparams(("parallel",)),
        name="dsa_post",
    )(proj_main, proj_tail, *tables, qn.reshape(1, HEAD), kn.reshape(1, HEAD),
      kg.reshape(1, HEAD), kb.reshape(1, HEAD))


def _sort_key(x):
    bits = pltpu.bitcast(x, jnp.int32)
    return jnp.where(bits < 0, bits ^ jnp.int32(0x7FFFFFFF), bits)


def _kth_largest_key(count_ge, rows, k):
    def body(it, thr):
        bit = lax.shift_left(jnp.int32(1), jnp.int32(31) - it)
        trial = thr + bit
        return jnp.where(count_ge(trial) >= k, trial, thr)

    shape = rows if isinstance(rows, tuple) else (rows, 1)
    init = jnp.full(shape, jnp.iinfo(jnp.int32).min, jnp.int32)
    return lax.fori_loop(0, 32, body, init)


def _dsa_prompt_kernel(qi_ref, wit_ref, kib_ref, q_ref, kb_ref, vt_ref, o_ref, key_ref, bias_ref,
                       *, tq, nidx, nq, nkv, topk, scale):
    i = pl.program_id(1)
    ntile = i + 1
    keyi = lax.broadcasted_iota(jnp.int32, (tq, tq), 0)
    qryi = lax.broadcasted_iota(jnp.int32, (tq, tq), 1)
    tri = keyi <= qryi

    def idx_tile(j, carry):
        kt = kib_ref[pl.ds(pl.multiple_of(j * tq, tq), tq), :]
        acc = jnp.zeros((tq, tq), F32)
        for h in range(nidx):
            d = _dot_nt(kt, qi_ref[:, h * HEAD:(h + 1) * HEAD])
            acc = acc + jnp.maximum(d, 0.0) * wit_ref[h:h + 1, :]
        acc = jnp.where(tri | (j < i), acc, NEG_BIG)
        key_ref[j] = _sort_key(acc)
        return carry

    lax.fori_loop(0, ntile, idx_tile, 0)

    def count_ge(trial):
        def tile(j, cnt):
            return cnt + jnp.sum(jnp.where(key_ref[j] >= trial, 1.0, 0.0), axis=0, keepdims=True)

        return lax.fori_loop(0, ntile, tile, jnp.zeros((1, tq), F32))

    thr = _kth_largest_key(count_ge, (1, tq), float(topk))

    def count_gt(j, cnt):
        return cnt + jnp.sum(jnp.where(key_ref[j] > thr, 1.0, 0.0), axis=0, keepdims=True)

    room = float(topk) - lax.fori_loop(0, ntile, count_gt, jnp.zeros((1, tq), F32))
    lower = jnp.where(qryi <= keyi, 1.0, 0.0).astype(BF16)

    def bias_tile(j, seen):
        k = key_ref[j]
        tie = jnp.where(k == thr, 1.0, 0.0)
        rank = seen + _dot(lower, tie.astype(BF16))
        keep = ((k > thr) | ((k == thr) & (rank <= room))) & (tri | (j < i))
        bias_ref[j] = jnp.where(keep, 0.0, NEG_BIG)
        return seen + jnp.sum(tie, axis=0, keepdims=True)

    lax.fori_loop(0, ntile, bias_tile, jnp.zeros((1, tq), F32))

    c2 = scale * math.log2(math.e)
    grp = nq // nkv
    ngb = DSA_KV_GROUPS_PER_BODY
    for n0 in range(0, nkv, ngb):
        heads = [(n, n * grp + g) for n in range(n0, n0 + ngb) for g in range(grp)]
        qs = [q_ref[:, h * HEAD:(h + 1) * HEAD] for _, h in heads]

        def att_tile(j, carry, n0=n0, heads=heads, qs=qs):
            r0 = pl.multiple_of(j * tq, tq)
            kts = {n: kb_ref[pl.ds(r0, tq), n * HEAD:(n + 1) * HEAD] for n in range(n0, n0 + ngb)}
            vts = {n: vt_ref[j, n * HEAD:(n + 1) * HEAD, :] for n in range(n0, n0 + ngb)}
            bias = bias_ref[j]
            ss = [_dot_nt(kts[n], qs[x]) + bias for x, (n, _) in enumerate(heads)]
            mid = []
            for x in range(len(heads)):
                m_i, l_i, _ = carry[x]
                m_new = jnp.maximum(m_i, jnp.max(ss[x], axis=0, keepdims=True))
                p = jnp.exp2((ss[x] - m_new) * c2)
                alpha = jnp.exp2((m_i - m_new) * c2)
                mid.append((m_new, alpha * l_i + jnp.sum(p, axis=0, keepdims=True), alpha, p.astype(BF16)))
            pvs = [_dot(vts[n], mid[x][3]) for x, (n, _) in enumerate(heads)]
            return tuple((mid[x][0], mid[x][1], mid[x][2] * carry[x][2] + pvs[x]) for x in range(len(heads)))

        one = (jnp.full((1, tq), NEG_BIG, F32), jnp.zeros((1, tq), F32), jnp.zeros((HEAD, tq), F32))
        fin = lax.fori_loop(0, ntile, att_tile, (one,) * len(heads))
        for x, (_, h) in enumerate(heads):
            o_ref[:, h * HEAD:(h + 1) * HEAD] = (fin[x][2] / fin[x][1]).T.astype(o_ref.dtype)


def dsa_prompt_attend(q_bf, qi_bf, wi_t, ki_bf, k_bf, vt_bf, *, bsz, t, nq, nkv, nidx):
    tq = DSA_TQ
    nqb = t // tq
    topk = min(TOPK_MAX, t // 4)
    assert topk <= tq and t % tq == 0
    qrow = lambda wdt: pl.BlockSpec((tq, wdt), lambda b, i: (b * nqb + i, 0))
    brow = lambda wdt: pl.BlockSpec((t, wdt), lambda b, i: (b, 0))
    kern = functools.partial(_dsa_prompt_kernel, tq=tq, nidx=nidx, nq=nq, nkv=nkv, topk=topk,
                             scale=float(HEAD ** -0.5))
    return pl.pallas_call(
        kern,
        grid=(bsz, nqb),
        in_specs=[qrow(nidx * HEAD), pl.BlockSpec((nidx, tq), lambda b, i: (0, b * nqb + i)), brow(HEAD),
                  qrow(nq * HEAD), brow(nkv * HEAD),
                  pl.BlockSpec((nqb, nkv * HEAD, tq), lambda b, i: (b, 0, 0))],
        out_specs=qrow(nq * HEAD),
        out_shape=jax.ShapeDtypeStruct((bsz * t, nq * HEAD), BF16),
        scratch_shapes=[pltpu.VMEM((nqb, tq, tq), jnp.int32), pltpu.VMEM((nqb, tq, tq), F32)],
        compiler_params=_cparams(("parallel", "arbitrary")),
        name="dsa_prompt",
    )(qi_bf, wi_t, ki_bf, q_bf, k_bf, vt_bf)


def _dsa_s_scores_kernel(pt_ref, qi_ref, wi_ref, *rest, ts, nidx, nstep, npp):
    page_refs, new_ref, sc_ref = rest[:npp], rest[npp], rest[npp + 1]
    p = pl.program_id(1)
    is_new = p == nstep
    psz = new_ref.shape[1]
    first = jnp.where(is_new, new_ref[0], page_refs[0][...])
    kcat = jnp.concatenate([first] + [page_refs[u][...] for u in range(1, npp)], axis=0).astype(BF16)
    d = _dot_nt(qi_ref[...], kcat)
    r = jnp.maximum(d, 0.0) * wi_ref[...]
    rows = [jnp.sum(r[tt * nidx:(tt + 1) * nidx, :], axis=0, keepdims=True) for tt in range(ts)]
    sc = jnp.concatenate(rows + [jnp.full((V7X_SUBLANES - ts, npp * psz), NEG_BIG, F32)], axis=0)
    rowi = lax.broadcasted_iota(jnp.int32, sc.shape, 0)
    coli = lax.broadcasted_iota(jnp.int32, sc.shape, 1)
    sc_ref[0] = jnp.where(is_new & (coli > rowi), NEG_BIG, sc)


def _dsa_s_select_kernel(sc_ref, keep_ref, *, topk, past):
    key = _sort_key(sc_ref[...])
    rows, ncol = key.shape

    def count_ge(trial):
        return jnp.sum(jnp.where(key >= trial, 1.0, 0.0), axis=1, keepdims=True)

    thr = _kth_largest_key(count_ge, rows, float(topk))
    above = key > thr
    room = float(topk) - jnp.sum(jnp.where(above, 1.0, 0.0), axis=1, keepdims=True)
    rowi = lax.broadcasted_iota(jnp.int32, key.shape, 0)
    coli = lax.broadcasted_iota(jnp.int32, key.shape, 1)
    visible = coli <= past + (rowi % V7X_SUBLANES)
    li = lax.broadcasted_iota(jnp.int32, (V7X_LANES, V7X_LANES), 0)
    lj = lax.broadcasted_iota(jnp.int32, (V7X_LANES, V7X_LANES), 1)
    upper = jnp.where(li <= lj, 1.0, 0.0).astype(BF16)
    seen = jnp.zeros((rows, 1), F32)
    for c in range(ncol // V7X_LANES):
        sl = slice(c * V7X_LANES, (c + 1) * V7X_LANES)
        tie = jnp.where(key[:, sl] == thr, 1.0, 0.0)
        rank = seen + _dot(tie.astype(BF16), upper)
        keep = (above[:, sl] | ((key[:, sl] == thr) & (rank <= room))) & visible[:, sl]
        keep_ref[:, sl] = jnp.where(keep, 1.0, 0.0)
        seen = seen + jnp.sum(tie, axis=1, keepdims=True)


def _dsa_s_attend_kernel(pt_ref, q_ref, keep_ref, expand_ref, struct_ref, *rest, nstep, npp, scale):
    kp_refs, vp_refs = rest[:npp], rest[npp:2 * npp]
    kn_ref, vn_ref, o_ref, m_ref, l_ref, acc_ref = rest[2 * npp:]
    p = pl.program_id(1)
    is_new = p == nstep
    psz = kn_ref.shape[1]

    @pl.when(p == 0)
    def _():
        m_ref[...] = jnp.full_like(m_ref, NEG_BIG)
        l_ref[...] = jnp.zeros_like(l_ref)
        acc_ref[...] = jnp.zeros_like(acc_ref)

    def flat(refs, new_ref, u):
        page = refs[u][...]
        if u == 0:
            page = jnp.where(is_new, new_ref[0], page)
        return page.reshape(-1, HEAD).astype(BF16)

    c2 = scale * math.log2(math.e)
    q = q_ref[0]
    struct = struct_ref[...]
    ss = []
    for u in range(npp):
        seen = _dot(keep_ref[0, :, u * psz:(u + 1) * psz], expand_ref[...])
        ss.append(_dot_nt(q, flat(kp_refs, kn_ref, u)) + jnp.where(seen * struct > 0.5, 0.0, NEG_BIG))
    m_i = m_ref[...]
    m_new = m_i
    for u in range(npp):
        m_new = jnp.maximum(m_new, jnp.max(ss[u], axis=1, keepdims=True))
    alpha = jnp.exp2((m_i - m_new) * c2)
    l_new = alpha * l_ref[...]
    acc = alpha * acc_ref[...]
    for u in range(npp):
        pr = jnp.exp2((ss[u] - m_new) * c2)
        l_new = l_new + jnp.sum(pr, axis=1, keepdims=True)
        acc = acc + _dot(pr.astype(BF16), flat(vp_refs, vn_ref, u))
    m_ref[...], l_ref[...], acc_ref[...] = m_new, l_new, acc

    @pl.when(is_new)
    def _():
        o_ref[0] = acc / l_new


def dsa_sample_attend(q_bf, qi_bf, wi, ki_new, k_new, v_new, ck, cv, cki, page_table,
                      *, layer_j, bsz, ts, nq, nkv, nidx):
    psz = ck.shape[2]
    npages = page_table.shape[1]
    npp = DSA_PAGES_PER_STEP
    assert npages % npp == 0
    nstep = npages // npp
    past = npages * psz
    topk = min(TOPK_MAX, (past + ts) // 4)
    grp = nq // nkv
    ncol = (nstep + 1) * npp * psz
    sub = V7X_SUBLANES
    padrows = lambda x, tail: jnp.pad(x.reshape((bsz, ts) + tail), ((0, 0), (0, psz - ts)) + ((0, 0),) * len(tail))

    def page_idx(u, ndim):
        def index(b, p, pt):
            return (layer_j, pt[b, jnp.minimum(p * npp + u, npages - 1)]) + (0,) * ndim
        return index

    scores = pl.pallas_call(
        functools.partial(_dsa_s_scores_kernel, ts=ts, nidx=nidx, nstep=nstep, npp=npp),
        grid_spec=pltpu.PrefetchScalarGridSpec(
            num_scalar_prefetch=1,
            grid=(bsz, nstep + 1),
            in_specs=[pl.BlockSpec((ts * nidx, HEAD), lambda b, p, pt: (b, 0)),
                      pl.BlockSpec((ts * nidx, 1), lambda b, p, pt: (b, 0))]
                     + [pl.BlockSpec((None, None, psz, HEAD), page_idx(u, 2)) for u in range(npp)]
                     + [pl.BlockSpec((1, psz, HEAD), lambda b, p, pt: (b, 0, 0))],
            out_specs=pl.BlockSpec((1, sub, npp * psz), lambda b, p, pt: (b, 0, p))),
        out_shape=jax.ShapeDtypeStruct((bsz, sub, ncol), F32),
        compiler_params=_cparams(("parallel", "arbitrary")),
        name="dsa_s_scores",
    )(page_table, qi_bf.reshape(bsz * ts * nidx, HEAD), wi.reshape(bsz * ts * nidx, 1),
      *([cki] * npp), padrows(ki_new, (HEAD,)))

    keep = pl.pallas_call(
        functools.partial(_dsa_s_select_kernel, topk=topk, past=past),
        out_shape=jax.ShapeDtypeStruct((bsz * sub, ncol), F32),
        compiler_params=pltpu.CompilerParams(vmem_limit_bytes=VMEM_LIMIT),
        name="dsa_s_select",
    )(scores.reshape(bsz * sub, ncol))
    nrow = nkv * ts * grp
    keep = jnp.broadcast_to(keep.reshape(bsz, 1, sub, 1, ncol)[:, :, :ts], (bsz, nkv, ts, grp, ncol))
    keep = keep.reshape(bsz, nrow, ncol).astype(BF16)
    q_all = q_bf.reshape(bsz, ts, nkv, grp, HEAD).transpose(0, 2, 1, 3, 4).reshape(bsz, nrow, HEAD)
    flat_col = np.arange(psz * nkv)
    expand = jnp.asarray(np.arange(psz)[:, None] == flat_col[None, :] // nkv, BF16)
    struct = jnp.asarray(np.arange(nrow)[:, None] // (ts * grp) == flat_col[None, :] % nkv, F32)
    kv_new_spec = pl.BlockSpec((1, psz, nkv, HEAD), lambda b, p, pt: (b, 0, 0, 0))
    const = lambda shape: pl.BlockSpec(shape, lambda b, p, pt: (0, 0))
    o = pl.pallas_call(
        functools.partial(_dsa_s_attend_kernel, nstep=nstep, npp=npp, scale=float(HEAD ** -0.5)),
        grid_spec=pltpu.PrefetchScalarGridSpec(
            num_scalar_prefetch=1,
            grid=(bsz, nstep + 1),
            in_specs=[pl.BlockSpec((1, nrow, HEAD), lambda b, p, pt: (b, 0, 0)),
                      pl.BlockSpec((1, nrow, npp * psz), lambda b, p, pt: (b, 0, p)),
                      const(expand.shape), const(struct.shape)]
                     + [pl.BlockSpec((None, None, psz, nkv, HEAD), page_idx(u, 3)) for u in range(npp)] * 2
                     + [kv_new_spec, kv_new_spec],
            out_specs=pl.BlockSpec((1, nrow, HEAD), lambda b, p, pt: (b, 0, 0)),
            scratch_shapes=[pltpu.VMEM((nrow, 1), F32), pltpu.VMEM((nrow, 1), F32),
                            pltpu.VMEM((nrow, HEAD), F32)]),
        out_shape=jax.ShapeDtypeStruct((bsz, nrow, HEAD), F32),
        compiler_params=_cparams(("parallel", "arbitrary")),
        name="dsa_s_attend",
    )(page_table, q_all, keep, expand, struct, *([ck] * npp), *([cv] * npp),
      padrows(k_new, (nkv, HEAD)), padrows(v_new, (nkv, HEAD)))
    o = o.reshape(bsz, nkv, ts, grp, HEAD).transpose(0, 2, 1, 3, 4).reshape(bsz * ts, nq * HEAD)
    return o.astype(BF16)


def kernel(x_prompt, x_sample, state_hgrn, state_rglru_h, state_rglru_conv, cache_k, cache_v, cache_kidx,
           page_table, p_prompt, p_sample, norm_mix, norm_mlp, norm_pe, w_in_ab, w_out_ab, hgrn_lb_logits,
           hgrn_out_norm, rg_conv_w, rg_conv_b, rg_wa, rg_ba, rg_wx, rg_bx, rg_lambda, w_in_c, w_out_c,
           c_q_norm, c_k_norm, idx_k_ln_g, idx_k_ln_b, w_up, w_down, w_pe, w_pg):
    bp, tp, d = x_prompt.shape
    bs, ts, _ = x_sample.shape
    depth = norm_mix.shape[0]
    a_width = hgrn_out_norm.shape[1]
    b_width = rg_lambda.shape[1]
    nh_a = a_width // HEAD
    nkv = cache_k.shape[3]
    nq = w_out_c.shape[1] // HEAD
    nidx = (w_in_c.shape[2] - nq * HEAD - 2 * nkv * HEAD - HEAD) // (HEAD + 1)
    past = page_table.shape[1] * cache_k.shape[2]
    c_main = nq * HEAD + 2 * nkv * HEAD + nidx * HEAD
    pe_dim = p_prompt.shape[-1]

    hp = x_prompt.reshape(bp * tp, d)
    hs = x_sample.reshape(bs * ts, d)
    tab_p = _rope_tables(jnp.arange(tp, dtype=jnp.int32))
    tab_s = _rope_tables(past + jnp.arange(ts, dtype=jnp.int32))
    tab_s = tuple(jnp.tile(tb, (bs, 1)) for tb in tab_s)
    ts_pad = 16
    pad_s = lambda x: jnp.pad(x.reshape(bs, ts, -1), ((0, 0), (0, ts_pad - ts), (0, 0))).reshape(bs * ts_pad, -1)
    unpad_s = lambda x: x.reshape(bs, ts_pad, -1)[:, :ts].reshape(bs * ts, -1)

    outs = {k: [] for k in ("hg_p", "rh_p", "rc_p", "k_p", "v_p", "ki_p",
                            "hg_s", "rh_s", "rc_s", "k_s", "v_s", "ki_s")}
    wd_bf = w_down.astype(BF16)
    w_in_c_t = w_in_c.transpose(0, 2, 1)
    pp_all = p_prompt.reshape(depth, -1, pe_dim).astype(BF16)
    ps_all = p_sample.reshape(depth, -1, pe_dim).astype(BF16)
    for layer in range(depth):
        j = layer // 2
        xpn = rmsnorm_bf16(hp, norm_mix[layer])
        xsn = rmsnorm_bf16(hs, norm_mix[layer])
        if layer % 2 == 0:
            gb_blk = 4 * a_width // b_width
            x_blk = gb_blk + 1
            rg_args = (rg_conv_w[j], rg_conv_b[j], rg_wa[j], rg_ba[j], rg_wx[j], rg_bx[j], rg_lambda[j])
            n_in = w_in_ab.shape[2]

            proj, w_in = matmul(xpn, w_in_ab, layer=j, emit_bf16=True)
            oa, sa = hgrn(proj, hgrn_lb_logits, hgrn_out_norm[j], jnp.zeros((bp, nh_a, HEAD, HEAD), F32),
                          bsz=bp, t=tp, layer_j=j, r=min(128, tp))
            ob, hl, cn = rglru(proj, jnp.zeros((bp, CONV_W - 1, b_width), F32), jnp.zeros((bp, b_width), F32),
                               *rg_args, bsz=bp, t=tp, r=min(256, tp), gb_blk=gb_blk, x_blk=x_blk,
                               first_pos_zero=True)
            hp, w_out = matmul(jnp.concatenate([oa, ob], axis=1), w_out_ab, layer=j, emit_bf16=True,
                               epilogue="residual", res=hp)
            outs["hg_p"].append(sa)
            outs["rh_p"].append(hl.reshape(bp, b_width))
            outs["rc_p"].append(cn)

            proj = pad_s(matmul(xsn, w_in, n_out=n_in))
            oa, sa = hgrn(proj, hgrn_lb_logits, hgrn_out_norm[j], state_hgrn[j],
                          bsz=bs, t=ts_pad, layer_j=j, r=ts_pad, t_valid=ts)
            ob, hl, cn = rglru(proj, state_rglru_conv[j], state_rglru_h[j], *rg_args, bsz=bs, t=ts_pad,
                               r=ts_pad, gb_blk=gb_blk, x_blk=x_blk, first_pos_zero=False, t_valid=ts)
            mix = unpad_s(jnp.concatenate([oa, ob], axis=1))
            hs = matmul(mix, w_out, n_out=d, epilogue="residual", res=hs)
            outs["hg_s"].append(sa)
            outs["rh_s"].append(hl.reshape(bs, b_width))
            outs["rc_s"].append(cn)
        else:
            norms = (c_q_norm[j], c_k_norm[j], idx_k_ln_g[j], idx_k_ln_b[j])
            pm, w_main = matmul(xpn, w_in_c_t, layer=j, w_t=True, n_out=c_main, emit_bf16=True)
            ptl, w_tail = matmul(xpn, w_in_c_t, layer=j, w_t=True, n_out=2 * HEAD, col0=c_main, emit_bf16=True)
            post = dsa_post(pm, ptl, tab_p, *norms, nq=nq, nkv=nkv, nidx=nidx, t_tab=tp)
            q_bf, qi_bf, kk, vv, ki, wi, k_bf, vt_bf, ki_bf = post
            o = dsa_prompt_attend(q_bf, qi_bf, wi.T, ki_bf, k_bf, vt_bf, bsz=bp, t=tp, nq=nq, nkv=nkv, nidx=nidx)
            hp, w_out = matmul(o, w_out_c, layer=j, emit_bf16=True, epilogue="residual", res=hp)
            outs["k_p"].append(kk.reshape(bp, tp, nkv, HEAD))
            outs["v_p"].append(vv.reshape(bp, tp, nkv, HEAD))
            outs["ki_p"].append(ki.reshape(bp, tp, HEAD))

            post = dsa_post(matmul(xsn, w_main, w_t=True, n_out=c_main), matmul(xsn, w_tail, w_t=True, n_out=2 * HEAD),
                            tab_s, *norms, nq=nq, nkv=nkv, nidx=nidx, t_tab=bs * ts)
            q_bf, qi_bf, kk, vv, ki, wi, _, _, _ = post
            o = dsa_sample_attend(q_bf, qi_bf, wi, ki, kk, vv, cache_k, cache_v, cache_kidx,
                                  page_table, layer_j=j, bsz=bs, ts=ts, nq=nq, nkv=nkv, nidx=nidx)
            hs = matmul(o, w_out, n_out=d, epilogue="residual", res=hs)
            outs["k_s"].append(kk.reshape(bs, ts, nkv, HEAD))
            outs["v_s"].append(vv.reshape(bs, ts, nkv, HEAD))
            outs["ki_s"].append(ki.reshape(bs, ts, HEAD))

        d_ff = w_up.shape[2]
        mid, wu = matmul(rmsnorm_bf16(hp, norm_mlp[layer]), w_up, layer=layer, emit_bf16=True,
                         epilogue="relu2", out_dtype=BF16)
        hp = matmul(mid, wd_bf, layer=layer, epilogue="residual", res=hp)
        hp, wpg = matmul(rmsnorm_bf16(hp, norm_pe[layer]), w_pg, layer=layer, emit_bf16=True, epilogue="pe_gate",
                         res=hp, p=pp_all[layer], w_pe=w_pe, pe_layer=layer)
        mid = matmul(rmsnorm_bf16(hs, norm_mlp[layer]), wu, n_out=d_ff, epilogue="relu2", out_dtype=BF16)
        hs = matmul(mid, wd_bf, layer=layer, epilogue="residual", res=hs)
        hs = matmul(rmsnorm_bf16(hs, norm_pe[layer]), wpg, n_out=d, epilogue="pe_gate", res=hs,
                    p=ps_all[layer], w_pe=w_pe, pe_layer=layer)

    st = lambda name: jnp.stack(outs[name])
    return (hp.reshape(bp, tp, d), hs.reshape(bs, ts, d),
            st("hg_p"), st("rh_p"), st("rc_p"), st("k_p"), st("v_p"), st("ki_p"),
            st("hg_s"), st("rh_s"), st("rc_s"), st("k_s"), st("v_s"), st("ki_s"))
```

```python
import functools
import math

import jax
import jax.numpy as jnp
import numpy as np
from jax import lax
from jax.experimental import pallas as pl
from jax.experimental.pallas import tpu as pltpu

F32 = jnp.float32
BF16 = jnp.bfloat16

V7X_LANES = 128
V7X_SUBLANES = 8
V7X_VMEM_BYTES = 64 * 1024 * 1024
VMEM_LIMIT = 56 * 1024 * 1024

EPS = 1e-6
NEG_BIG = -1e30
RG_C = 8.0
ROPE_THETA = 500000.0
CONV_W = 4
TOPK_MAX = 256

HEAD = 128
ROPE_HALF = HEAD // 8
DSA_KV_GROUPS_PER_BODY = 4
DSA_PAGES_PER_STEP = 8
DSA_TQ = 256
HGRN_HEADS_PER_STEP = 16

_NT = (((1,), (1,)), ((), ()))
_TN = (((0,), (0,)), ((), ()))


def _cparams(sem):
    return pltpu.CompilerParams(dimension_semantics=sem, vmem_limit_bytes=VMEM_LIMIT)


def _dot(a, b):
    return jnp.dot(a, b, preferred_element_type=F32)


def _dot_nt(a, b):
    return lax.dot_general(a, b, _NT, preferred_element_type=F32)


def _dot_tn(a, b):
    return lax.dot_general(a, b, _TN, preferred_element_type=F32)


def _rmsnorm_kernel(x_ref, g_ref, o_ref):
    x = x_ref[...]
    y = x * lax.rsqrt(jnp.mean(x * x, axis=-1, keepdims=True) + EPS)
    o_ref[...] = (y * g_ref[...]).astype(o_ref.dtype)


def rmsnorm_bf16(x, g):
    m, d = x.shape
    tm = min(m, 512)
    return pl.pallas_call(
        _rmsnorm_kernel,
        grid=(m // tm,),
        in_specs=[pl.BlockSpec((tm, d), lambda i: (i, 0)),
                  pl.BlockSpec((1, d), lambda i: (0, 0))],
        out_specs=pl.BlockSpec((tm, d), lambda i: (i, 0)),
        out_shape=jax.ShapeDtypeStruct((m, d), BF16),
        compiler_params=_cparams(("parallel",)),
        name="rmsnorm",
    )(x, g.reshape(1, d))


def _mm_kernel(*refs, nk, epilogue, w_t, emit):
    x_ref, w_ref = refs[0], refs[1]
    pos = 2
    res_ref = p_ref = wpe_ref = None
    if epilogue in ("residual", "pe_gate"):
        res_ref = refs[pos]
        pos += 1
    if epilogue == "pe_gate":
        p_ref, wpe_ref = refs[pos], refs[pos + 1]
        pos += 2
    o_ref = refs[pos]
    pos += 1
    wb_ref = None
    if emit:
        wb_ref = refs[pos]
        pos += 1
    acc_ref = refs[pos] if nk > 1 else None

    def finish(acc):
        if epilogue == "none":
            out = acc
        elif epilogue == "relu2":
            r = jnp.maximum(acc, 0.0)
            out = r * r
        elif epilogue == "residual":
            out = res_ref[...] + acc
        else:
            pe = _dot(p_ref[...], wpe_ref[...].astype(BF16))
            out = res_ref[...] + pe * jax.nn.sigmoid(acc)
        o_ref[...] = out.astype(o_ref.dtype)

    wb = w_ref[...].astype(BF16)
    if emit:
        wb_ref[...] = wb
    prod = _dot_nt(x_ref[...], wb) if w_t else _dot(x_ref[...], wb)
    if nk == 1:
        finish(prod)
    else:
        k = pl.program_id(2)

        @pl.when(k == 0)
        def _():
            acc_ref[...] = jnp.zeros_like(acc_ref)

        acc_ref[...] += prod

        @pl.when(k == nk - 1)
        def _():
            finish(acc_ref[...])


def _mm_tiles(m, n, k, epilogue, w_f32):
    tk = min(k, 4096)
    if m <= 64:
        return m, min(n, 512 if w_f32 else 2048), tk
    if k > tk:
        return 1024, 512, tk
    if w_f32:
        return min(m, 2048), min(n, 256), tk
    if epilogue in ("none", "relu2"):
        return min(m, 2048), min(n, 512), tk
    if epilogue == "pe_gate":
        return min(m, 1024), min(n, 512), tk
    return min(m, 1024), min(n, 1024), tk


def matmul(x, w, *, layer=None, n_out=None, col0=0, w_t=False, emit_bf16=False, epilogue="none", res=None,
           p=None, w_pe=None, pe_layer=None, out_dtype=F32):
    m, k = x.shape
    n = (w.shape[-2] if w_t else w.shape[-1]) if n_out is None else n_out
    tm, tn, tk = _mm_tiles(m, n, k, epilogue, w.dtype == F32)
    while n % tn:
        tn //= 2
    assert m % tm == 0 and k % tk == 0 and col0 % tn == 0
    nk = k // tk
    nj = n // tn
    jb = col0 // tn
    lead = () if layer is None else (layer,)
    pe_lead = () if pe_layer is None else (pe_layer,)
    none = lambda t: (None,) * len(t)
    wtile = (tn, tk) if w_t else (tk, tn)
    worder = (lambda kk, jj: (jj, kk)) if w_t else (lambda kk, jj: (kk, jj))
    if nk == 1:
        grid = (m // tm, nj)
        xmap = lambda i, j: (i, 0)
        wmap = lambda i, j: lead + worder(0, jb + j)
        omap = lambda i, j: (i, j)
        pmap = lambda i, j: (i, 0)
        pemap = lambda i, j: pe_lead + (0, j)
        emap = lambda i, j: worder(0, jnp.where(i == 0, j, nj))
        sem = ("arbitrary", "arbitrary") if emit_bf16 else ("parallel", "parallel")
    else:
        assert not emit_bf16
        grid = (m // tm, nj, nk)
        xmap = lambda i, j, kk: (i, kk)
        wmap = lambda i, j, kk: lead + worder(kk, jb + j)
        omap = lambda i, j, kk: (i, j)
        pmap = lambda i, j, kk: (i, 0)
        pemap = lambda i, j, kk: pe_lead + (0, j)
        sem = ("parallel", "parallel", "arbitrary")
    xspec = (pl.BlockSpec((tm, tk), xmap, pipeline_mode=pl.Buffered(1)) if tm * tk * 2 >= 16 * 2**20
             else pl.BlockSpec((tm, tk), xmap))
    in_specs = [xspec, pl.BlockSpec(none(lead) + wtile, wmap)]
    args = [x, w]
    if epilogue in ("residual", "pe_gate"):
        in_specs.append(pl.BlockSpec((tm, tn), omap))
        args.append(res)
    if epilogue == "pe_gate":
        pe_dim = p.shape[1]
        in_specs.append(pl.BlockSpec((tm, pe_dim), pmap))
        in_specs.append(pl.BlockSpec(none(pe_lead) + (pe_dim, tn), pemap))
        args += [p, w_pe]
    out_specs = [pl.BlockSpec((tm, tn), omap)]
    out_shape = [jax.ShapeDtypeStruct((m, n), out_dtype)]
    if emit_bf16:
        out_specs.append(pl.BlockSpec(wtile, emap))
        out_shape.append(jax.ShapeDtypeStruct((n + tn, k) if w_t else (k, n + tn), BF16))
    scratch = [pltpu.VMEM((tm, tn), F32)] if nk > 1 else []
    outs = pl.pallas_call(
        functools.partial(_mm_kernel, nk=nk, epilogue=epilogue, w_t=w_t, emit=emit_bf16),
        grid=grid,
        in_specs=in_specs,
        out_specs=out_specs,
        out_shape=out_shape,
        scratch_shapes=scratch,
        compiler_params=_cparams(sem),
        name="mm_" + epilogue,
    )(*args)
    return outs if emit_bf16 else outs[0]


def _hgrn_level_matrices(r):
    idx = np.arange(r)
    t = idx[:, None]
    u = idx[None, :]
    mats = []
    lvl = 2
    while lvl <= r:
        h = lvl // 2
        pos = t % lvl
        mid = t - pos + h
        upper = pos >= h
        m_up = (u >= mid) & (u <= t)
        m_lo = (u > t) & (u <= mid - 1)
        mats.append(np.where(upper, m_up, m_lo))
        lvl *= 2
    mats.append(u <= t)
    mats.append(u > t)
    return np.concatenate(mats, axis=0).astype(np.float32)


def _hgrn_kernel(q_ref, f_ref, i_ref, ga_ref, lbl_ref, an_ref, s0_ref, mall_ref,
                 o_ref, s_ref, st_ref, *, r, nhb, layer_j, t_valid):
    c = pl.program_id(2)
    nc = pl.num_programs(2)
    nlev = int(math.log2(r))

    @pl.when(c == 0)
    def _():
        for hh in range(nhb):
            st_ref[hh] = s0_ref[0, hh].T

    lbl = lbl_ref[...]
    e = jnp.exp(lbl - jnp.max(lbl, axis=0, keepdims=True))
    soft = e / jnp.sum(e, axis=0, keepdims=True)
    lb_all = jnp.zeros((1, nhb * HEAD), F32)
    for jj in range(1, layer_j + 1):
        lb_all = lb_all + soft[jj:jj + 1, :]

    z = f_ref[...]
    logf = jnp.log1p(lb_all * jnp.exp(-z)) - _softplus(-z)
    kk_all = (1.0 - lb_all) * jax.nn.sigmoid(-z)
    qq_all = jax.nn.silu(q_ref[...])
    vv_all = i_ref[...]
    if t_valid is not None:
        live = lax.broadcasted_iota(jnp.int32, (r, 1), 0) < t_valid
        logf = jnp.where(live, logf, 0.0)
        kk_all = jnp.where(live, kk_all, 0.0)
        qq_all = jnp.where(live, qq_all, 0.0)
        vv_all = jnp.where(live, vv_all, 0.0)

    hi = logf.astype(BF16)
    mid = (logf - hi.astype(F32)).astype(BF16)
    xs_all = _dot(mall_ref[...], hi) + _dot(mall_ref[...], mid)

    ti = lax.broadcasted_iota(jnp.int32, (r, r), 0)
    si = lax.broadcasted_iota(jnp.int32, (r, r), 1)
    diag = ti == si
    rowi = lax.broadcasted_iota(jnp.int32, (r, 1), 0)
    ups, pairs = [], []
    for lv in range(nlev):
        blk = 2 << lv
        h = blk // 2
        ups.append((rowi & (blk - 1)) >= h)
        pairs.append(((ti >> (lv + 1)) == (si >> (lv + 1))) & ((ti & (blk - 1)) >= h) & ((si & (blk - 1)) < h))

    ga = ga_ref[...]
    an = an_ref[...]
    sls = [slice(hh * HEAD, (hh + 1) * HEAD) for hh in range(nhb)]
    xc0, xe0 = nlev * r, (nlev + 1) * r
    sts = [st_ref[hh] for hh in range(nhb)]
    vbs = [vv_all[:, sl].astype(BF16) for sl in sls]
    o_st = [_dot_nt((qq_all[:, sl] * jnp.exp(xs_all[xc0:xc0 + r, sl])).astype(BF16), sts[hh].astype(BF16))
            for hh, sl in enumerate(sls)]
    upd = [_dot_tn(vbs[hh], (kk_all[:, sl] * jnp.exp(xs_all[xe0:xe0 + r, sl])).astype(BF16))
           for hh, sl in enumerate(sls)]
    atts = []
    for hh, sl in enumerate(sls):
        qq, kk = qq_all[:, sl], kk_all[:, sl]
        att = jnp.where(diag, _dot_nt(qq.astype(BF16), kk.astype(BF16)), 0.0)
        for lv in range(nlev):
            w = jnp.exp(xs_all[lv * r:(lv + 1) * r, sl])
            qt = jnp.where(ups[lv], qq * w, 0.0).astype(BF16)
            kt = jnp.where(ups[lv], 0.0, kk * w).astype(BF16)
            att = att + jnp.where(pairs[lv], _dot_nt(qt, kt), 0.0)
        atts.append(att.astype(BF16))
    for hh, sl in enumerate(sls):
        o = _dot(atts[hh], vbs[hh]) + o_st[hh]
        g_end = jnp.exp(xs_all[xc0 + r - 1:xc0 + r, sl])
        st_ref[hh] = g_end * sts[hh] + upd[hh]
        on = o * lax.rsqrt(jnp.mean(o * o, axis=-1, keepdims=True) + EPS) * an[:, sl]
        o_ref[:, sl] = (on * jax.nn.silu(ga[:, sl])).astype(o_ref.dtype)

    @pl.when(c == nc - 1)
    def _():
        for hh in range(nhb):
            s_ref[0, hh] = st_ref[hh].T


def hgrn(proj, lb_logits, a_norm, s0, *, bsz, t, layer_j, r, t_valid=None):
    width = a_norm.shape[0]
    nh = width // HEAD
    nhb = HGRN_HEADS_PER_STEP
    ng = nh // nhb
    n_ab = lb_logits.shape[0]
    nchunk = t // r
    mall = jnp.asarray(_hgrn_level_matrices(r), BF16)
    bw = nhb * HEAD
    col = lambda off: (lambda b, h, c: (b * nchunk + c, off * ng + h))
    kern = functools.partial(_hgrn_kernel, r=r, nhb=nhb, layer_j=layer_j, t_valid=t_valid)
    return pl.pallas_call(
        kern,
        grid=(bsz, ng, nchunk),
        in_specs=[pl.BlockSpec((r, bw), col(0)),
                  pl.BlockSpec((r, bw), col(1)),
                  pl.BlockSpec((r, bw), col(2)),
                  pl.BlockSpec((r, bw), col(3)),
                  pl.BlockSpec((n_ab, bw), lambda b, h, c: (0, h)),
                  pl.BlockSpec((1, bw), lambda b, h, c: (0, h)),
                  pl.BlockSpec((1, nhb, HEAD, HEAD), lambda b, h, c: (b, h, 0, 0)),
                  pl.BlockSpec(mall.shape, lambda b, h, c: (0, 0))],
        out_specs=[pl.BlockSpec((r, bw), lambda b, h, c: (b * nchunk + c, h)),
                   pl.BlockSpec((1, nhb, HEAD, HEAD), lambda b, h, c: (b, h, 0, 0))],
        out_shape=[jax.ShapeDtypeStruct((bsz * t, width), BF16),
                   jax.ShapeDtypeStruct((bsz, nh, HEAD, HEAD), F32)],
        scratch_shapes=[pltpu.VMEM((nhb, HEAD, HEAD), F32)],
        compiler_params=_cparams(("parallel", "parallel", "arbitrary")),
        name="hgrn",
    )(proj, proj, proj, proj, lb_logits, a_norm.reshape(1, width), s0, mall)


def _softplus(x):
    return jnp.maximum(x, 0.0) + jnp.log1p(jnp.exp(-jnp.abs(x)))


def _rglru_kernel(x_ref, gb_ref, c0_ref, h0_ref, cw_ref, cb_ref, wa_ref, ba_ref, wx_ref, bx_ref,
                  lam_ref, o_ref, hl_ref, cn_ref, xp_ref, a_ref, b_ref, hc_ref,
                  *, r, nblk, first_pos_zero, t_valid):
    i = pl.program_id(1)
    ni = pl.num_programs(1)
    pad = V7X_SUBLANES

    @pl.when(i == 0)
    def _():
        xp_ref[0:pad, :] = c0_ref[0]
        hc_ref[...] = h0_ref[0]

    x = x_ref[...]
    xp_ref[pad:pad + r, :] = x
    cw = cw_ref[...]
    y = cw[0:1, :] * xp_ref[pad - 3:pad - 3 + r, :]
    y = y + cw[1:2, :] * xp_ref[pad - 2:pad - 2 + r, :]
    y = y + cw[2:3, :] * xp_ref[pad - 1:pad - 1 + r, :]
    y = cb_ref[...] + (y + cw[3:4, :] * x)

    last = r if t_valid is None else t_valid
    cn_ref[0] = xp_ref[pad + last - 3:pad + last, :]
    xp_ref[0:pad, :] = xp_ref[r:r + pad, :]

    ra = []
    rx = []
    for n in range(nblk):
        ys = y[:, n * HEAD:(n + 1) * HEAD].astype(BF16)
        ra.append(_dot(ys, wa_ref[n]))
        rx.append(_dot(ys, wx_ref[n]))
    rg = jax.nn.sigmoid(jnp.concatenate(ra, axis=1) + ba_ref[...])
    gi = jax.nn.sigmoid(jnp.concatenate(rx, axis=1) + bx_ref[...])
    log_a = (-RG_C * rg) * _softplus(-lam_ref[...])
    a = jnp.exp(log_a)
    th = jnp.tanh(log_a)
    mult = jnp.sqrt((-2.0 * th) / (1.0 - th))
    rowi = lax.broadcasted_iota(jnp.int32, (r, 1), 0)
    if first_pos_zero:
        mult = jnp.where((rowi == 0) & (i == 0), 1.0, mult)
    bterm = mult * gi * y
    if t_valid is not None:
        live = rowi < t_valid
        a = jnp.where(live, a, 1.0)
        bterm = jnp.where(live, bterm, 0.0)
    a_ref[...] = a
    b_ref[...] = bterm

    def step(tt, h):
        h = a_ref[pl.ds(tt, 1), :] * h + b_ref[pl.ds(tt, 1), :]
        b_ref[pl.ds(tt, 1), :] = h
        return h

    h_last = lax.fori_loop(0, r, step, hc_ref[...], unroll=8)
    hc_ref[...] = h_last
    o_ref[...] = (jax.nn.gelu(gb_ref[...]) * b_ref[...]).astype(o_ref.dtype)

    @pl.when(i == ni - 1)
    def _():
        hl_ref[0] = h_last


def rglru(proj, conv0, h0, conv_w, conv_b, wa, ba, wx, bx, lam, *, bsz, t, r, gb_blk, x_blk,
          first_pos_zero, t_valid=None):
    w = lam.shape[0]
    nblk = wa.shape[0]
    nrb = t // r
    pad = V7X_SUBLANES
    c0 = jnp.pad(conv0, ((0, 0), (pad - (CONV_W - 1), 0), (0, 0)))
    row = lambda v: v.reshape(1, w)
    kern = functools.partial(_rglru_kernel, r=r, nblk=nblk, first_pos_zero=first_pos_zero, t_valid=t_valid)
    full = lambda shape: pl.BlockSpec(shape, lambda b, i: (0,) * len(shape))
    return pl.pallas_call(
        kern,
        grid=(bsz, nrb),
        in_specs=[pl.BlockSpec((r, w), lambda b, i: (b * nrb + i, x_blk)),
                  pl.BlockSpec((r, w), lambda b, i: (b * nrb + i, gb_blk)),
                  pl.BlockSpec((1, pad, w), lambda b, i: (b, 0, 0)),
                  pl.BlockSpec((1, 1, w), lambda b, i: (b, 0, 0)),
                  full((CONV_W, w)), full((1, w)),
                  full(wa.shape), full((1, w)), full(wx.shape), full((1, w)), full((1, w))],
        out_specs=[pl.BlockSpec((r, w), lambda b, i: (b * nrb + i, 0)),
                   pl.BlockSpec((1, 1, w), lambda b, i: (b, 0, 0)),
                   pl.BlockSpec((1, CONV_W - 1, w), lambda b, i: (b, 0, 0))],
        out_shape=[jax.ShapeDtypeStruct((bsz * t, w), BF16),
                   jax.ShapeDtypeStruct((bsz, 1, w), F32),
                   jax.ShapeDtypeStruct((bsz, CONV_W - 1, w), F32)],
        scratch_shapes=[pltpu.VMEM((r + pad, w), F32), pltpu.VMEM((r, w), F32),
                        pltpu.VMEM((r, w), F32), pltpu.VMEM((1, w), F32)],
        compiler_params=_cparams(("parallel", "arbitrary")),
        name="rglru",
    )(proj, proj, c0, h0.reshape(bsz, 1, w), conv_w, row(conv_b), wa.astype(BF16), row(ba),
      wx.astype(BF16), row(bx), row(lam))


def _rope_tables(pos):
    rd = HEAD // 4
    half = rd // 2
    inv = jnp.exp(-math.log(ROPE_THETA) * jnp.arange(half, dtype=F32) * (2.0 / rd))
    ang = pos.astype(F32)[:, None] * inv[None, :]
    cos, sin = jnp.cos(ang), jnp.sin(ang)
    n = pos.shape[0]
    ones = jnp.ones((n, HEAD - rd), F32)
    zeros = jnp.zeros((n, HEAD - rd), F32)
    zh = jnp.zeros((n, half), F32)
    c = jnp.concatenate([cos, cos, ones], axis=1)
    a = jnp.concatenate([-sin, zh, zeros], axis=1)
    b = jnp.concatenate([zh, sin, zeros], axis=1)
    return c, a, b


def _rope(x, c, a, b):
    return x * c + pltpu.roll(x, HEAD - ROPE_HALF, 1) * a + pltpu.roll(x, ROPE_HALF, 1) * b


def _dsa_post_kernel(pm_ref, pt_ref, c_ref, a_ref, b_ref, qn_ref, kn_ref, kg_ref, kb_ref,
                     q_ref, qi_ref, k_ref, v_ref, ki_ref, wi_ref, kbf_ref, vbf_ref, kibf_ref,
                     *, nq, nkv, nidx, wi_scale):
    c, a, b = c_ref[...], a_ref[...], b_ref[...]
    qn, kn = qn_ref[...], kn_ref[...]
    off = 0
    for h in range(nq):
        x = pm_ref[:, off + h * HEAD:off + (h + 1) * HEAD]
        y = x * lax.rsqrt(jnp.mean(x * x, axis=-1, keepdims=True) + EPS) * qn
        q_ref[:, h * HEAD:(h + 1) * HEAD] = _rope(y, c, a, b).astype(q_ref.dtype)
    off += nq * HEAD
    for h in range(nkv):
        x = pm_ref[:, off + h * HEAD:off + (h + 1) * HEAD]
        y = x * lax.rsqrt(jnp.mean(x * x, axis=-1, keepdims=True) + EPS) * kn
        kr = _rope(y, c, a, b)
        k_ref[:, h, :] = kr
        kbf_ref[:, h * HEAD:(h + 1) * HEAD] = kr.astype(BF16)
    off += nkv * HEAD
    vv = pm_ref[:, off:off + nkv * HEAD]
    for h in range(nkv):
        v_ref[:, h, :] = vv[:, h * HEAD:(h + 1) * HEAD]
    vbf_ref[0] = vv.T.astype(BF16)
    off += nkv * HEAD
    for h in range(nidx):
        x = pm_ref[:, off + h * HEAD:off + (h + 1) * HEAD]
        qi_ref[:, h * HEAD:(h + 1) * HEAD] = _rope(x, c, a, b).astype(qi_ref.dtype)
    x = pt_ref[:, 0:HEAD]
    mu = jnp.mean(x, axis=-1, keepdims=True)
    xc = x - mu
    y = xc * lax.rsqrt(jnp.mean(xc * xc, axis=-1, keepdims=True) + EPS)
    kir = _rope(y * kg_ref[...] + kb_ref[...], c, a, b)
    ki_ref[...] = kir
    kibf_ref[...] = kir.astype(BF16)
    wi_ref[...] = pt_ref[:, HEAD:HEAD + nidx] * wi_scale


def dsa_post(proj_main, proj_tail, tables, qn, kn, kg, kb, *, nq, nkv, nidx, t_tab):
    m = proj_main.shape[0]
    tm = min(m, DSA_TQ, t_tab)
    ntab = t_tab // tm
    wm = proj_main.shape[1]
    wt = proj_tail.shape[1]
    rowspec = lambda wdt: pl.BlockSpec((tm, wdt), lambda i: (i, 0))
    tabspec = pl.BlockSpec((tm, HEAD), lambda i: (i % ntab, 0))
    vec = pl.BlockSpec((1, HEAD), lambda i: (0, 0))
    kern = functools.partial(_dsa_post_kernel, nq=nq, nkv=nkv, nidx=nidx,
                             wi_scale=float((nidx * HEAD) ** -0.5))
    kv_spec = pl.BlockSpec((tm, nkv, HEAD), lambda i: (i, 0, 0))
    kv_shape = jax.ShapeDtypeStruct((m, nkv, HEAD), F32)
    vt_spec = pl.BlockSpec((1, nkv * HEAD, tm), lambda i: (i, 0, 0))
    vt_shape = jax.ShapeDtypeStruct((m // tm, nkv * HEAD, tm), BF16)
    flat = lambda wdt, dt: (rowspec(wdt), jax.ShapeDtypeStruct((m, wdt), dt))
    outs = [flat(nq * HEAD, BF16), flat(nidx * HEAD, BF16), (kv_spec, kv_shape), (kv_spec, kv_shape),
            flat(HEAD, F32), flat(nidx, F32), flat(nkv * HEAD, BF16), (vt_spec, vt_shape), flat(HEAD, BF16)]
    return pl.pallas_call(
        kern,
        grid=(m // tm,),
        in_specs=[rowspec(wm), rowspec(wt), tabspec, tabspec, tabspec, vec, vec, vec, vec],
        out_specs=[o[0] for o in outs],
        out_shape=[o[1] for o in outs],
        compiler_params=_cparams(("parallel",)),
        name="dsa_post",
    )(proj_main, proj_tail, *tables, qn.reshape(1, HEAD), kn.reshape(1, HEAD),
      kg.reshape(1, HEAD), kb.reshape(1, HEAD))


def _sort_key(x):
    bits = pltpu.bitcast(x, jnp.int32)
    return jnp.where(bits < 0, bits ^ jnp.int32(0x7FFFFFFF), bits)


def _kth_largest_key(count_ge, rows, k):
    def body(it, thr):
        bit = lax.shift_left(jnp.int32(1), jnp.int32(31) - it)
        trial = thr + bit
        return jnp.where(count_ge(trial) >= k, trial, thr)

    shape = rows if isinstance(rows, tuple) else (rows, 1)
    init = jnp.full(shape, jnp.iinfo(jnp.int32).min, jnp.int32)
    return lax.fori_loop(0, 32, body, init)


def _dsa_prompt_kernel(qi_ref, wit_ref, kib_ref, q_ref, kb_ref, vt_ref, o_ref, key_ref, bias_ref,
                       *, tq, nidx, nq, nkv, topk, scale):
    i = pl.program_id(1)
    ntile = i + 1
    keyi = lax.broadcasted_iota(jnp.int32, (tq, tq), 0)
    qryi = lax.broadcasted_iota(jnp.int32, (tq, tq), 1)
    tri = keyi <= qryi

    def idx_tile(j, carry):
        kt = kib_ref[pl.ds(pl.multiple_of(j * tq, tq), tq), :]
        acc = jnp.zeros((tq, tq), F32)
        for h in range(nidx):
            d = _dot_nt(kt, qi_ref[:, h * HEAD:(h + 1) * HEAD])
            acc = acc + jnp.maximum(d, 0.0) * wit_ref[h:h + 1, :]
        acc = jnp.where(tri | (j < i), acc, NEG_BIG)
        key_ref[j] = _sort_key(acc)
        return carry

    lax.fori_loop(0, ntile, idx_tile, 0)

    sub = V7X_SUBLANES

    def count_ge(trial):
        def tile(j, cnt):
            hit = jnp.where(key_ref[j] >= trial, 1.0, 0.0)
            return cnt + jnp.sum(hit.reshape(tq // sub, sub, tq), axis=0)

        return jnp.sum(lax.fori_loop(0, ntile, tile, jnp.zeros((sub, tq), F32)), axis=0, keepdims=True)

    thr = _kth_largest_key(count_ge, (1, tq), float(topk))

    def count_gt(j, cnt):
        return cnt + jnp.sum(jnp.where(key_ref[j] > thr, 1.0, 0.0), axis=0, keepdims=True)

    room = float(topk) - lax.fori_loop(0, ntile, count_gt, jnp.zeros((1, tq), F32))
    lower = jnp.where(qryi <= keyi, 1.0, 0.0).astype(BF16)

    def bias_tile(j, seen):
        k = key_ref[j]
        tie = jnp.where(k == thr, 1.0, 0.0)
        rank = seen + _dot(lower, tie.astype(BF16))
        keep = ((k > thr) | ((k == thr) & (rank <= room))) & (tri | (j < i))
        bias_ref[j] = jnp.where(keep, 0.0, NEG_BIG)
        return seen + jnp.sum(tie, axis=0, keepdims=True)

    lax.fori_loop(0, ntile, bias_tile, jnp.zeros((1, tq), F32))

    c2 = scale * math.log2(math.e)
    grp = nq // nkv
    ngb = DSA_KV_GROUPS_PER_BODY
    for n0 in range(0, nkv, ngb):
        heads = [(n, n * grp + g) for n in range(n0, n0 + ngb) for g in range(grp)]
        qs = [q_ref[:, h * HEAD:(h + 1) * HEAD] for _, h in heads]

        def att_tile(j, carry, n0=n0, heads=heads, qs=qs):
            r0 = pl.multiple_of(j * tq, tq)
            kts = {n: kb_ref[pl.ds(r0, tq), n * HEAD:(n + 1) * HEAD] for n in range(n0, n0 + ngb)}
            vts = {n: vt_ref[j, n * HEAD:(n + 1) * HEAD, :] for n in range(n0, n0 + ngb)}
            bias = bias_ref[j]
            ss = [_dot_nt(kts[n], qs[x]) + bias for x, (n, _) in enumerate(heads)]
            mid = []
            for x in range(len(heads)):
                m_i, l_i, _ = carry[x]
                m_new = jnp.maximum(m_i, jnp.max(ss[x], axis=0, keepdims=True))
                p = jnp.exp2((ss[x] - m_new) * c2)
                alpha = jnp.exp2((m_i - m_new) * c2)
                mid.append((m_new, alpha * l_i + jnp.sum(p, axis=0, keepdims=True), alpha, p.astype(BF16)))
            pvs = [_dot(vts[n], mid[x][3]) for x, (n, _) in enumerate(heads)]
            return tuple((mid[x][0], mid[x][1], mid[x][2] * carry[x][2] + pvs[x]) for x in range(len(heads)))

        one = (jnp.full((1, tq), NEG_BIG, F32), jnp.zeros((1, tq), F32), jnp.zeros((HEAD, tq), F32))
        fin = lax.fori_loop(0, ntile, att_tile, (one,) * len(heads))
        for x, (_, h) in enumerate(heads):
            o_ref[:, h * HEAD:(h + 1) * HEAD] = (fin[x][2] / fin[x][1]).T.astype(o_ref.dtype)


def dsa_prompt_attend(q_bf, qi_bf, wi_t, ki_bf, k_bf, vt_bf, *, bsz, t, nq, nkv, nidx):
    tq = DSA_TQ
    nqb = t // tq
    topk = min(TOPK_MAX, t // 4)
    assert topk <= tq and t % tq == 0
    qrow = lambda wdt: pl.BlockSpec((tq, wdt), lambda b, i: (b * nqb + i, 0))
    brow = lambda wdt: pl.BlockSpec((t, wdt), lambda b, i: (b, 0))
    kern = functools.partial(_dsa_prompt_kernel, tq=tq, nidx=nidx, nq=nq, nkv=nkv, topk=topk,
                             scale=float(HEAD ** -0.5))
    return pl.pallas_call(
        kern,
        grid=(bsz, nqb),
        in_specs=[qrow(nidx * HEAD), pl.BlockSpec((nidx, tq), lambda b, i: (0, b * nqb + i)), brow(HEAD),
                  qrow(nq * HEAD), brow(nkv * HEAD),
                  pl.BlockSpec((nqb, nkv * HEAD, tq), lambda b, i: (b, 0, 0))],
        out_specs=qrow(nq * HEAD),
        out_shape=jax.ShapeDtypeStruct((bsz * t, nq * HEAD), BF16),
        scratch_shapes=[pltpu.VMEM((nqb, tq, tq), jnp.int32), pltpu.VMEM((nqb, tq, tq), F32)],
        compiler_params=_cparams(("parallel", "arbitrary")),
        name="dsa_prompt",
    )(qi_bf, wi_t, ki_bf, q_bf, k_bf, vt_bf)


def _dsa_s_scores_kernel(pt_ref, qi_ref, wi_ref, *rest, ts, nidx, nstep, npp):
    page_refs, new_ref, sc_ref = rest[:npp], rest[npp], rest[npp + 1]
    p = pl.program_id(1)
    is_new = p == nstep
    psz = new_ref.shape[1]
    first = jnp.where(is_new, new_ref[0], page_refs[0][...])
    kcat = jnp.concatenate([first] + [page_refs[u][...] for u in range(1, npp)], axis=0).astype(BF16)
    d = _dot_nt(qi_ref[...], kcat)
    r = jnp.maximum(d, 0.0) * wi_ref[...]
    rows = [jnp.sum(r[tt * nidx:(tt + 1) * nidx, :], axis=0, keepdims=True) for tt in range(ts)]
    sc = jnp.concatenate(rows + [jnp.full((V7X_SUBLANES - ts, npp * psz), NEG_BIG, F32)], axis=0)
    rowi = lax.broadcasted_iota(jnp.int32, sc.shape, 0)
    coli = lax.broadcasted_iota(jnp.int32, sc.shape, 1)
    sc_ref[0] = jnp.where(is_new & (coli > rowi), NEG_BIG, sc)


def _dsa_s_select_kernel(sc_ref, keep_ref, *, topk, past):
    key = _sort_key(sc_ref[...])
    rows, ncol = key.shape

    def count_ge(trial):
        return jnp.sum(jnp.where(key >= trial, 1.0, 0.0), axis=1, keepdims=True)

    thr = _kth_largest_key(count_ge, rows, float(topk))
    above = key > thr
    room = float(topk) - jnp.sum(jnp.where(above, 1.0, 0.0), axis=1, keepdims=True)
    rowi = lax.broadcasted_iota(jnp.int32, key.shape, 0)
    coli = lax.broadcasted_iota(jnp.int32, key.shape, 1)
    visible = coli <= past + (rowi % V7X_SUBLANES)
    li = lax.broadcasted_iota(jnp.int32, (V7X_LANES, V7X_LANES), 0)
    lj = lax.broadcasted_iota(jnp.int32, (V7X_LANES, V7X_LANES), 1)
    upper = jnp.where(li <= lj, 1.0, 0.0).astype(BF16)
    seen = jnp.zeros((rows, 1), F32)
    for c in range(ncol // V7X_LANES):
        sl = slice(c * V7X_LANES, (c + 1) * V7X_LANES)
        tie = jnp.where(key[:, sl] == thr, 1.0, 0.0)
        rank = seen + _dot(tie.astype(BF16), upper)
        keep = (above[:, sl] | ((key[:, sl] == thr) & (rank <= room))) & visible[:, sl]
        keep_ref[:, sl] = jnp.where(keep, 1.0, 0.0)
        seen = seen + jnp.sum(tie, axis=1, keepdims=True)


def _dsa_s_attend_kernel(pt_ref, q_ref, keep_ref, expand_ref, struct_ref, *rest, nstep, npp, scale):
    kp_refs, vp_refs = rest[:npp], rest[npp:2 * npp]
    kn_ref, vn_ref, o_ref, m_ref, l_ref, acc_ref = rest[2 * npp:]
    p = pl.program_id(1)
    is_new = p == nstep
    psz = kn_ref.shape[1]

    @pl.when(p == 0)
    def _():
        m_ref[...] = jnp.full_like(m_ref, NEG_BIG)
        l_ref[...] = jnp.zeros_like(l_ref)
        acc_ref[...] = jnp.zeros_like(acc_ref)

    def flat(refs, new_ref, u):
        page = refs[u][...]
        if u == 0:
            page = jnp.where(is_new, new_ref[0], page)
        return page.reshape(-1, HEAD).astype(BF16)

    c2 = scale * math.log2(math.e)
    q = q_ref[0]
    struct = struct_ref[...]
    ss = []
    for u in range(npp):
        seen = _dot(keep_ref[0, :, u * psz:(u + 1) * psz], expand_ref[...])
        ss.append(_dot_nt(q, flat(kp_refs, kn_ref, u)) + jnp.where(seen * struct > 0.5, 0.0, NEG_BIG))
    m_i = m_ref[...]
    m_new = m_i
    for u in range(npp):
        m_new = jnp.maximum(m_new, jnp.max(ss[u], axis=1, keepdims=True))
    alpha = jnp.exp2((m_i - m_new) * c2)
    l_new = alpha * l_ref[...]
    acc = alpha * acc_ref[...]
    for u in range(npp):
        pr = jnp.exp2((ss[u] - m_new) * c2)
        l_new = l_new + jnp.sum(pr, axis=1, keepdims=True)
        acc = acc + _dot(pr.astype(BF16), flat(vp_refs, vn_ref, u))
    m_ref[...], l_ref[...], acc_ref[...] = m_new, l_new, acc

    @pl.when(is_new)
    def _():
        o_ref[0] = acc / l_new


def dsa_sample_attend(q_bf, qi_bf, wi, ki_new, k_new, v_new, ck, cv, cki, page_table,
                      *, layer_j, bsz, ts, nq, nkv, nidx):
    psz = ck.shape[2]
    npages = page_table.shape[1]
    npp = DSA_PAGES_PER_STEP
    assert npages % npp == 0
    nstep = npages // npp
    past = npages * psz
    topk = min(TOPK_MAX, (past + ts) // 4)
    grp = nq // nkv
    ncol = (nstep + 1) * npp * psz
    sub = V7X_SUBLANES
    padrows = lambda x, tail: jnp.pad(x.reshape((bsz, ts) + tail), ((0, 0), (0, psz - ts)) + ((0, 0),) * len(tail))

    def page_idx(u, ndim):
        def index(b, p, pt):
            return (layer_j, pt[b, jnp.minimum(p * npp + u, npages - 1)]) + (0,) * ndim
        return index

    scores = pl.pallas_call(
        functools.partial(_dsa_s_scores_kernel, ts=ts, nidx=nidx, nstep=nstep, npp=npp),
        grid_spec=pltpu.PrefetchScalarGridSpec(
            num_scalar_prefetch=1,
            grid=(bsz, nstep + 1),
            in_specs=[pl.BlockSpec((ts * nidx, HEAD), lambda b, p, pt: (b, 0)),
                      pl.BlockSpec((ts * nidx, 1), lambda b, p, pt: (b, 0))]
                     + [pl.BlockSpec((None, None, psz, HEAD), page_idx(u, 2)) for u in range(npp)]
                     + [pl.BlockSpec((1, psz, HEAD), lambda b, p, pt: (b, 0, 0))],
            out_specs=pl.BlockSpec((1, sub, npp * psz), lambda b, p, pt: (b, 0, p))),
        out_shape=jax.ShapeDtypeStruct((bsz, sub, ncol), F32),
        compiler_params=_cparams(("parallel", "arbitrary")),
        name="dsa_s_scores",
    )(page_table, qi_bf.reshape(bsz * ts * nidx, HEAD), wi.reshape(bsz * ts * nidx, 1),
      *([cki] * npp), padrows(ki_new, (HEAD,)))

    keep = pl.pallas_call(
        functools.partial(_dsa_s_select_kernel, topk=topk, past=past),
        out_shape=jax.ShapeDtypeStruct((bsz * sub, ncol), F32),
        compiler_params=pltpu.CompilerParams(vmem_limit_bytes=VMEM_LIMIT),
        name="dsa_s_select",
    )(scores.reshape(bsz * sub, ncol))
    nrow = nkv * ts * grp
    keep = jnp.broadcast_to(keep.reshape(bsz, 1, sub, 1, ncol)[:, :, :ts], (bsz, nkv, ts, grp, ncol))
    keep = keep.reshape(bsz, nrow, ncol).astype(BF16)
    q_all = q_bf.reshape(bsz, ts, nkv, grp, HEAD).transpose(0, 2, 1, 3, 4).reshape(bsz, nrow, HEAD)
    flat_col = np.arange(psz * nkv)
    expand = jnp.asarray(np.arange(psz)[:, None] == flat_col[None, :] // nkv, BF16)
    struct = jnp.asarray(np.arange(nrow)[:, None] // (ts * grp) == flat_col[None, :] % nkv, F32)
    kv_new_spec = pl.BlockSpec((1, psz, nkv, HEAD), lambda b, p, pt: (b, 0, 0, 0))
    const = lambda shape: pl.BlockSpec(shape, lambda b, p, pt: (0, 0))
    o = pl.pallas_call(
        functools.partial(_dsa_s_attend_kernel, nstep=nstep, npp=npp, scale=float(HEAD ** -0.5)),
        grid_spec=pltpu.PrefetchScalarGridSpec(
            num_scalar_prefetch=1,
            grid=(bsz, nstep + 1),
            in_specs=[pl.BlockSpec((1, nrow, HEAD), lambda b, p, pt: (b, 0, 0)),
                      pl.BlockSpec((1, nrow, npp * psz), lambda b, p, pt: (b, 0, p)),
                      const(expand.shape), const(struct.shape)]
                     + [pl.BlockSpec((None, None, psz, nkv, HEAD), page_idx(u, 3)) for u in range(npp)] * 2
                     + [kv_new_spec, kv_new_spec],
            out_specs=pl.BlockSpec((1, nrow, HEAD), lambda b, p, pt: (b, 0, 0)),
            scratch_shapes=[pltpu.VMEM((nrow, 1), F32), pltpu.VMEM((nrow, 1), F32),
                            pltpu.VMEM((nrow, HEAD), F32)]),
        out_shape=jax.ShapeDtypeStruct((bsz, nrow, HEAD), F32),
        compiler_params=_cparams(("parallel", "arbitrary")),
        name="dsa_s_attend",
    )(page_table, q_all, keep, expand, struct, *([ck] * npp), *([cv] * npp),
      padrows(k_new, (nkv, HEAD)), padrows(v_new, (nkv, HEAD)))
    o = o.reshape(bsz, nkv, ts, grp, HEAD).transpose(0, 2, 1, 3, 4).reshape(bsz * ts, nq * HEAD)
    return o.astype(BF16)


def kernel(x_prompt, x_sample, state_hgrn, state_rglru_h, state_rglru_conv, cache_k, cache_v, cache_kidx,
           page_table, p_prompt, p_sample, norm_mix, norm_mlp, norm_pe, w_in_ab, w_out_ab, hgrn_lb_logits,
           hgrn_out_norm, rg_conv_w, rg_conv_b, rg_wa, rg_ba, rg_wx, rg_bx, rg_lambda, w_in_c, w_out_c,
           c_q_norm, c_k_norm, idx_k_ln_g, idx_k_ln_b, w_up, w_down, w_pe, w_pg):
    bp, tp, d = x_prompt.shape
    bs, ts, _ = x_sample.shape
    depth = norm_mix.shape[0]
    a_width = hgrn_out_norm.shape[1]
    b_width = rg_lambda.shape[1]
    nh_a = a_width // HEAD
    nkv = cache_k.shape[3]
    nq = w_out_c.shape[1] // HEAD
    nidx = (w_in_c.shape[2] - nq * HEAD - 2 * nkv * HEAD - HEAD) // (HEAD + 1)
    past = page_table.shape[1] * cache_k.shape[2]
    c_main = nq * HEAD + 2 * nkv * HEAD + nidx * HEAD
    pe_dim = p_prompt.shape[-1]

    hp = x_prompt.reshape(bp * tp, d)
    hs = x_sample.reshape(bs * ts, d)
    tab_p = _rope_tables(jnp.arange(tp, dtype=jnp.int32))
    tab_s = _rope_tables(past + jnp.arange(ts, dtype=jnp.int32))
    tab_s = tuple(jnp.tile(tb, (bs, 1)) for tb in tab_s)
    ts_pad = 16
    pad_s = lambda x: jnp.pad(x.reshape(bs, ts, -1), ((0, 0), (0, ts_pad - ts), (0, 0))).reshape(bs * ts_pad, -1)
    unpad_s = lambda x: x.reshape(bs, ts_pad, -1)[:, :ts].reshape(bs * ts, -1)

    outs = {k: [] for k in ("hg_p", "rh_p", "rc_p", "k_p", "v_p", "ki_p",
                            "hg_s", "rh_s", "rc_s", "k_s", "v_s", "ki_s")}
    wd_bf = w_down.astype(BF16)
    w_in_c_t = w_in_c.transpose(0, 2, 1)
    pp_all = p_prompt.reshape(depth, -1, pe_dim).astype(BF16)
    ps_all = p_sample.reshape(depth, -1, pe_dim).astype(BF16)
    for layer in range(depth):
        j = layer // 2
        xpn = rmsnorm_bf16(hp, norm_mix[layer])
        xsn = rmsnorm_bf16(hs, norm_mix[layer])
        if layer % 2 == 0:
            gb_blk = 4 * a_width // b_width
            x_blk = gb_blk + 1
            rg_args = (rg_conv_w[j], rg_conv_b[j], rg_wa[j], rg_ba[j], rg_wx[j], rg_bx[j], rg_lambda[j])
            n_in = w_in_ab.shape[2]

            proj, w_in = matmul(xpn, w_in_ab, layer=j, emit_bf16=True)
            oa, sa = hgrn(proj, hgrn_lb_logits, hgrn_out_norm[j], jnp.zeros((bp, nh_a, HEAD, HEAD), F32),
                          bsz=bp, t=tp, layer_j=j, r=min(128, tp))
            ob, hl, cn = rglru(proj, jnp.zeros((bp, CONV_W - 1, b_width), F32), jnp.zeros((bp, b_width), F32),
                               *rg_args, bsz=bp, t=tp, r=min(256, tp), gb_blk=gb_blk, x_blk=x_blk,
                               first_pos_zero=True)
            hp, w_out = matmul(jnp.concatenate([oa, ob], axis=1), w_out_ab, layer=j, emit_bf16=True,
                               epilogue="residual", res=hp)
            outs["hg_p"].append(sa)
            outs["rh_p"].append(hl.reshape(bp, b_width))
            outs["rc_p"].append(cn)

            proj = pad_s(matmul(xsn, w_in, n_out=n_in))
            oa, sa = hgrn(proj, hgrn_lb_logits, hgrn_out_norm[j], state_hgrn[j],
                          bsz=bs, t=ts_pad, layer_j=j, r=ts_pad, t_valid=ts)
            ob, hl, cn = rglru(proj, state_rglru_conv[j], state_rglru_h[j], *rg_args, bsz=bs, t=ts_pad,
                               r=ts_pad, gb_blk=gb_blk, x_blk=x_blk, first_pos_zero=False, t_valid=ts)
            mix = unpad_s(jnp.concatenate([oa, ob], axis=1))
            hs = matmul(mix, w_out, n_out=d, epilogue="residual", res=hs)
            outs["hg_s"].append(sa)
            outs["rh_s"].append(hl.reshape(bs, b_width))
            outs["rc_s"].append(cn)
        else:
            norms = (c_q_norm[j], c_k_norm[j], idx_k_ln_g[j], idx_k_ln_b[j])
            pm, w_main = matmul(xpn, w_in_c_t, layer=j, w_t=True, n_out=c_main, emit_bf16=True)
            ptl, w_tail = matmul(xpn, w_in_c_t, layer=j, w_t=True, n_out=2 * HEAD, col0=c_main, emit_bf16=True)
            post = dsa_post(pm, ptl, tab_p, *norms, nq=nq, nkv=nkv, nidx=nidx, t_tab=tp)
            q_bf, qi_bf, kk, vv, ki, wi, k_bf, vt_bf, ki_bf = post
            o = dsa_prompt_attend(q_bf, qi_bf, wi.T, ki_bf, k_bf, vt_bf, bsz=bp, t=tp, nq=nq, nkv=nkv, nidx=nidx)
            hp, w_out = matmul(o, w_out_c, layer=j, emit_bf16=True, epilogue="residual", res=hp)
            outs["k_p"].append(kk.reshape(bp, tp, nkv, HEAD))
            outs["v_p"].append(vv.reshape(bp, tp, nkv, HEAD))
            outs["ki_p"].append(ki.reshape(bp, tp, HEAD))

            post = dsa_post(matmul(xsn, w_main, w_t=True, n_out=c_main), matmul(xsn, w_tail, w_t=True, n_out=2 * HEAD),
                            tab_s, *norms, nq=nq, nkv=nkv, nidx=nidx, t_tab=bs * ts)
            q_bf, qi_bf, kk, vv, ki, wi, _, _, _ = post
            o = dsa_sample_attend(q_bf, qi_bf, wi, ki, kk, vv, cache_k, cache_v, cache_kidx,
                                  page_table, layer_j=j, bsz=bs, ts=ts, nq=nq, nkv=nkv, nidx=nidx)
            hs = matmul(o, w_out, n_out=d, epilogue="residual", res=hs)
            outs["k_s"].append(kk.reshape(bs, ts, nkv, HEAD))
            outs["v_s"].append(vv.reshape(bs, ts, nkv, HEAD))
            outs["ki_s"].append(ki.reshape(bs, ts, HEAD))

        d_ff = w_up.shape[2]
        mid, wu = matmul(rmsnorm_bf16(hp, norm_mlp[layer]), w_up, layer=layer, emit_bf16=True,
                         epilogue="relu2", out_dtype=BF16)
        hp = matmul(mid, wd_bf, layer=layer, epilogue="residual", res=hp)
        hp, wpg = matmul(rmsnorm_bf16(hp, norm_pe[layer]), w_pg, layer=layer, emit_bf16=True, epilogue="pe_gate",
                         res=hp, p=pp_all[layer], w_pe=w_pe, pe_layer=layer)
        mid = matmul(rmsnorm_bf16(hs, norm_mlp[layer]), wu, n_out=d_ff, epilogue="relu2", out_dtype=BF16)
        hs = matmul(mid, wd_bf, layer=layer, epilogue="residual", res=hs)
        hs = matmul(rmsnorm_bf16(hs, norm_pe[layer]), wpg, n_out=d, epilogue="pe_gate", res=hs,
                    p=ps_all[layer], w_pe=w_pe, pe_layer=layer)

    st = lambda name: jnp.stack(outs[name])
    return (hp.reshape(bp, tp, d), hs.reshape(bs, ts, d),
            st("hg_p"), st("rh_p"), st("rc_p"), st("k_p"), st("v_p"), st("ki_p"),
            st("hg_s"), st("rh_s"), st("rc_s"), st("k_s"), st("v_s"), st("ki_s"))
```

```python
import functools
import math

import jax
import jax.numpy as jnp
import numpy as np
from jax import lax
from jax.experimental import pallas as pl
from jax.experimental.pallas import tpu as pltpu

F32 = jnp.float32
BF16 = jnp.bfloat16

V7X_LANES = 128
V7X_SUBLANES = 8
V7X_VMEM_BYTES = 64 * 1024 * 1024
VMEM_LIMIT = V7X_VMEM_BYTES - 8 * 1024 * 1024

EPS = 1e-6
NEG_BIG = -1e30
RG_C = 8.0
ROPE_THETA = 500000.0
CONV_W = 4
TOPK_MAX = 256

HEAD = 128
ROPE_HALF = HEAD // 8
DSA_KV_GROUPS_PER_BODY = 8
HGRN_CHUNK_ROWS = 128
RGLRU_BLOCK_ROWS = 256
SAMPLE_CHUNK_ROWS = 16
DSA_PAGES_PER_STEP = 8
DSA_TQ = 256
HGRN_HEADS_PER_STEP = 16

_NT = (((1,), (1,)), ((), ()))
_TN = (((0,), (0,)), ((), ()))


def _cparams(sem):
    return pltpu.CompilerParams(dimension_semantics=sem, vmem_limit_bytes=VMEM_LIMIT)


def _dot(a, b):
    return jnp.dot(a, b, preferred_element_type=F32)


def _dot_nt(a, b):
    return lax.dot_general(a, b, _NT, preferred_element_type=F32)


def _dot_tn(a, b):
    return lax.dot_general(a, b, _TN, preferred_element_type=F32)


def _rmsnorm_kernel(x_ref, g_ref, o_ref):
    x = x_ref[...]
    y = x * lax.rsqrt(jnp.mean(x * x, axis=-1, keepdims=True) + EPS)
    o_ref[...] = (y * g_ref[...]).astype(o_ref.dtype)


def rmsnorm_bf16(x, g):
    m, d = x.shape
    tm = min(m, 512)
    return pl.pallas_call(
        _rmsnorm_kernel,
        grid=(m // tm,),
        in_specs=[pl.BlockSpec((tm, d), lambda i: (i, 0)),
                  pl.BlockSpec((1, d), lambda i: (0, 0))],
        out_specs=pl.BlockSpec((tm, d), lambda i: (i, 0)),
        out_shape=jax.ShapeDtypeStruct((m, d), BF16),
        compiler_params=_cparams(("parallel",)),
        name="rmsnorm",
    )(x, g.reshape(1, d))


def _mm_kernel(*refs, nk, epilogue, w_t, emit):
    x_ref, w_ref = refs[0], refs[1]
    pos = 2
    res_ref = p_ref = wpe_ref = None
    if epilogue in ("residual", "pe_gate"):
        res_ref = refs[pos]
        pos += 1
    if epilogue == "pe_gate":
        p_ref, wpe_ref = refs[pos], refs[pos + 1]
        pos += 2
    o_ref = refs[pos]
    pos += 1
    wb_ref = None
    if emit:
        wb_ref = refs[pos]
        pos += 1
    acc_ref = refs[pos] if nk > 1 else None

    def finish(acc):
        if epilogue == "none":
            out = acc
        elif epilogue == "relu2":
            r = jnp.maximum(acc, 0.0)
            out = r * r
        elif epilogue == "residual":
            out = res_ref[...] + acc
        else:
            pe = _dot(p_ref[...], wpe_ref[...].astype(BF16))
            out = res_ref[...] + pe * jax.nn.sigmoid(acc)
        o_ref[...] = out.astype(o_ref.dtype)

    wb = w_ref[...].astype(BF16)
    if emit:
        wb_ref[...] = wb
    prod = _dot_nt(x_ref[...], wb) if w_t else _dot(x_ref[...], wb)
    if nk == 1:
        finish(prod)
    else:
        k = pl.program_id(2)

        @pl.when(k == 0)
        def _():
            acc_ref[...] = jnp.zeros_like(acc_ref)

        acc_ref[...] += prod

        @pl.when(k == nk - 1)
        def _():
            finish(acc_ref[...])


MIB = 1024 * 1024
MM_TK = 4096
MM_SMALL_M = 64
MM_SMALL_M_TN = {True: 512, False: 2048}
MM_TILE_F32W = (2048, 256)
MM_TILE_BF16W = (1024, 512)
X_SINGLE_BUFFER_BYTES = 16 * MIB


def _mm_tiles(m, n, k, w_f32):
    tk = min(k, MM_TK)
    if m <= MM_SMALL_M:
        return m, min(n, MM_SMALL_M_TN[w_f32]), tk
    tm, tn = MM_TILE_F32W if (w_f32 and k == tk) else MM_TILE_BF16W
    return min(m, tm), min(n, tn), tk


def matmul(x, w, *, layer=None, n_out=None, col0=0, w_t=False, emit_bf16=False, epilogue="none", res=None,
           p=None, w_pe=None, pe_layer=None, out_dtype=F32):
    m, k = x.shape
    n = (w.shape[-2] if w_t else w.shape[-1]) if n_out is None else n_out
    tm, tn, tk = _mm_tiles(m, n, k, w.dtype == F32)
    while n % tn:
        tn //= 2
    assert m % tm == 0 and k % tk == 0 and col0 % tn == 0
    nk = k // tk
    nj = n // tn
    jb = col0 // tn
    lead = () if layer is None else (layer,)
    pe_lead = () if pe_layer is None else (pe_layer,)
    none = lambda t: (None,) * len(t)
    wtile = (tn, tk) if w_t else (tk, tn)
    worder = (lambda kk, jj: (jj, kk)) if w_t else (lambda kk, jj: (kk, jj))
    if nk == 1:
        grid = (m // tm, nj)
        xmap = lambda i, j: (i, 0)
        wmap = lambda i, j: lead + worder(0, jb + j)
        omap = lambda i, j: (i, j)
        pmap = lambda i, j: (i, 0)
        pemap = lambda i, j: pe_lead + (0, j)
        emap = lambda i, j: worder(0, jnp.where(i == 0, j, nj))
        sem = ("arbitrary", "arbitrary") if emit_bf16 else ("parallel", "parallel")
    else:
        assert not emit_bf16
        grid = (m // tm, nj, nk)
        xmap = lambda i, j, kk: (i, kk)
        wmap = lambda i, j, kk: lead + worder(kk, jb + j)
        omap = lambda i, j, kk: (i, j)
        pmap = lambda i, j, kk: (i, 0)
        pemap = lambda i, j, kk: pe_lead + (0, j)
        sem = ("parallel", "parallel", "arbitrary")
    xspec = (pl.BlockSpec((tm, tk), xmap, pipeline_mode=pl.Buffered(1))
             if tm * tk * x.dtype.itemsize >= X_SINGLE_BUFFER_BYTES else pl.BlockSpec((tm, tk), xmap))
    in_specs = [xspec, pl.BlockSpec(none(lead) + wtile, wmap)]
    args = [x, w]
    if epilogue in ("residual", "pe_gate"):
        in_specs.append(pl.BlockSpec((tm, tn), omap))
        args.append(res)
    if epilogue == "pe_gate":
        pe_dim = p.shape[1]
        in_specs.append(pl.BlockSpec((tm, pe_dim), pmap))
        in_specs.append(pl.BlockSpec(none(pe_lead) + (pe_dim, tn), pemap))
        args += [p, w_pe]
    out_specs = [pl.BlockSpec((tm, tn), omap)]
    out_shape = [jax.ShapeDtypeStruct((m, n), out_dtype)]
    if emit_bf16:
        out_specs.append(pl.BlockSpec(wtile, emap))
        out_shape.append(jax.ShapeDtypeStruct((n + tn, k) if w_t else (k, n + tn), BF16))
    scratch = [pltpu.VMEM((tm, tn), F32)] if nk > 1 else []
    outs = pl.pallas_call(
        functools.partial(_mm_kernel, nk=nk, epilogue=epilogue, w_t=w_t, emit=emit_bf16),
        grid=grid,
        in_specs=in_specs,
        out_specs=out_specs,
        out_shape=out_shape,
        scratch_shapes=scratch,
        compiler_params=_cparams(sem),
        name="mm_" + epilogue,
    )(*args)
    return outs if emit_bf16 else outs[0]


def _hgrn_level_matrices(r):
    idx = np.arange(r)
    t = idx[:, None]
    u = idx[None, :]
    mats = []
    lvl = 2
    while lvl <= r:
        h = lvl // 2
        pos = t % lvl
        mid = t - pos + h
        upper = pos >= h
        m_up = (u >= mid) & (u <= t)
        m_lo = (u > t) & (u <= mid - 1)
        mats.append(np.where(upper, m_up, m_lo))
        lvl *= 2
    mats.append(u <= t)
    mats.append(u > t)
    return np.concatenate(mats, axis=0).astype(np.float32)


def _hgrn_kernel(q_ref, f_ref, i_ref, ga_ref, lbl_ref, an_ref, s0_ref, mall_ref,
                 o_ref, s_ref, st_ref, *, r, nhb, layer_j, t_valid):
    c = pl.program_id(2)
    nc = pl.num_programs(2)
    nlev = int(math.log2(r))

    @pl.when(c == 0)
    def _():
        for hh in range(nhb):
            st_ref[hh] = s0_ref[0, hh].T

    lbl = lbl_ref[...]
    e = jnp.exp(lbl - jnp.max(lbl, axis=0, keepdims=True))
    soft = e / jnp.sum(e, axis=0, keepdims=True)
    lb_all = jnp.zeros((1, nhb * HEAD), F32)
    for jj in range(1, layer_j + 1):
        lb_all = lb_all + soft[jj:jj + 1, :]

    z = f_ref[...]
    logf = jnp.log1p(lb_all * jnp.exp(-z)) - _softplus(-z)
    kk_all = (1.0 - lb_all) * jax.nn.sigmoid(-z)
    qq_all = jax.nn.silu(q_ref[...])
    vv_all = i_ref[...]
    if t_valid is not None:
        live = lax.broadcasted_iota(jnp.int32, (r, 1), 0) < t_valid
        logf = jnp.where(live, logf, 0.0)
        kk_all = jnp.where(live, kk_all, 0.0)
        qq_all = jnp.where(live, qq_all, 0.0)
        vv_all = jnp.where(live, vv_all, 0.0)

    hi = logf.astype(BF16)
    mid = (logf - hi.astype(F32)).astype(BF16)
    xs_all = _dot(mall_ref[...], hi) + _dot(mall_ref[...], mid)

    ti = lax.broadcasted_iota(jnp.int32, (r, r), 0)
    si = lax.broadcasted_iota(jnp.int32, (r, r), 1)
    diag = ti == si
    rowi = lax.broadcasted_iota(jnp.int32, (r, 1), 0)
    ups, pairs = [], []
    for lv in range(nlev):
        blk = 2 << lv
        h = blk // 2
        ups.append((rowi & (blk - 1)) >= h)
        pairs.append(((ti >> (lv + 1)) == (si >> (lv + 1))) & ((ti & (blk - 1)) >= h) & ((si & (blk - 1)) < h))

    ga = ga_ref[...]
    an = an_ref[...]
    sls = [slice(hh * HEAD, (hh + 1) * HEAD) for hh in range(nhb)]
    xc0, xe0 = nlev * r, (nlev + 1) * r
    sts = [st_ref[hh] for hh in range(nhb)]
    vbs = [vv_all[:, sl].astype(BF16) for sl in sls]
    o_st = [_dot_nt((qq_all[:, sl] * jnp.exp(xs_all[xc0:xc0 + r, sl])).astype(BF16), sts[hh].astype(BF16))
            for hh, sl in enumerate(sls)]
    upd = [_dot_tn(vbs[hh], (kk_all[:, sl] * jnp.exp(xs_all[xe0:xe0 + r, sl])).astype(BF16))
           for hh, sl in enumerate(sls)]
    atts = []
    for hh, sl in enumerate(sls):
        qq, kk = qq_all[:, sl], kk_all[:, sl]
        att = jnp.where(diag, _dot_nt(qq.astype(BF16), kk.astype(BF16)), 0.0)
        for lv in range(nlev):
            w = jnp.exp(xs_all[lv * r:(lv + 1) * r, sl])
            qt = jnp.where(ups[lv], qq * w, 0.0).astype(BF16)
            kt = jnp.where(ups[lv], 0.0, kk * w).astype(BF16)
            att = att + jnp.where(pairs[lv], _dot_nt(qt, kt), 0.0)
        atts.append(att.astype(BF16))
    for hh, sl in enumerate(sls):
        o = _dot(atts[hh], vbs[hh]) + o_st[hh]
        g_end = jnp.exp(xs_all[xc0 + r - 1:xc0 + r, sl])
        st_ref[hh] = g_end * sts[hh] + upd[hh]
        on = o * lax.rsqrt(jnp.mean(o * o, axis=-1, keepdims=True) + EPS) * an[:, sl]
        o_ref[:, sl] = (on * jax.nn.silu(ga[:, sl])).astype(o_ref.dtype)

    @pl.when(c == nc - 1)
    def _():
        for hh in range(nhb):
            s_ref[0, hh] = st_ref[hh].T


def hgrn(proj, lb_logits, a_norm, s0, *, bsz, t, layer_j, r, t_valid=None):
    width = a_norm.shape[0]
    nh = width // HEAD
    nhb = HGRN_HEADS_PER_STEP
    ng = nh // nhb
    n_ab = lb_logits.shape[0]
    nchunk = t // r
    mall = jnp.asarray(_hgrn_level_matrices(r), BF16)
    bw = nhb * HEAD
    col = lambda off: (lambda b, h, c: (b * nchunk + c, off * ng + h))
    kern = functools.partial(_hgrn_kernel, r=r, nhb=nhb, layer_j=layer_j, t_valid=t_valid)
    return pl.pallas_call(
        kern,
        grid=(bsz, ng, nchunk),
        in_specs=[pl.BlockSpec((r, bw), col(0)),
                  pl.BlockSpec((r, bw), col(1)),
                  pl.BlockSpec((r, bw), col(2)),
                  pl.BlockSpec((r, bw), col(3)),
                  pl.BlockSpec((n_ab, bw), lambda b, h, c: (0, h)),
                  pl.BlockSpec((1, bw), lambda b, h, c: (0, h)),
                  pl.BlockSpec((1, nhb, HEAD, HEAD), lambda b, h, c: (b, h, 0, 0)),
                  pl.BlockSpec(mall.shape, lambda b, h, c: (0, 0))],
        out_specs=[pl.BlockSpec((r, bw), lambda b, h, c: (b * nchunk + c, h)),
                   pl.BlockSpec((1, nhb, HEAD, HEAD), lambda b, h, c: (b, h, 0, 0))],
        out_shape=[jax.ShapeDtypeStruct((bsz * t, width), BF16),
                   jax.ShapeDtypeStruct((bsz, nh, HEAD, HEAD), F32)],
        scratch_shapes=[pltpu.VMEM((nhb, HEAD, HEAD), F32)],
        compiler_params=_cparams(("parallel", "parallel", "arbitrary")),
        name="hgrn",
    )(proj, proj, proj, proj, lb_logits, a_norm.reshape(1, width), s0, mall)


def _softplus(x):
    return jnp.maximum(x, 0.0) + jnp.log1p(jnp.exp(-jnp.abs(x)))


def _rglru_kernel(x_ref, gb_ref, c0_ref, h0_ref, cw_ref, cb_ref, wa_ref, ba_ref, wx_ref, bx_ref,
                  lam_ref, o_ref, hl_ref, cn_ref, xp_ref, a_ref, b_ref, hc_ref,
                  *, r, nblk, first_pos_zero, t_valid):
    i = pl.program_id(1)
    ni = pl.num_programs(1)
    pad = V7X_SUBLANES

    @pl.when(i == 0)
    def _():
        xp_ref[0:pad, :] = c0_ref[0]
        hc_ref[...] = h0_ref[0]

    x = x_ref[...]
    xp_ref[pad:pad + r, :] = x
    cw = cw_ref[...]
    y = cw[0:1, :] * xp_ref[pad - 3:pad - 3 + r, :]
    y = y + cw[1:2, :] * xp_ref[pad - 2:pad - 2 + r, :]
    y = y + cw[2:3, :] * xp_ref[pad - 1:pad - 1 + r, :]
    y = cb_ref[...] + (y + cw[3:4, :] * x)

    last = r if t_valid is None else t_valid
    cn_ref[0] = xp_ref[pad + last - 3:pad + last, :]
    xp_ref[0:pad, :] = xp_ref[r:r + pad, :]

    ra = []
    rx = []
    for n in range(nblk):
        ys = y[:, n * HEAD:(n + 1) * HEAD].astype(BF16)
        ra.append(_dot(ys, wa_ref[n]))
        rx.append(_dot(ys, wx_ref[n]))
    rg = jax.nn.sigmoid(jnp.concatenate(ra, axis=1) + ba_ref[...])
    gi = jax.nn.sigmoid(jnp.concatenate(rx, axis=1) + bx_ref[...])
    log_a = (-RG_C * rg) * _softplus(-lam_ref[...])
    a = jnp.exp(log_a)
    th = jnp.tanh(log_a)
    mult = jnp.sqrt((-2.0 * th) / (1.0 - th))
    rowi = lax.broadcasted_iota(jnp.int32, (r, 1), 0)
    if first_pos_zero:
        mult = jnp.where((rowi == 0) & (i == 0), 1.0, mult)
    bterm = mult * gi * y
    if t_valid is not None:
        live = rowi < t_valid
        a = jnp.where(live, a, 1.0)
        bterm = jnp.where(live, bterm, 0.0)
    a_ref[...] = a
    b_ref[...] = bterm

    def step(tt, h):
        h = a_ref[pl.ds(tt, 1), :] * h + b_ref[pl.ds(tt, 1), :]
        b_ref[pl.ds(tt, 1), :] = h
        return h

    h_last = lax.fori_loop(0, r, step, hc_ref[...], unroll=8)
    hc_ref[...] = h_last
    o_ref[...] = (jax.nn.gelu(gb_ref[...]) * b_ref[...]).astype(o_ref.dtype)

    @pl.when(i == ni - 1)
    def _():
        hl_ref[0] = h_last


def rglru(proj, conv0, h0, conv_w, conv_b, wa, ba, wx, bx, lam, *, bsz, t, r, gb_blk, x_blk,
          first_pos_zero, t_valid=None):
    w = lam.shape[0]
    nblk = wa.shape[0]
    nrb = t // r
    pad = V7X_SUBLANES
    c0 = jnp.pad(conv0, ((0, 0), (pad - (CONV_W - 1), 0), (0, 0)))
    row = lambda v: v.reshape(1, w)
    kern = functools.partial(_rglru_kernel, r=r, nblk=nblk, first_pos_zero=first_pos_zero, t_valid=t_valid)
    full = lambda shape: pl.BlockSpec(shape, lambda b, i: (0,) * len(shape))
    return pl.pallas_call(
        kern,
        grid=(bsz, nrb),
        in_specs=[pl.BlockSpec((r, w), lambda b, i: (b * nrb + i, x_blk)),
                  pl.BlockSpec((r, w), lambda b, i: (b * nrb + i, gb_blk)),
                  pl.BlockSpec((1, pad, w), lambda b, i: (b, 0, 0)),
                  pl.BlockSpec((1, 1, w), lambda b, i: (b, 0, 0)),
                  full((CONV_W, w)), full((1, w)),
                  full(wa.shape), full((1, w)), full(wx.shape), full((1, w)), full((1, w))],
        out_specs=[pl.BlockSpec((r, w), lambda b, i: (b * nrb + i, 0)),
                   pl.BlockSpec((1, 1, w), lambda b, i: (b, 0, 0)),
                   pl.BlockSpec((1, CONV_W - 1, w), lambda b, i: (b, 0, 0))],
        out_shape=[jax.ShapeDtypeStruct((bsz * t, w), BF16),
                   jax.ShapeDtypeStruct((bsz, 1, w), F32),
                   jax.ShapeDtypeStruct((bsz, CONV_W - 1, w), F32)],
        scratch_shapes=[pltpu.VMEM((r + pad, w), F32), pltpu.VMEM((r, w), F32),
                        pltpu.VMEM((r, w), F32), pltpu.VMEM((1, w), F32)],
        compiler_params=_cparams(("parallel", "arbitrary")),
        name="rglru",
    )(proj, proj, c0, h0.reshape(bsz, 1, w), conv_w, row(conv_b), wa.astype(BF16), row(ba),
      wx.astype(BF16), row(bx), row(lam))


def _rope_tables(pos):
    rd = HEAD // 4
    half = rd // 2
    inv = jnp.exp(-math.log(ROPE_THETA) * jnp.arange(half, dtype=F32) * (2.0 / rd))
    ang = pos.astype(F32)[:, None] * inv[None, :]
    cos, sin = jnp.cos(ang), jnp.sin(ang)
    n = pos.shape[0]
    ones = jnp.ones((n, HEAD - rd), F32)
    zeros = jnp.zeros((n, HEAD - rd), F32)
    zh = jnp.zeros((n, half), F32)
    c = jnp.concatenate([cos, cos, ones], axis=1)
    a = jnp.concatenate([-sin, zh, zeros], axis=1)
    b = jnp.concatenate([zh, sin, zeros], axis=1)
    return c, a, b


def _rope(x, c, a, b):
    return x * c + pltpu.roll(x, HEAD - ROPE_HALF, 1) * a + pltpu.roll(x, ROPE_HALF, 1) * b


def _dsa_post_kernel(pm_ref, pt_ref, c_ref, a_ref, b_ref, qn_ref, kn_ref, kg_ref, kb_ref,
                     q_ref, qi_ref, k_ref, v_ref, ki_ref, wi_ref, kbf_ref, vbf_ref, kibf_ref,
                     *, nq, nkv, nidx, wi_scale):
    c, a, b = c_ref[...], a_ref[...], b_ref[...]
    qn, kn = qn_ref[...], kn_ref[...]
    off = 0
    for h in range(nq):
        x = pm_ref[:, off + h * HEAD:off + (h + 1) * HEAD]
        y = x * lax.rsqrt(jnp.mean(x * x, axis=-1, keepdims=True) + EPS) * qn
        q_ref[:, h * HEAD:(h + 1) * HEAD] = _rope(y, c, a, b).astype(q_ref.dtype)
    off += nq * HEAD
    for h in range(nkv):
        x = pm_ref[:, off + h * HEAD:off + (h + 1) * HEAD]
        y = x * lax.rsqrt(jnp.mean(x * x, axis=-1, keepdims=True) + EPS) * kn
        kr = _rope(y, c, a, b)
        k_ref[:, h, :] = kr
        kbf_ref[:, h * HEAD:(h + 1) * HEAD] = kr.astype(BF16)
    off += nkv * HEAD
    vv = pm_ref[:, off:off + nkv * HEAD]
    for h in range(nkv):
        v_ref[:, h, :] = vv[:, h * HEAD:(h + 1) * HEAD]
    vbf_ref[0] = vv.T.astype(BF16)
    off += nkv * HEAD
    for h in range(nidx):
        x = pm_ref[:, off + h * HEAD:off + (h + 1) * HEAD]
        qi_ref[:, h * HEAD:(h + 1) * HEAD] = _rope(x, c, a, b).astype(qi_ref.dtype)
    x = pt_ref[:, 0:HEAD]
    mu = jnp.mean(x, axis=-1, keepdims=True)
    xc = x - mu
    y = xc * lax.rsqrt(jnp.mean(xc * xc, axis=-1, keepdims=True) + EPS)
    kir = _rope(y * kg_ref[...] + kb_ref[...], c, a, b)
    ki_ref[...] = kir
    kibf_ref[...] = kir.astype(BF16)
    wi_ref[...] = pt_ref[:, HEAD:HEAD + nidx] * wi_scale


def dsa_post(proj_main, proj_tail, tables, qn, kn, kg, kb, *, nq, nkv, nidx, t_tab):
    m = proj_main.shape[0]
    tm = min(m, DSA_TQ, t_tab)
    ntab = t_tab // tm
    wm = proj_main.shape[1]
    wt = proj_tail.shape[1]
    rowspec = lambda wdt: pl.BlockSpec((tm, wdt), lambda i: (i, 0))
    tabspec = pl.BlockSpec((tm, HEAD), lambda i: (i % ntab, 0))
    vec = pl.BlockSpec((1, HEAD), lambda i: (0, 0))
    kern = functools.partial(_dsa_post_kernel, nq=nq, nkv=nkv, nidx=nidx,
                             wi_scale=float((nidx * HEAD) ** -0.5))
    kv_spec = pl.BlockSpec((tm, nkv, HEAD), lambda i: (i, 0, 0))
    kv_shape = jax.ShapeDtypeStruct((m, nkv, HEAD), F32)
    vt_spec = pl.BlockSpec((1, nkv * HEAD, tm), lambda i: (i, 0, 0))
    vt_shape = jax.ShapeDtypeStruct((m // tm, nkv * HEAD, tm), BF16)
    flat = lambda wdt, dt: (rowspec(wdt), jax.ShapeDtypeStruct((m, wdt), dt))
    outs = [flat(nq * HEAD, BF16), flat(nidx * HEAD, BF16), (kv_spec, kv_shape), (kv_spec, kv_shape),
            flat(HEAD, F32), flat(nidx, F32), flat(nkv * HEAD, BF16), (vt_spec, vt_shape), flat(HEAD, BF16)]
    return pl.pallas_call(
        kern,
        grid=(m // tm,),
        in_specs=[rowspec(wm), rowspec(wt), tabspec, tabspec, tabspec, vec, vec, vec, vec],
        out_specs=[o[0] for o in outs],
        out_shape=[o[1] for o in outs],
        compiler_params=_cparams(("parallel",)),
        name="dsa_post",
    )(proj_main, proj_tail, *tables, qn.reshape(1, HEAD), kn.reshape(1, HEAD),
      kg.reshape(1, HEAD), kb.reshape(1, HEAD))


def _sort_key(x):
    bits = pltpu.bitcast(x, jnp.int32)
    return jnp.where(bits < 0, bits ^ jnp.int32(0x7FFFFFFF), bits)


def _kth_largest_key(count_ge, rows, k):
    def body(it, thr):
        bit = lax.shift_left(jnp.int32(1), jnp.int32(31) - it)
        trial = thr + bit
        return jnp.where(count_ge(trial) >= k, trial, thr)

    shape = rows if isinstance(rows, tuple) else (rows, 1)
    init = jnp.full(shape, jnp.iinfo(jnp.int32).min, jnp.int32)
    return lax.fori_loop(0, 32, body, init)


def _dsa_prompt_kernel(qi_ref, wit_ref, kib_ref, q_ref, kb_ref, vt_ref, o_ref, key_ref, bias_ref,
                       *, tq, nidx, nq, nkv, topk, scale):
    i = pl.program_id(1)
    ntile = i + 1
    keyi = lax.broadcasted_iota(jnp.int32, (tq, tq), 0)
    qryi = lax.broadcasted_iota(jnp.int32, (tq, tq), 1)
    tri = keyi <= qryi

    def idx_tile(j, carry):
        kt = kib_ref[pl.ds(pl.multiple_of(j * tq, tq), tq), :]
        acc = jnp.zeros((tq, tq), F32)
        for h in range(nidx):
            d = _dot_nt(kt, qi_ref[:, h * HEAD:(h + 1) * HEAD])
            acc = acc + jnp.maximum(d, 0.0) * wit_ref[h:h + 1, :]
        acc = jnp.where(tri | (j < i), acc, NEG_BIG)
        key_ref[j] = _sort_key(acc)
        return carry

    lax.fori_loop(0, ntile, idx_tile, 0)

    sub = V7X_SUBLANES

    def count_ge(trial):
        def tile(j, cnt):
            hit = jnp.where(key_ref[j] >= trial, 1.0, 0.0)
            return cnt + jnp.sum(hit.reshape(tq // sub, sub, tq), axis=0)

        return jnp.sum(lax.fori_loop(0, ntile, tile, jnp.zeros((sub, tq), F32)), axis=0, keepdims=True)

    thr = _kth_largest_key(count_ge, (1, tq), float(topk))

    def count_gt(j, cnt):
        return cnt + jnp.sum(jnp.where(key_ref[j] > thr, 1.0, 0.0), axis=0, keepdims=True)

    room = float(topk) - lax.fori_loop(0, ntile, count_gt, jnp.zeros((1, tq), F32))
    lower = jnp.where(qryi <= keyi, 1.0, 0.0).astype(BF16)

    def bias_tile(j, seen):
        k = key_ref[j]
        tie = jnp.where(k == thr, 1.0, 0.0)
        rank = seen + _dot(lower, tie.astype(BF16))
        keep = ((k > thr) | ((k == thr) & (rank <= room))) & (tri | (j < i))
        bias_ref[j] = jnp.where(keep, 0.0, NEG_BIG)
        return seen + jnp.sum(tie, axis=0, keepdims=True)

    lax.fori_loop(0, ntile, bias_tile, jnp.zeros((1, tq), F32))

    c2 = scale * math.log2(math.e)
    grp = nq // nkv
    ngb = DSA_KV_GROUPS_PER_BODY
    for n0 in range(0, nkv, ngb):
        heads = [(n, n * grp + g) for n in range(n0, n0 + ngb) for g in range(grp)]
        qs = [q_ref[:, h * HEAD:(h + 1) * HEAD] for _, h in heads]

        def att_tile(j, carry, n0=n0, heads=heads, qs=qs):
            r0 = pl.multiple_of(j * tq, tq)
            kts = {n: kb_ref[pl.ds(r0, tq), n * HEAD:(n + 1) * HEAD] for n in range(n0, n0 + ngb)}
            vts = {n: vt_ref[j, n * HEAD:(n + 1) * HEAD, :] for n in range(n0, n0 + ngb)}
            bias = bias_ref[j]
            ss = [_dot_nt(kts[n], qs[x]) + bias for x, (n, _) in enumerate(heads)]
            mid = []
            for x in range(len(heads)):
                m_i, l_i, _ = carry[x]
                m_new = jnp.maximum(m_i, jnp.max(ss[x], axis=0, keepdims=True))
                p = jnp.exp2((ss[x] - m_new) * c2)
                alpha = jnp.exp2((m_i - m_new) * c2)
                mid.append((m_new, alpha * l_i + jnp.sum(p, axis=0, keepdims=True), alpha, p.astype(BF16)))
            pvs = [_dot(vts[n], mid[x][3]) for x, (n, _) in enumerate(heads)]
            return tuple((mid[x][0], mid[x][1], mid[x][2] * carry[x][2] + pvs[x]) for x in range(len(heads)))

        one = (jnp.full((1, tq), NEG_BIG, F32), jnp.zeros((1, tq), F32), jnp.zeros((HEAD, tq), F32))
        fin = lax.fori_loop(0, ntile, att_tile, (one,) * len(heads))
        for x, (_, h) in enumerate(heads):
            o_ref[:, h * HEAD:(h + 1) * HEAD] = (fin[x][2] / fin[x][1]).T.astype(o_ref.dtype)


def dsa_prompt_attend(q_bf, qi_bf, wi_t, ki_bf, k_bf, vt_bf, *, bsz, t, nq, nkv, nidx):
    tq = DSA_TQ
    nqb = t // tq
    topk = min(TOPK_MAX, t // 4)
    assert topk <= tq and t % tq == 0
    qrow = lambda wdt: pl.BlockSpec((tq, wdt), lambda b, i: (b * nqb + i, 0))
    brow = lambda wdt: pl.BlockSpec((t, wdt), lambda b, i: (b, 0))
    kern = functools.partial(_dsa_prompt_kernel, tq=tq, nidx=nidx, nq=nq, nkv=nkv, topk=topk,
                             scale=float(HEAD ** -0.5))
    return pl.pallas_call(
        kern,
        grid=(bsz, nqb),
        in_specs=[qrow(nidx * HEAD), pl.BlockSpec((nidx, tq), lambda b, i: (0, b * nqb + i)), brow(HEAD),
                  qrow(nq * HEAD), brow(nkv * HEAD),
                  pl.BlockSpec((nqb, nkv * HEAD, tq), lambda b, i: (b, 0, 0))],
        out_specs=qrow(nq * HEAD),
        out_shape=jax.ShapeDtypeStruct((bsz * t, nq * HEAD), BF16),
        scratch_shapes=[pltpu.VMEM((nqb, tq, tq), jnp.int32), pltpu.VMEM((nqb, tq, tq), F32)],
        compiler_params=_cparams(("parallel", "arbitrary")),
        name="dsa_prompt",
    )(qi_bf, wi_t, ki_bf, q_bf, k_bf, vt_bf)


def _dsa_s_scores_kernel(pt_ref, qi_ref, wi_ref, *rest, ts, nidx, nstep, npp):
    page_refs, new_ref, sc_ref = rest[:npp], rest[npp], rest[npp + 1]
    p = pl.program_id(1)
    is_new = p == nstep
    psz = new_ref.shape[1]
    first = jnp.where(is_new, new_ref[0], page_refs[0][...])
    kcat = jnp.concatenate([first] + [page_refs[u][...] for u in range(1, npp)], axis=0).astype(BF16)
    d = _dot_nt(qi_ref[...], kcat)
    r = jnp.maximum(d, 0.0) * wi_ref[...]
    rows = [jnp.sum(r[tt * nidx:(tt + 1) * nidx, :], axis=0, keepdims=True) for tt in range(ts)]
    sc = jnp.concatenate(rows + [jnp.full((V7X_SUBLANES - ts, npp * psz), NEG_BIG, F32)], axis=0)
    rowi = lax.broadcasted_iota(jnp.int32, sc.shape, 0)
    coli = lax.broadcasted_iota(jnp.int32, sc.shape, 1)
    sc_ref[0] = jnp.where(is_new & (coli > rowi), NEG_BIG, sc)


def _dsa_s_select_kernel(sc_ref, keep_ref, *, topk, past):
    key = _sort_key(sc_ref[...])
    rows, ncol = key.shape

    def count_ge(trial):
        return jnp.sum(jnp.where(key >= trial, 1.0, 0.0), axis=1, keepdims=True)

    thr = _kth_largest_key(count_ge, rows, float(topk))
    above = key > thr
    room = float(topk) - jnp.sum(jnp.where(above, 1.0, 0.0), axis=1, keepdims=True)
    rowi = lax.broadcasted_iota(jnp.int32, key.shape, 0)
    coli = lax.broadcasted_iota(jnp.int32, key.shape, 1)
    visible = coli <= past + (rowi % V7X_SUBLANES)
    li = lax.broadcasted_iota(jnp.int32, (V7X_LANES, V7X_LANES), 0)
    lj = lax.broadcasted_iota(jnp.int32, (V7X_LANES, V7X_LANES), 1)
    upper = jnp.where(li <= lj, 1.0, 0.0).astype(BF16)
    seen = jnp.zeros((rows, 1), F32)
    for c in range(ncol // V7X_LANES):
        sl = slice(c * V7X_LANES, (c + 1) * V7X_LANES)
        tie = jnp.where(key[:, sl] == thr, 1.0, 0.0)
        rank = seen + _dot(tie.astype(BF16), upper)
        keep = (above[:, sl] | ((key[:, sl] == thr) & (rank <= room))) & visible[:, sl]
        keep_ref[:, sl] = jnp.where(keep, 1.0, 0.0)
        seen = seen + jnp.sum(tie, axis=1, keepdims=True)


def _dsa_s_attend_kernel(pt_ref, q_ref, keep_ref, expand_ref, struct_ref, *rest, nstep, npp, scale):
    kp_refs, vp_refs = rest[:npp], rest[npp:2 * npp]
    kn_ref, vn_ref, o_ref, m_ref, l_ref, acc_ref = rest[2 * npp:]
    p = pl.program_id(1)
    is_new = p == nstep
    psz = kn_ref.shape[1]

    @pl.when(p == 0)
    def _():
        m_ref[...] = jnp.full_like(m_ref, NEG_BIG)
        l_ref[...] = jnp.zeros_like(l_ref)
        acc_ref[...] = jnp.zeros_like(acc_ref)

    def flat(refs, new_ref, u):
        page = refs[u][...]
        if u == 0:
            page = jnp.where(is_new, new_ref[0], page)
        return page.reshape(-1, HEAD).astype(BF16)

    c2 = scale * math.log2(math.e)
    q = q_ref[0]
    struct = struct_ref[...]
    ss = []
    for u in range(npp):
        seen = _dot(keep_ref[0, :, u * psz:(u + 1) * psz], expand_ref[...])
        ss.append(_dot_nt(q, flat(kp_refs, kn_ref, u)) + jnp.where(seen * struct > 0.5, 0.0, NEG_BIG))
    m_i = m_ref[...]
    m_new = m_i
    for u in range(npp):
        m_new = jnp.maximum(m_new, jnp.max(ss[u], axis=1, keepdims=True))
    alpha = jnp.exp2((m_i - m_new) * c2)
    l_new = alpha * l_ref[...]
    acc = alpha * acc_ref[...]
    for u in range(npp):
        pr = jnp.exp2((ss[u] - m_new) * c2)
        l_new = l_new + jnp.sum(pr, axis=1, keepdims=True)
        acc = acc + _dot(pr.astype(BF16), flat(vp_refs, vn_ref, u))
    m_ref[...], l_ref[...], acc_ref[...] = m_new, l_new, acc

    @pl.when(is_new)
    def _():
        o_ref[0] = acc / l_new


def dsa_sample_attend(q_bf, qi_bf, wi, ki_new, k_new, v_new, ck, cv, cki, page_table,
                      *, layer_j, bsz, ts, nq, nkv, nidx):
    psz = ck.shape[2]
    npages = page_table.shape[1]
    npp = DSA_PAGES_PER_STEP
    assert npages % npp == 0
    nstep = npages // npp
    past = npages * psz
    topk = min(TOPK_MAX, (past + ts) // 4)
    grp = nq // nkv
    ncol = (nstep + 1) * npp * psz
    sub = V7X_SUBLANES
    padrows = lambda x, tail: jnp.pad(x.reshape((bsz, ts) + tail), ((0, 0), (0, psz - ts)) + ((0, 0),) * len(tail))

    def page_idx(u, ndim):
        def index(b, p, pt):
            return (layer_j, pt[b, jnp.minimum(p * npp + u, npages - 1)]) + (0,) * ndim
        return index

    scores = pl.pallas_call(
        functools.partial(_dsa_s_scores_kernel, ts=ts, nidx=nidx, nstep=nstep, npp=npp),
        grid_spec=pltpu.PrefetchScalarGridSpec(
            num_scalar_prefetch=1,
            grid=(bsz, nstep + 1),
            in_specs=[pl.BlockSpec((ts * nidx, HEAD), lambda b, p, pt: (b, 0)),
                      pl.BlockSpec((ts * nidx, 1), lambda b, p, pt: (b, 0))]
                     + [pl.BlockSpec((None, None, psz, HEAD), page_idx(u, 2)) for u in range(npp)]
                     + [pl.BlockSpec((1, psz, HEAD), lambda b, p, pt: (b, 0, 0))],
            out_specs=pl.BlockSpec((1, sub, npp * psz), lambda b, p, pt: (b, 0, p))),
        out_shape=jax.ShapeDtypeStruct((bsz, sub, ncol), F32),
        compiler_params=_cparams(("parallel", "arbitrary")),
        name="dsa_s_scores",
    )(page_table, qi_bf.reshape(bsz * ts * nidx, HEAD), wi.reshape(bsz * ts * nidx, 1),
      *([cki] * npp), padrows(ki_new, (HEAD,)))

    keep = pl.pallas_call(
        functools.partial(_dsa_s_select_kernel, topk=topk, past=past),
        out_shape=jax.ShapeDtypeStruct((bsz * sub, ncol), F32),
        compiler_params=pltpu.CompilerParams(vmem_limit_bytes=VMEM_LIMIT),
        name="dsa_s_select",
    )(scores.reshape(bsz * sub, ncol))
    nrow = nkv * ts * grp
    keep = jnp.broadcast_to(keep.reshape(bsz, 1, sub, 1, ncol)[:, :, :ts], (bsz, nkv, ts, grp, ncol))
    keep = keep.reshape(bsz, nrow, ncol).astype(BF16)
    q_all = q_bf.reshape(bsz, ts, nkv, grp, HEAD).transpose(0, 2, 1, 3, 4).reshape(bsz, nrow, HEAD)
    flat_col = np.arange(psz * nkv)
    expand = jnp.asarray(np.arange(psz)[:, None] == flat_col[None, :] // nkv, BF16)
    struct = jnp.asarray(np.arange(nrow)[:, None] // (ts * grp) == flat_col[None, :] % nkv, F32)
    kv_new_spec = pl.BlockSpec((1, psz, nkv, HEAD), lambda b, p, pt: (b, 0, 0, 0))
    const = lambda shape: pl.BlockSpec(shape, lambda b, p, pt: (0, 0))
    o = pl.pallas_call(
        functools.partial(_dsa_s_attend_kernel, nstep=nstep, npp=npp, scale=float(HEAD ** -0.5)),
        grid_spec=pltpu.PrefetchScalarGridSpec(
            num_scalar_prefetch=1,
            grid=(bsz, nstep + 1),
            in_specs=[pl.BlockSpec((1, nrow, HEAD), lambda b, p, pt: (b, 0, 0)),
                      pl.BlockSpec((1, nrow, npp * psz), lambda b, p, pt: (b, 0, p)),
                      const(expand.shape), const(struct.shape)]
                     + [pl.BlockSpec((None, None, psz, nkv, HEAD), page_idx(u, 3)) for u in range(npp)] * 2
                     + [kv_new_spec, kv_new_spec],
            out_specs=pl.BlockSpec((1, nrow, HEAD), lambda b, p, pt: (b, 0, 0)),
            scratch_shapes=[pltpu.VMEM((nrow, 1), F32), pltpu.VMEM((nrow, 1), F32),
                            pltpu.VMEM((nrow, HEAD), F32)]),
        out_shape=jax.ShapeDtypeStruct((bsz, nrow, HEAD), F32),
        compiler_params=_cparams(("parallel", "arbitrary")),
        name="dsa_s_attend",
    )(page_table, q_all, keep, expand, struct, *([ck] * npp), *([cv] * npp),
      padrows(k_new, (nkv, HEAD)), padrows(v_new, (nkv, HEAD)))
    o = o.reshape(bsz, nkv, ts, grp, HEAD).transpose(0, 2, 1, 3, 4).reshape(bsz * ts, nq * HEAD)
    return o.astype(BF16)


def kernel(x_prompt, x_sample, state_hgrn, state_rglru_h, state_rglru_conv, cache_k, cache_v, cache_kidx,
           page_table, p_prompt, p_sample, norm_mix, norm_mlp, norm_pe, w_in_ab, w_out_ab, hgrn_lb_logits,
           hgrn_out_norm, rg_conv_w, rg_conv_b, rg_wa, rg_ba, rg_wx, rg_bx, rg_lambda, w_in_c, w_out_c,
           c_q_norm, c_k_norm, idx_k_ln_g, idx_k_ln_b, w_up, w_down, w_pe, w_pg):
    bp, tp, d = x_prompt.shape
    bs, ts, _ = x_sample.shape
    depth = norm_mix.shape[0]
    a_width = hgrn_out_norm.shape[1]
    b_width = rg_lambda.shape[1]
    nh_a = a_width // HEAD
    nkv = cache_k.shape[3]
    nq = w_out_c.shape[1] // HEAD
    nidx = (w_in_c.shape[2] - nq * HEAD - 2 * nkv * HEAD - HEAD) // (HEAD + 1)
    past = page_table.shape[1] * cache_k.shape[2]
    c_main = nq * HEAD + 2 * nkv * HEAD + nidx * HEAD
    pe_dim = p_prompt.shape[-1]

    hp = x_prompt.reshape(bp * tp, d)
    hs = x_sample.reshape(bs * ts, d)
    tab_p = _rope_tables(jnp.arange(tp, dtype=jnp.int32))
    tab_s = _rope_tables(past + jnp.arange(ts, dtype=jnp.int32))
    tab_s = tuple(jnp.tile(tb, (bs, 1)) for tb in tab_s)
    ts_pad = SAMPLE_CHUNK_ROWS
    assert ts <= ts_pad
    pad_s = lambda x: jnp.pad(x.reshape(bs, ts, -1), ((0, 0), (0, ts_pad - ts), (0, 0))).reshape(bs * ts_pad, -1)
    unpad_s = lambda x: x.reshape(bs, ts_pad, -1)[:, :ts].reshape(bs * ts, -1)

    outs = {k: [] for k in ("hg_p", "rh_p", "rc_p", "k_p", "v_p", "ki_p",
                            "hg_s", "rh_s", "rc_s", "k_s", "v_s", "ki_s")}
    wd_bf = w_down.astype(BF16)
    w_in_c_t = w_in_c.transpose(0, 2, 1)
    pp_all = p_prompt.reshape(depth, -1, pe_dim).astype(BF16)
    ps_all = p_sample.reshape(depth, -1, pe_dim).astype(BF16)
    for layer in range(depth):
        j = layer // 2
        xpn = rmsnorm_bf16(hp, norm_mix[layer])
        xsn = rmsnorm_bf16(hs, norm_mix[layer])
        if layer % 2 == 0:
            gb_blk = 4 * a_width // b_width
            x_blk = gb_blk + 1
            rg_args = (rg_conv_w[j], rg_conv_b[j], rg_wa[j], rg_ba[j], rg_wx[j], rg_bx[j], rg_lambda[j])
            n_in = w_in_ab.shape[2]

            proj, w_in = matmul(xpn, w_in_ab, layer=j, emit_bf16=True)
            oa, sa = hgrn(proj, hgrn_lb_logits, hgrn_out_norm[j], jnp.zeros((bp, nh_a, HEAD, HEAD), F32),
                          bsz=bp, t=tp, layer_j=j, r=min(HGRN_CHUNK_ROWS, tp))
            ob, hl, cn = rglru(proj, jnp.zeros((bp, CONV_W - 1, b_width), F32), jnp.zeros((bp, b_width), F32),
                               *rg_args, bsz=bp, t=tp, r=min(RGLRU_BLOCK_ROWS, tp), gb_blk=gb_blk, x_blk=x_blk,
                               first_pos_zero=True)
            hp, w_out = matmul(jnp.concatenate([oa, ob], axis=1), w_out_ab, layer=j, emit_bf16=True,
                               epilogue="residual", res=hp)
            outs["hg_p"].append(sa)
            outs["rh_p"].append(hl.reshape(bp, b_width))
            outs["rc_p"].append(cn)

            proj = pad_s(matmul(xsn, w_in, n_out=n_in))
            oa, sa = hgrn(proj, hgrn_lb_logits, hgrn_out_norm[j], state_hgrn[j],
                          bsz=bs, t=ts_pad, layer_j=j, r=ts_pad, t_valid=ts)
            ob, hl, cn = rglru(proj, state_rglru_conv[j], state_rglru_h[j], *rg_args, bsz=bs, t=ts_pad,
                               r=ts_pad, gb_blk=gb_blk, x_blk=x_blk, first_pos_zero=False, t_valid=ts)
            mix = unpad_s(jnp.concatenate([oa, ob], axis=1))
            hs = matmul(mix, w_out, n_out=d, epilogue="residual", res=hs)
            outs["hg_s"].append(sa)
            outs["rh_s"].append(hl.reshape(bs, b_width))
            outs["rc_s"].append(cn)
        else:
            norms = (c_q_norm[j], c_k_norm[j], idx_k_ln_g[j], idx_k_ln_b[j])
            pm, w_main = matmul(xpn, w_in_c_t, layer=j, w_t=True, n_out=c_main, emit_bf16=True)
            ptl, w_tail = matmul(xpn, w_in_c_t, layer=j, w_t=True, n_out=2 * HEAD, col0=c_main, emit_bf16=True)
            post = dsa_post(pm, ptl, tab_p, *norms, nq=nq, nkv=nkv, nidx=nidx, t_tab=tp)
            q_bf, qi_bf, kk, vv, ki, wi, k_bf, vt_bf, ki_bf = post
            o = dsa_prompt_attend(q_bf, qi_bf, wi.T, ki_bf, k_bf, vt_bf, bsz=bp, t=tp, nq=nq, nkv=nkv, nidx=nidx)
            hp, w_out = matmul(o, w_out_c, layer=j, emit_bf16=True, epilogue="residual", res=hp)
            outs["k_p"].append(kk.reshape(bp, tp, nkv, HEAD))
            outs["v_p"].append(vv.reshape(bp, tp, nkv, HEAD))
            outs["ki_p"].append(ki.reshape(bp, tp, HEAD))

            post = dsa_post(matmul(xsn, w_main, w_t=True, n_out=c_main), matmul(xsn, w_tail, w_t=True, n_out=2 * HEAD),
                            tab_s, *norms, nq=nq, nkv=nkv, nidx=nidx, t_tab=bs * ts)
            q_bf, qi_bf, kk, vv, ki, wi, _, _, _ = post
            o = dsa_sample_attend(q_bf, qi_bf, wi, ki, kk, vv, cache_k, cache_v, cache_kidx,
                                  page_table, layer_j=j, bsz=bs, ts=ts, nq=nq, nkv=nkv, nidx=nidx)
            hs = matmul(o, w_out, n_out=d, epilogue="residual", res=hs)
            outs["k_s"].append(kk.reshape(bs, ts, nkv, HEAD))
            outs["v_s"].append(vv.reshape(bs, ts, nkv, HEAD))
            outs["ki_s"].append(ki.reshape(bs, ts, HEAD))

        d_ff = w_up.shape[2]
        mid, wu = matmul(rmsnorm_bf16(hp, norm_mlp[layer]), w_up, layer=layer, emit_bf16=True,
                         epilogue="relu2", out_dtype=BF16)
        hp = matmul(mid, wd_bf, layer=layer, epilogue="residual", res=hp)
        hp, wpg = matmul(rmsnorm_bf16(hp, norm_pe[layer]), w_pg, layer=layer, emit_bf16=True, epilogue="pe_gate",
                         res=hp, p=pp_all[layer], w_pe=w_pe, pe_layer=layer)
        mid = matmul(rmsnorm_bf16(hs, norm_mlp[layer]), wu, n_out=d_ff, epilogue="relu2", out_dtype=BF16)
        hs = matmul(mid, wd_bf, layer=layer, epilogue="residual", res=hs)
        hs = matmul(rmsnorm_bf16(hs, norm_pe[layer]), wpg, n_out=d, epilogue="pe_gate", res=hs,
                    p=ps_all[layer], w_pe=w_pe, pe_layer=layer)

    st = lambda name: jnp.stack(outs[name])
    return (hp.reshape(bp, tp, d), hs.reshape(bs, ts, d),
            st("hg_p"), st("rh_p"), st("rc_p"), st("k_p"), st("v_p"), st("ki_p"),
            st("hg_s"), st("rh_s"), st("rc_s"), st("k_s"), st("v_s"), st("ki_s"))
```

```python
import functools
import math

import jax
import jax.numpy as jnp
import numpy as np
from jax import lax
from jax.experimental import pallas as pl
from jax.experimental.pallas import tpu as pltpu

F32 = jnp.float32
BF16 = jnp.bfloat16

V7X_LANES = 128
V7X_SUBLANES = 8
V7X_VMEM_BYTES = 64 * 1024 * 1024
VMEM_LIMIT = V7X_VMEM_BYTES - 8 * 1024 * 1024

EPS = 1e-6
NEG_BIG = -1e30
RG_C = 8.0
ROPE_THETA = 500000.0
CONV_W = 4
TOPK_MAX = 256

HEAD = 128
ROPE_HALF = HEAD // 8
DSA_KV_GROUPS_PER_BODY = 8
HGRN_CHUNK_ROWS = 128
RGLRU_BLOCK_ROWS = 256
SAMPLE_CHUNK_ROWS = 16
DSA_PAGES_PER_STEP = 8
DSA_TQ = 256
HGRN_HEADS_PER_STEP = 16

_NT = (((1,), (1,)), ((), ()))
_TN = (((0,), (0,)), ((), ()))


def _cparams(sem):
    return pltpu.CompilerParams(dimension_semantics=sem, vmem_limit_bytes=VMEM_LIMIT)


def _dot(a, b):
    return jnp.dot(a, b, preferred_element_type=F32)


def _dot_nt(a, b):
    return lax.dot_general(a, b, _NT, preferred_element_type=F32)


def _dot_tn(a, b):
    return lax.dot_general(a, b, _TN, preferred_element_type=F32)


def _rmsnorm_kernel(x_ref, g_ref, o_ref):
    x = x_ref[...]
    y = x * lax.rsqrt(jnp.mean(x * x, axis=-1, keepdims=True) + EPS)
    o_ref[...] = (y * g_ref[...]).astype(o_ref.dtype)


def rmsnorm_bf16(x, g):
    m, d = x.shape
    tm = min(m, 512)
    return pl.pallas_call(
        _rmsnorm_kernel,
        grid=(m // tm,),
        in_specs=[pl.BlockSpec((tm, d), lambda i: (i, 0)),
                  pl.BlockSpec((1, d), lambda i: (0, 0))],
        out_specs=pl.BlockSpec((tm, d), lambda i: (i, 0)),
        out_shape=jax.ShapeDtypeStruct((m, d), BF16),
        compiler_params=_cparams(("parallel",)),
        name="rmsnorm",
    )(x, g.reshape(1, d))


def _mm_kernel(*refs, nk, epilogue, w_t, emit):
    x_ref, w_ref = refs[0], refs[1]
    pos = 2
    res_ref = p_ref = wpe_ref = None
    if epilogue in ("residual", "pe_gate"):
        res_ref = refs[pos]
        pos += 1
    if epilogue == "pe_gate":
        p_ref, wpe_ref = refs[pos], refs[pos + 1]
        pos += 2
    o_ref = refs[pos]
    pos += 1
    wb_ref = None
    if emit:
        wb_ref = refs[pos]
        pos += 1
    acc_ref = refs[pos] if nk > 1 else None

    def finish(acc):
        if epilogue == "none":
            out = acc
        elif epilogue == "relu2":
            r = jnp.maximum(acc, 0.0)
            out = r * r
        elif epilogue == "residual":
            out = res_ref[...] + acc
        else:
            pe = _dot(p_ref[...], wpe_ref[...].astype(BF16))
            out = res_ref[...] + pe * jax.nn.sigmoid(acc)
        o_ref[...] = out.astype(o_ref.dtype)

    wb = w_ref[...].astype(BF16)
    if emit:
        wb_ref[...] = wb
    prod = _dot_nt(x_ref[...], wb) if w_t else _dot(x_ref[...], wb)
    if nk == 1:
        finish(prod)
    else:
        k = pl.program_id(2)

        @pl.when(k == 0)
        def _():
            acc_ref[...] = jnp.zeros_like(acc_ref)

        acc_ref[...] += prod

        @pl.when(k == nk - 1)
        def _():
            finish(acc_ref[...])


MIB = 1024 * 1024
MM_TK = 4096
MM_SMALL_M = 64
MM_SMALL_M_TN = {True: 512, False: 2048}
MM_TILE_F32W = (2048, 256)
MM_TILE_BF16W = (1024, 512)
X_SINGLE_BUFFER_BYTES = 16 * MIB


def _mm_tiles(m, n, k, w_f32):
    tk = min(k, MM_TK)
    if m <= MM_SMALL_M:
        return m, min(n, MM_SMALL_M_TN[w_f32]), tk
    tm, tn = MM_TILE_F32W if (w_f32 and k == tk) else MM_TILE_BF16W
    return min(m, tm), min(n, tn), tk


def matmul(x, w, *, layer=None, n_out=None, col0=0, w_t=False, emit_bf16=False, epilogue="none", res=None,
           p=None, w_pe=None, pe_layer=None, out_dtype=F32):
    m, k = x.shape
    n = (w.shape[-2] if w_t else w.shape[-1]) if n_out is None else n_out
    tm, tn, tk = _mm_tiles(m, n, k, w.dtype == F32)
    while n % tn:
        tn //= 2
    assert m % tm == 0 and k % tk == 0 and col0 % tn == 0
    nk = k // tk
    nj = n // tn
    jb = col0 // tn
    lead = () if layer is None else (layer,)
    pe_lead = () if pe_layer is None else (pe_layer,)
    none = lambda t: (None,) * len(t)
    wtile = (tn, tk) if w_t else (tk, tn)
    worder = (lambda kk, jj: (jj, kk)) if w_t else (lambda kk, jj: (kk, jj))
    if nk == 1:
        grid = (m // tm, nj)
        xmap = lambda i, j: (i, 0)
        wmap = lambda i, j: lead + worder(0, jb + j)
        omap = lambda i, j: (i, j)
        pmap = lambda i, j: (i, 0)
        pemap = lambda i, j: pe_lead + (0, j)
        emap = lambda i, j: worder(0, jnp.where(i == 0, j, nj))
        sem = ("arbitrary", "arbitrary") if emit_bf16 else ("parallel", "parallel")
    else:
        assert not emit_bf16
        grid = (m // tm, nj, nk)
        xmap = lambda i, j, kk: (i, kk)
        wmap = lambda i, j, kk: lead + worder(kk, jb + j)
        omap = lambda i, j, kk: (i, j)
        pmap = lambda i, j, kk: (i, 0)
        pemap = lambda i, j, kk: pe_lead + (0, j)
        sem = ("parallel", "parallel", "arbitrary")
    xspec = (pl.BlockSpec((tm, tk), xmap, pipeline_mode=pl.Buffered(1))
             if tm * tk * x.dtype.itemsize >= X_SINGLE_BUFFER_BYTES else pl.BlockSpec((tm, tk), xmap))
    in_specs = [xspec, pl.BlockSpec(none(lead) + wtile, wmap)]
    args = [x, w]
    if epilogue in ("residual", "pe_gate"):
        in_specs.append(pl.BlockSpec((tm, tn), omap))
        args.append(res)
    if epilogue == "pe_gate":
        pe_dim = p.shape[1]
        in_specs.append(pl.BlockSpec((tm, pe_dim), pmap))
        in_specs.append(pl.BlockSpec(none(pe_lead) + (pe_dim, tn), pemap))
        args += [p, w_pe]
    out_specs = [pl.BlockSpec((tm, tn), omap)]
    out_shape = [jax.ShapeDtypeStruct((m, n), out_dtype)]
    if emit_bf16:
        out_specs.append(pl.BlockSpec(wtile, emap))
        out_shape.append(jax.ShapeDtypeStruct((n + tn, k) if w_t else (k, n + tn), BF16))
    scratch = [pltpu.VMEM((tm, tn), F32)] if nk > 1 else []
    outs = pl.pallas_call(
        functools.partial(_mm_kernel, nk=nk, epilogue=epilogue, w_t=w_t, emit=emit_bf16),
        grid=grid,
        in_specs=in_specs,
        out_specs=out_specs,
        out_shape=out_shape,
        scratch_shapes=scratch,
        compiler_params=_cparams(sem),
        name="mm_" + epilogue,
    )(*args)
    return outs if emit_bf16 else outs[0]


def _hgrn_level_matrices(r):
    idx = np.arange(r)
    t = idx[:, None]
    u = idx[None, :]
    mats = []
    lvl = 2
    while lvl <= r:
        h = lvl // 2
        pos = t % lvl
        mid = t - pos + h
        upper = pos >= h
        m_up = (u >= mid) & (u <= t)
        m_lo = (u > t) & (u <= mid - 1)
        mats.append(np.where(upper, m_up, m_lo))
        lvl *= 2
    mats.append(u <= t)
    mats.append(u > t)
    return np.concatenate(mats, axis=0).astype(np.float32)


def _hgrn_kernel(q_ref, f_ref, i_ref, ga_ref, lbl_ref, an_ref, s0_ref, mall_ref,
                 o_ref, s_ref, st_ref, *, r, nhb, layer_j, t_valid):
    c = pl.program_id(2)
    nc = pl.num_programs(2)
    nlev = int(math.log2(r))

    @pl.when(c == 0)
    def _():
        for hh in range(nhb):
            st_ref[hh] = s0_ref[0, hh].T

    lbl = lbl_ref[...]
    e = jnp.exp(lbl - jnp.max(lbl, axis=0, keepdims=True))
    soft = e / jnp.sum(e, axis=0, keepdims=True)
    lb_all = jnp.zeros((1, nhb * HEAD), F32)
    for jj in range(1, layer_j + 1):
        lb_all = lb_all + soft[jj:jj + 1, :]

    z = f_ref[...]
    logf = jnp.log1p(lb_all * jnp.exp(-z)) - _softplus(-z)
    kk_all = (1.0 - lb_all) * jax.nn.sigmoid(-z)
    qq_all = jax.nn.silu(q_ref[...])
    vv_all = i_ref[...]
    if t_valid is not None:
        live = lax.broadcasted_iota(jnp.int32, (r, 1), 0) < t_valid
        logf = jnp.where(live, logf, 0.0)
        kk_all = jnp.where(live, kk_all, 0.0)
        qq_all = jnp.where(live, qq_all, 0.0)
        vv_all = jnp.where(live, vv_all, 0.0)

    hi = logf.astype(BF16)
    mid = (logf - hi.astype(F32)).astype(BF16)
    xs_all = _dot(mall_ref[...], hi) + _dot(mall_ref[...], mid)

    ti = lax.broadcasted_iota(jnp.int32, (r, r), 0)
    si = lax.broadcasted_iota(jnp.int32, (r, r), 1)
    diag = ti == si
    rowi = lax.broadcasted_iota(jnp.int32, (r, 1), 0)
    ups, pairs = [], []
    for lv in range(nlev):
        blk = 2 << lv
        h = blk // 2
        ups.append((rowi & (blk - 1)) >= h)
        pairs.append(((ti >> (lv + 1)) == (si >> (lv + 1))) & ((ti & (blk - 1)) >= h) & ((si & (blk - 1)) < h))

    ga = ga_ref[...]
    an = an_ref[...]
    sls = [slice(hh * HEAD, (hh + 1) * HEAD) for hh in range(nhb)]
    xc0, xe0 = nlev * r, (nlev + 1) * r
    sts = [st_ref[hh] for hh in range(nhb)]
    vbs = [vv_all[:, sl].astype(BF16) for sl in sls]
    o_st = [_dot_nt((qq_all[:, sl] * jnp.exp(xs_all[xc0:xc0 + r, sl])).astype(BF16), sts[hh].astype(BF16))
            for hh, sl in enumerate(sls)]
    upd = [_dot_tn(vbs[hh], (kk_all[:, sl] * jnp.exp(xs_all[xe0:xe0 + r, sl])).astype(BF16))
           for hh, sl in enumerate(sls)]
    atts = []
    for hh, sl in enumerate(sls):
        qq, kk = qq_all[:, sl], kk_all[:, sl]
        att = jnp.where(diag, _dot_nt(qq.astype(BF16), kk.astype(BF16)), 0.0)
        for lv in range(nlev):
            w = jnp.exp(xs_all[lv * r:(lv + 1) * r, sl])
            qt = jnp.where(ups[lv], qq * w, 0.0).astype(BF16)
            kt = jnp.where(ups[lv], 0.0, kk * w).astype(BF16)
            att = att + jnp.where(pairs[lv], _dot_nt(qt, kt), 0.0)
        atts.append(att.astype(BF16))
    for hh, sl in enumerate(sls):
        o = _dot(atts[hh], vbs[hh]) + o_st[hh]
        g_end = jnp.exp(xs_all[xc0 + r - 1:xc0 + r, sl])
        st_ref[hh] = g_end * sts[hh] + upd[hh]
        on = o * lax.rsqrt(jnp.mean(o * o, axis=-1, keepdims=True) + EPS) * an[:, sl]
        o_ref[:, sl] = (on * jax.nn.silu(ga[:, sl])).astype(o_ref.dtype)

    @pl.when(c == nc - 1)
    def _():
        for hh in range(nhb):
            s_ref[0, hh] = st_ref[hh].T


def hgrn(proj, lb_logits, a_norm, s0, *, bsz, t, layer_j, r, out_width, t_valid=None):
    width = a_norm.shape[0]
    nh = width // HEAD
    nhb = HGRN_HEADS_PER_STEP
    ng = nh // nhb
    n_ab = lb_logits.shape[0]
    nchunk = t // r
    mall = jnp.asarray(_hgrn_level_matrices(r), BF16)
    bw = nhb * HEAD
    col = lambda off: (lambda b, h, c: (b * nchunk + c, off * ng + h))
    kern = functools.partial(_hgrn_kernel, r=r, nhb=nhb, layer_j=layer_j, t_valid=t_valid)
    return pl.pallas_call(
        kern,
        grid=(bsz, ng, nchunk),
        in_specs=[pl.BlockSpec((r, bw), col(0)),
                  pl.BlockSpec((r, bw), col(1)),
                  pl.BlockSpec((r, bw), col(2)),
                  pl.BlockSpec((r, bw), col(3)),
                  pl.BlockSpec((n_ab, bw), lambda b, h, c: (0, h)),
                  pl.BlockSpec((1, bw), lambda b, h, c: (0, h)),
                  pl.BlockSpec((1, nhb, HEAD, HEAD), lambda b, h, c: (b, h, 0, 0)),
                  pl.BlockSpec(mall.shape, lambda b, h, c: (0, 0))],
        out_specs=[pl.BlockSpec((r, bw), lambda b, h, c: (b * nchunk + c, h)),
                   pl.BlockSpec((1, nhb, HEAD, HEAD), lambda b, h, c: (b, h, 0, 0))],
        out_shape=[jax.ShapeDtypeStruct((bsz * t, out_width), BF16),
                   jax.ShapeDtypeStruct((bsz, nh, HEAD, HEAD), F32)],
        scratch_shapes=[pltpu.VMEM((nhb, HEAD, HEAD), F32)],
        compiler_params=_cparams(("parallel", "parallel", "arbitrary")),
        name="hgrn",
    )(proj, proj, proj, proj, lb_logits, a_norm.reshape(1, width), s0, mall)


def _softplus(x):
    return jnp.maximum(x, 0.0) + jnp.log1p(jnp.exp(-jnp.abs(x)))


def _rglru_kernel(x_ref, gb_ref, c0_ref, h0_ref, cw_ref, cb_ref, wa_ref, ba_ref, wx_ref, bx_ref,
                  lam_ref, mix_ref, o_ref, hl_ref, cn_ref, xp_ref, a_ref, b_ref, hc_ref,
                  *, r, nblk, first_pos_zero, t_valid):
    del mix_ref
    i = pl.program_id(1)
    ni = pl.num_programs(1)
    pad = V7X_SUBLANES

    @pl.when(i == 0)
    def _():
        xp_ref[0:pad, :] = c0_ref[0]
        hc_ref[...] = h0_ref[0]

    x = x_ref[...]
    xp_ref[pad:pad + r, :] = x
    cw = cw_ref[...]
    y = cw[0:1, :] * xp_ref[pad - 3:pad - 3 + r, :]
    y = y + cw[1:2, :] * xp_ref[pad - 2:pad - 2 + r, :]
    y = y + cw[2:3, :] * xp_ref[pad - 1:pad - 1 + r, :]
    y = cb_ref[...] + (y + cw[3:4, :] * x)

    last = r if t_valid is None else t_valid
    cn_ref[0] = xp_ref[pad + last - 3:pad + last, :]
    xp_ref[0:pad, :] = xp_ref[r:r + pad, :]

    ra = []
    rx = []
    for n in range(nblk):
        ys = y[:, n * HEAD:(n + 1) * HEAD].astype(BF16)
        ra.append(_dot(ys, wa_ref[n]))
        rx.append(_dot(ys, wx_ref[n]))
    rg = jax.nn.sigmoid(jnp.concatenate(ra, axis=1) + ba_ref[...])
    gi = jax.nn.sigmoid(jnp.concatenate(rx, axis=1) + bx_ref[...])
    log_a = (-RG_C * rg) * _softplus(-lam_ref[...])
    a = jnp.exp(log_a)
    th = jnp.tanh(log_a)
    mult = jnp.sqrt((-2.0 * th) / (1.0 - th))
    rowi = lax.broadcasted_iota(jnp.int32, (r, 1), 0)
    if first_pos_zero:
        mult = jnp.where((rowi == 0) & (i == 0), 1.0, mult)
    bterm = mult * gi * y
    if t_valid is not None:
        live = rowi < t_valid
        a = jnp.where(live, a, 1.0)
        bterm = jnp.where(live, bterm, 0.0)
    a_ref[...] = a
    b_ref[...] = bterm

    def step(tt, h):
        h = a_ref[pl.ds(tt, 1), :] * h + b_ref[pl.ds(tt, 1), :]
        b_ref[pl.ds(tt, 1), :] = h
        return h

    h_last = lax.fori_loop(0, r, step, hc_ref[...], unroll=8)
    hc_ref[...] = h_last
    o_ref[...] = (jax.nn.gelu(gb_ref[...]) * b_ref[...]).astype(o_ref.dtype)

    @pl.when(i == ni - 1)
    def _():
        hl_ref[0] = h_last


def rglru(proj, mix, conv0, h0, conv_w, conv_b, wa, ba, wx, bx, lam, *, bsz, t, r, gb_blk, x_blk, out_blk,
          first_pos_zero, t_valid=None):
    w = lam.shape[0]
    nblk = wa.shape[0]
    nrb = t // r
    pad = V7X_SUBLANES
    c0 = jnp.pad(conv0, ((0, 0), (pad - (CONV_W - 1), 0), (0, 0)))
    row = lambda v: v.reshape(1, w)
    kern = functools.partial(_rglru_kernel, r=r, nblk=nblk, first_pos_zero=first_pos_zero, t_valid=t_valid)
    full = lambda shape: pl.BlockSpec(shape, lambda b, i: (0,) * len(shape))
    return pl.pallas_call(
        kern,
        grid=(bsz, nrb),
        in_specs=[pl.BlockSpec((r, w), lambda b, i: (b * nrb + i, x_blk)),
                  pl.BlockSpec((r, w), lambda b, i: (b * nrb + i, gb_blk)),
                  pl.BlockSpec((1, pad, w), lambda b, i: (b, 0, 0)),
                  pl.BlockSpec((1, 1, w), lambda b, i: (b, 0, 0)),
                  full((CONV_W, w)), full((1, w)),
                  full(wa.shape), full((1, w)), full(wx.shape), full((1, w)), full((1, w)),
                  pl.BlockSpec(memory_space=pl.ANY)],
        out_specs=[pl.BlockSpec((r, w), lambda b, i: (b * nrb + i, out_blk)),
                   pl.BlockSpec((1, 1, w), lambda b, i: (b, 0, 0)),
                   pl.BlockSpec((1, CONV_W - 1, w), lambda b, i: (b, 0, 0))],
        input_output_aliases={11: 0},
        out_shape=[jax.ShapeDtypeStruct(mix.shape, mix.dtype),
                   jax.ShapeDtypeStruct((bsz, 1, w), F32),
                   jax.ShapeDtypeStruct((bsz, CONV_W - 1, w), F32)],
        scratch_shapes=[pltpu.VMEM((r + pad, w), F32), pltpu.VMEM((r, w), F32),
                        pltpu.VMEM((r, w), F32), pltpu.VMEM((1, w), F32)],
        compiler_params=_cparams(("parallel", "arbitrary")),
        name="rglru",
    )(proj, proj, c0, h0.reshape(bsz, 1, w), conv_w, row(conv_b), wa.astype(BF16), row(ba),
      wx.astype(BF16), row(bx), row(lam), mix)


def _rope_tables(pos):
    rd = HEAD // 4
    half = rd // 2
    inv = jnp.exp(-math.log(ROPE_THETA) * jnp.arange(half, dtype=F32) * (2.0 / rd))
    ang = pos.astype(F32)[:, None] * inv[None, :]
    cos, sin = jnp.cos(ang), jnp.sin(ang)
    n = pos.shape[0]
    ones = jnp.ones((n, HEAD - rd), F32)
    zeros = jnp.zeros((n, HEAD - rd), F32)
    zh = jnp.zeros((n, half), F32)
    c = jnp.concatenate([cos, cos, ones], axis=1)
    a = jnp.concatenate([-sin, zh, zeros], axis=1)
    b = jnp.concatenate([zh, sin, zeros], axis=1)
    return c, a, b


def _rope(x, c, a, b):
    return x * c + pltpu.roll(x, HEAD - ROPE_HALF, 1) * a + pltpu.roll(x, ROPE_HALF, 1) * b


def _dsa_post_kernel(pm_ref, pt_ref, c_ref, a_ref, b_ref, qn_ref, kn_ref, kg_ref, kb_ref,
                     q_ref, qi_ref, k_ref, v_ref, ki_ref, wi_ref, kbf_ref, vbf_ref, kibf_ref,
                     *, nq, nkv, nidx, wi_scale):
    c, a, b = c_ref[...], a_ref[...], b_ref[...]
    qn, kn = qn_ref[...], kn_ref[...]
    off = 0
    for h in range(nq):
        x = pm_ref[:, off + h * HEAD:off + (h + 1) * HEAD]
        y = x * lax.rsqrt(jnp.mean(x * x, axis=-1, keepdims=True) + EPS) * qn
        q_ref[:, h * HEAD:(h + 1) * HEAD] = _rope(y, c, a, b).astype(q_ref.dtype)
    off += nq * HEAD
    for h in range(nkv):
        x = pm_ref[:, off + h * HEAD:off + (h + 1) * HEAD]
        y = x * lax.rsqrt(jnp.mean(x * x, axis=-1, keepdims=True) + EPS) * kn
        kr = _rope(y, c, a, b)
        k_ref[:, h, :] = kr
        kbf_ref[:, h * HEAD:(h + 1) * HEAD] = kr.astype(BF16)
    off += nkv * HEAD
    vv = pm_ref[:, off:off + nkv * HEAD]
    for h in range(nkv):
        v_ref[:, h, :] = vv[:, h * HEAD:(h + 1) * HEAD]
    vbf_ref[0] = vv.T.astype(BF16)
    off += nkv * HEAD
    for h in range(nidx):
        x = pm_ref[:, off + h * HEAD:off + (h + 1) * HEAD]
        qi_ref[:, h * HEAD:(h + 1) * HEAD] = _rope(x, c, a, b).astype(qi_ref.dtype)
    x = pt_ref[:, 0:HEAD]
    mu = jnp.mean(x, axis=-1, keepdims=True)
    xc = x - mu
    y = xc * lax.rsqrt(jnp.mean(xc * xc, axis=-1, keepdims=True) + EPS)
    kir = _rope(y * kg_ref[...] + kb_ref[...], c, a, b)
    ki_ref[...] = kir
    kibf_ref[...] = kir.astype(BF16)
    wi_ref[...] = pt_ref[:, HEAD:HEAD + nidx] * wi_scale


def dsa_post(proj_main, proj_tail, tables, qn, kn, kg, kb, *, nq, nkv, nidx, t_tab):
    m = proj_main.shape[0]
    tm = min(m, DSA_TQ, t_tab)
    ntab = t_tab // tm
    wm = proj_main.shape[1]
    wt = proj_tail.shape[1]
    rowspec = lambda wdt: pl.BlockSpec((tm, wdt), lambda i: (i, 0))
    tabspec = pl.BlockSpec((tm, HEAD), lambda i: (i % ntab, 0))
    vec = pl.BlockSpec((1, HEAD), lambda i: (0, 0))
    kern = functools.partial(_dsa_post_kernel, nq=nq, nkv=nkv, nidx=nidx,
                             wi_scale=float((nidx * HEAD) ** -0.5))
    kv_spec = pl.BlockSpec((tm, nkv, HEAD), lambda i: (i, 0, 0))
    kv_shape = jax.ShapeDtypeStruct((m, nkv, HEAD), F32)
    vt_spec = pl.BlockSpec((1, nkv * HEAD, tm), lambda i: (i, 0, 0))
    vt_shape = jax.ShapeDtypeStruct((m // tm, nkv * HEAD, tm), BF16)
    flat = lambda wdt, dt: (rowspec(wdt), jax.ShapeDtypeStruct((m, wdt), dt))
    outs = [flat(nq * HEAD, BF16), flat(nidx * HEAD, BF16), (kv_spec, kv_shape), (kv_spec, kv_shape),
            flat(HEAD, F32), flat(nidx, F32), flat(nkv * HEAD, BF16), (vt_spec, vt_shape), flat(HEAD, BF16)]
    return pl.pallas_call(
        kern,
        grid=(m // tm,),
        in_specs=[rowspec(wm), rowspec(wt), tabspec, tabspec, tabspec, vec, vec, vec, vec],
        out_specs=[o[0] for o in outs],
        out_shape=[o[1] for o in outs],
        compiler_params=_cparams(("parallel",)),
        name="dsa_post",
    )(proj_main, proj_tail, *tables, qn.reshape(1, HEAD), kn.reshape(1, HEAD),
      kg.reshape(1, HEAD), kb.reshape(1, HEAD))


def _sort_key(x):
    bits = pltpu.bitcast(x, jnp.int32)
    return jnp.where(bits < 0, bits ^ jnp.int32(0x7FFFFFFF), bits)


def _kth_largest_key(count_ge, rows, k):
    def body(it, thr):
        bit = lax.shift_left(jnp.int32(1), jnp.int32(31) - it)
        trial = thr + bit
        return jnp.where(count_ge(trial) >= k, trial, thr)

    shape = rows if isinstance(rows, tuple) else (rows, 1)
    init = jnp.full(shape, jnp.iinfo(jnp.int32).min, jnp.int32)
    return lax.fori_loop(0, 32, body, init)


def _dsa_prompt_kernel(qi_ref, wit_ref, kib_ref, q_ref, kb_ref, vt_ref, o_ref, key_ref, bias_ref,
                       *, tq, nidx, nq, nkv, topk, scale):
    i = pl.program_id(1)
    ntile = i + 1
    keyi = lax.broadcasted_iota(jnp.int32, (tq, tq), 0)
    qryi = lax.broadcasted_iota(jnp.int32, (tq, tq), 1)
    tri = keyi <= qryi

    def idx_tile(j, carry):
        kt = kib_ref[pl.ds(pl.multiple_of(j * tq, tq), tq), :]
        acc = jnp.zeros((tq, tq), F32)
        for h in range(nidx):
            d = _dot_nt(kt, qi_ref[:, h * HEAD:(h + 1) * HEAD])
            acc = acc + jnp.maximum(d, 0.0) * wit_ref[h:h + 1, :]
        acc = jnp.where(tri | (j < i), acc, NEG_BIG)
        key_ref[j] = _sort_key(acc)
        return carry

    lax.fori_loop(0, ntile, idx_tile, 0)

    sub = V7X_SUBLANES

    def count_ge(trial):
        def tile(j, cnt):
            hit = jnp.where(key_ref[j] >= trial, 1.0, 0.0)
            return cnt + jnp.sum(hit.reshape(tq // sub, sub, tq), axis=0)

        return jnp.sum(lax.fori_loop(0, ntile, tile, jnp.zeros((sub, tq), F32)), axis=0, keepdims=True)

    thr = _kth_largest_key(count_ge, (1, tq), float(topk))

    def count_gt(j, cnt):
        return cnt + jnp.sum(jnp.where(key_ref[j] > thr, 1.0, 0.0), axis=0, keepdims=True)

    room = float(topk) - lax.fori_loop(0, ntile, count_gt, jnp.zeros((1, tq), F32))
    lower = jnp.where(qryi <= keyi, 1.0, 0.0).astype(BF16)

    def bias_tile(j, seen):
        k = key_ref[j]
        tie = jnp.where(k == thr, 1.0, 0.0)
        rank = seen + _dot(lower, tie.astype(BF16))
        keep = ((k > thr) | ((k == thr) & (rank <= room))) & (tri | (j < i))
        bias_ref[j] = jnp.where(keep, 0.0, NEG_BIG)
        return seen + jnp.sum(tie, axis=0, keepdims=True)

    lax.fori_loop(0, ntile, bias_tile, jnp.zeros((1, tq), F32))

    c2 = scale * math.log2(math.e)
    grp = nq // nkv
    ngb = DSA_KV_GROUPS_PER_BODY
    for n0 in range(0, nkv, ngb):
        heads = [(n, n * grp + g) for n in range(n0, n0 + ngb) for g in range(grp)]
        qs = [q_ref[:, h * HEAD:(h + 1) * HEAD] for _, h in heads]

        def att_tile(j, carry, n0=n0, heads=heads, qs=qs):
            r0 = pl.multiple_of(j * tq, tq)
            kts = {n: kb_ref[pl.ds(r0, tq), n * HEAD:(n + 1) * HEAD] for n in range(n0, n0 + ngb)}
            vts = {n: vt_ref[j, n * HEAD:(n + 1) * HEAD, :] for n in range(n0, n0 + ngb)}
            bias = bias_ref[j]
            ss = [_dot_nt(kts[n], qs[x]) + bias for x, (n, _) in enumerate(heads)]
            mid = []
            for x in range(len(heads)):
                m_i, l_i, _ = carry[x]
                m_new = jnp.maximum(m_i, jnp.max(ss[x], axis=0, keepdims=True))
                p = jnp.exp2((ss[x] - m_new) * c2)
                alpha = jnp.exp2((m_i - m_new) * c2)
                mid.append((m_new, alpha * l_i + jnp.sum(p, axis=0, keepdims=True), alpha, p.astype(BF16)))
            pvs = [_dot(vts[n], mid[x][3]) for x, (n, _) in enumerate(heads)]
            return tuple((mid[x][0], mid[x][1], mid[x][2] * carry[x][2] + pvs[x]) for x in range(len(heads)))

        one = (jnp.full((1, tq), NEG_BIG, F32), jnp.zeros((1, tq), F32), jnp.zeros((HEAD, tq), F32))
        fin = lax.fori_loop(0, ntile, att_tile, (one,) * len(heads))
        for x, (_, h) in enumerate(heads):
            o_ref[:, h * HEAD:(h + 1) * HEAD] = (fin[x][2] / fin[x][1]).T.astype(o_ref.dtype)


def dsa_prompt_attend(q_bf, qi_bf, wi_t, ki_bf, k_bf, vt_bf, *, bsz, t, nq, nkv, nidx):
    tq = DSA_TQ
    nqb = t // tq
    topk = min(TOPK_MAX, t // 4)
    assert topk <= tq and t % tq == 0
    qrow = lambda wdt: pl.BlockSpec((tq, wdt), lambda b, i: (b * nqb + i, 0))
    brow = lambda wdt: pl.BlockSpec((t, wdt), lambda b, i: (b, 0))
    kern = functools.partial(_dsa_prompt_kernel, tq=tq, nidx=nidx, nq=nq, nkv=nkv, topk=topk,
                             scale=float(HEAD ** -0.5))
    return pl.pallas_call(
        kern,
        grid=(bsz, nqb),
        in_specs=[qrow(nidx * HEAD), pl.BlockSpec((nidx, tq), lambda b, i: (0, b * nqb + i)), brow(HEAD),
                  qrow(nq * HEAD), brow(nkv * HEAD),
                  pl.BlockSpec((nqb, nkv * HEAD, tq), lambda b, i: (b, 0, 0))],
        out_specs=qrow(nq * HEAD),
        out_shape=jax.ShapeDtypeStruct((bsz * t, nq * HEAD), BF16),
        scratch_shapes=[pltpu.VMEM((nqb, tq, tq), jnp.int32), pltpu.VMEM((nqb, tq, tq), F32)],
        compiler_params=_cparams(("parallel", "arbitrary")),
        name="dsa_prompt",
    )(qi_bf, wi_t, ki_bf, q_bf, k_bf, vt_bf)


def _dsa_s_scores_kernel(pt_ref, qi_ref, wi_ref, *rest, ts, nidx, nstep, npp):
    page_refs, new_ref, sc_ref = rest[:npp], rest[npp], rest[npp + 1]
    p = pl.program_id(1)
    is_new = p == nstep
    psz = new_ref.shape[1]
    first = jnp.where(is_new, new_ref[0], page_refs[0][...])
    kcat = jnp.concatenate([first] + [page_refs[u][...] for u in range(1, npp)], axis=0).astype(BF16)
    d = _dot_nt(qi_ref[...], kcat)
    r = jnp.maximum(d, 0.0) * wi_ref[...]
    rows = [jnp.sum(r[tt * nidx:(tt + 1) * nidx, :], axis=0, keepdims=True) for tt in range(ts)]
    sc = jnp.concatenate(rows + [jnp.full((V7X_SUBLANES - ts, npp * psz), NEG_BIG, F32)], axis=0)
    rowi = lax.broadcasted_iota(jnp.int32, sc.shape, 0)
    coli = lax.broadcasted_iota(jnp.int32, sc.shape, 1)
    sc_ref[0] = jnp.where(is_new & (coli > rowi), NEG_BIG, sc)


def _dsa_s_select_kernel(sc_ref, keep_ref, *, topk, past):
    key = _sort_key(sc_ref[...])
    rows, ncol = key.shape

    def count_ge(trial):
        return jnp.sum(jnp.where(key >= trial, 1.0, 0.0), axis=1, keepdims=True)

    thr = _kth_largest_key(count_ge, rows, float(topk))
    above = key > thr
    room = float(topk) - jnp.sum(jnp.where(above, 1.0, 0.0), axis=1, keepdims=True)
    rowi = lax.broadcasted_iota(jnp.int32, key.shape, 0)
    coli = lax.broadcasted_iota(jnp.int32, key.shape, 1)
    visible = coli <= past + (rowi % V7X_SUBLANES)
    li = lax.broadcasted_iota(jnp.int32, (V7X_LANES, V7X_LANES), 0)
    lj = lax.broadcasted_iota(jnp.int32, (V7X_LANES, V7X_LANES), 1)
    upper = jnp.where(li <= lj, 1.0, 0.0).astype(BF16)
    seen = jnp.zeros((rows, 1), F32)
    for c in range(ncol // V7X_LANES):
        sl = slice(c * V7X_LANES, (c + 1) * V7X_LANES)
        tie = jnp.where(key[:, sl] == thr, 1.0, 0.0)
        rank = seen + _dot(tie.astype(BF16), upper)
        keep = (above[:, sl] | ((key[:, sl] == thr) & (rank <= room))) & visible[:, sl]
        keep_ref[:, sl] = jnp.where(keep, 1.0, 0.0)
        seen = seen + jnp.sum(tie, axis=1, keepdims=True)


def _dsa_s_attend_kernel(pt_ref, q_ref, keep_ref, expand_ref, struct_ref, *rest, nstep, npp, scale):
    kp_refs, vp_refs = rest[:npp], rest[npp:2 * npp]
    kn_ref, vn_ref, o_ref, m_ref, l_ref, acc_ref = rest[2 * npp:]
    p = pl.program_id(1)
    is_new = p == nstep
    psz = kn_ref.shape[1]

    @pl.when(p == 0)
    def _():
        m_ref[...] = jnp.full_like(m_ref, NEG_BIG)
        l_ref[...] = jnp.zeros_like(l_ref)
        acc_ref[...] = jnp.zeros_like(acc_ref)

    def flat(refs, new_ref, u):
        page = refs[u][...]
        if u == 0:
            page = jnp.where(is_new, new_ref[0], page)
        return page.reshape(-1, HEAD).astype(BF16)

    c2 = scale * math.log2(math.e)
    q = q_ref[0]
    struct = struct_ref[...]
    ss = []
    for u in range(npp):
        seen = _dot(keep_ref[0, :, u * psz:(u + 1) * psz], expand_ref[...])
        ss.append(_dot_nt(q, flat(kp_refs, kn_ref, u)) + jnp.where(seen * struct > 0.5, 0.0, NEG_BIG))
    m_i = m_ref[...]
    m_new = m_i
    for u in range(npp):
        m_new = jnp.maximum(m_new, jnp.max(ss[u], axis=1, keepdims=True))
    alpha = jnp.exp2((m_i - m_new) * c2)
    l_new = alpha * l_ref[...]
    acc = alpha * acc_ref[...]
    for u in range(npp):
        pr = jnp.exp2((ss[u] - m_new) * c2)
        l_new = l_new + jnp.sum(pr, axis=1, keepdims=True)
        acc = acc + _dot(pr.astype(BF16), flat(vp_refs, vn_ref, u))
    m_ref[...], l_ref[...], acc_ref[...] = m_new, l_new, acc

    @pl.when(is_new)
    def _():
        o_ref[0] = acc / l_new


def dsa_sample_attend(q_bf, qi_bf, wi, ki_new, k_new, v_new, ck, cv, cki, page_table,
                      *, layer_j, bsz, ts, nq, nkv, nidx):
    psz = ck.shape[2]
    npages = page_table.shape[1]
    npp = DSA_PAGES_PER_STEP
    assert npages % npp == 0
    nstep = npages // npp
    past = npages * psz
    topk = min(TOPK_MAX, (past + ts) // 4)
    grp = nq // nkv
    ncol = (nstep + 1) * npp * psz
    sub = V7X_SUBLANES
    padrows = lambda x, tail: jnp.pad(x.reshape((bsz, ts) + tail), ((0, 0), (0, psz - ts)) + ((0, 0),) * len(tail))

    def page_idx(u, ndim):
        def index(b, p, pt):
            return (layer_j, pt[b, jnp.minimum(p * npp + u, npages - 1)]) + (0,) * ndim
        return index

    scores = pl.pallas_call(
        functools.partial(_dsa_s_scores_kernel, ts=ts, nidx=nidx, nstep=nstep, npp=npp),
        grid_spec=pltpu.PrefetchScalarGridSpec(
            num_scalar_prefetch=1,
            grid=(bsz, nstep + 1),
            in_specs=[pl.BlockSpec((ts * nidx, HEAD), lambda b, p, pt: (b, 0)),
                      pl.BlockSpec((ts * nidx, 1), lambda b, p, pt: (b, 0))]
                     + [pl.BlockSpec((None, None, psz, HEAD), page_idx(u, 2)) for u in range(npp)]
                     + [pl.BlockSpec((1, psz, HEAD), lambda b, p, pt: (b, 0, 0))],
            out_specs=pl.BlockSpec((1, sub, npp * psz), lambda b, p, pt: (b, 0, p))),
        out_shape=jax.ShapeDtypeStruct((bsz, sub, ncol), F32),
        compiler_params=_cparams(("parallel", "arbitrary")),
        name="dsa_s_scores",
    )(page_table, qi_bf.reshape(bsz * ts * nidx, HEAD), wi.reshape(bsz * ts * nidx, 1),
      *([cki] * npp), padrows(ki_new, (HEAD,)))

    keep = pl.pallas_call(
        functools.partial(_dsa_s_select_kernel, topk=topk, past=past),
        out_shape=jax.ShapeDtypeStruct((bsz * sub, ncol), F32),
        compiler_params=pltpu.CompilerParams(vmem_limit_bytes=VMEM_LIMIT),
        name="dsa_s_select",
    )(scores.reshape(bsz * sub, ncol))
    nrow = nkv * ts * grp
    keep = jnp.broadcast_to(keep.reshape(bsz, 1, sub, 1, ncol)[:, :, :ts], (bsz, nkv, ts, grp, ncol))
    keep = keep.reshape(bsz, nrow, ncol).astype(BF16)
    q_all = q_bf.reshape(bsz, ts, nkv, grp, HEAD).transpose(0, 2, 1, 3, 4).reshape(bsz, nrow, HEAD)
    flat_col = np.arange(psz * nkv)
    expand = jnp.asarray(np.arange(psz)[:, None] == flat_col[None, :] // nkv, BF16)
    struct = jnp.asarray(np.arange(nrow)[:, None] // (ts * grp) == flat_col[None, :] % nkv, F32)
    kv_new_spec = pl.BlockSpec((1, psz, nkv, HEAD), lambda b, p, pt: (b, 0, 0, 0))
    const = lambda shape: pl.BlockSpec(shape, lambda b, p, pt: (0, 0))
    o = pl.pallas_call(
        functools.partial(_dsa_s_attend_kernel, nstep=nstep, npp=npp, scale=float(HEAD ** -0.5)),
        grid_spec=pltpu.PrefetchScalarGridSpec(
            num_scalar_prefetch=1,
            grid=(bsz, nstep + 1),
            in_specs=[pl.BlockSpec((1, nrow, HEAD), lambda b, p, pt: (b, 0, 0)),
                      pl.BlockSpec((1, nrow, npp * psz), lambda b, p, pt: (b, 0, p)),
                      const(expand.shape), const(struct.shape)]
                     + [pl.BlockSpec((None, None, psz, nkv, HEAD), page_idx(u, 3)) for u in range(npp)] * 2
                     + [kv_new_spec, kv_new_spec],
            out_specs=pl.BlockSpec((1, nrow, HEAD), lambda b, p, pt: (b, 0, 0)),
            scratch_shapes=[pltpu.VMEM((nrow, 1), F32), pltpu.VMEM((nrow, 1), F32),
                            pltpu.VMEM((nrow, HEAD), F32)]),
        out_shape=jax.ShapeDtypeStruct((bsz, nrow, HEAD), F32),
        compiler_params=_cparams(("parallel", "arbitrary")),
        name="dsa_s_attend",
    )(page_table, q_all, keep, expand, struct, *([ck] * npp), *([cv] * npp),
      padrows(k_new, (nkv, HEAD)), padrows(v_new, (nkv, HEAD)))
    o = o.reshape(bsz, nkv, ts, grp, HEAD).transpose(0, 2, 1, 3, 4).reshape(bsz * ts, nq * HEAD)
    return o.astype(BF16)


def kernel(x_prompt, x_sample, state_hgrn, state_rglru_h, state_rglru_conv, cache_k, cache_v, cache_kidx,
           page_table, p_prompt, p_sample, norm_mix, norm_mlp, norm_pe, w_in_ab, w_out_ab, hgrn_lb_logits,
           hgrn_out_norm, rg_conv_w, rg_conv_b, rg_wa, rg_ba, rg_wx, rg_bx, rg_lambda, w_in_c, w_out_c,
           c_q_norm, c_k_norm, idx_k_ln_g, idx_k_ln_b, w_up, w_down, w_pe, w_pg):
    bp, tp, d = x_prompt.shape
    bs, ts, _ = x_sample.shape
    depth = norm_mix.shape[0]
    a_width = hgrn_out_norm.shape[1]
    b_width = rg_lambda.shape[1]
    nh_a = a_width // HEAD
    nkv = cache_k.shape[3]
    nq = w_out_c.shape[1] // HEAD
    nidx = (w_in_c.shape[2] - nq * HEAD - 2 * nkv * HEAD - HEAD) // (HEAD + 1)
    past = page_table.shape[1] * cache_k.shape[2]
    c_main = nq * HEAD + 2 * nkv * HEAD + nidx * HEAD
    pe_dim = p_prompt.shape[-1]

    hp = x_prompt.reshape(bp * tp, d)
    hs = x_sample.reshape(bs * ts, d)
    tab_p = _rope_tables(jnp.arange(tp, dtype=jnp.int32))
    tab_s = _rope_tables(past + jnp.arange(ts, dtype=jnp.int32))
    tab_s = tuple(jnp.tile(tb, (bs, 1)) for tb in tab_s)
    ts_pad = SAMPLE_CHUNK_ROWS
    assert ts <= ts_pad
    pad_s = lambda x: jnp.pad(x.reshape(bs, ts, -1), ((0, 0), (0, ts_pad - ts), (0, 0))).reshape(bs * ts_pad, -1)
    unpad_s = lambda x: x.reshape(bs, ts_pad, -1)[:, :ts].reshape(bs * ts, -1)

    outs = {k: [] for k in ("hg_p", "rh_p", "rc_p", "k_p", "v_p", "ki_p",
                            "hg_s", "rh_s", "rc_s", "k_s", "v_s", "ki_s")}
    wd_bf = w_down.astype(BF16)
    w_in_c_t = w_in_c.transpose(0, 2, 1)
    pp_all = p_prompt.reshape(depth, -1, pe_dim).astype(BF16)
    ps_all = p_sample.reshape(depth, -1, pe_dim).astype(BF16)
    for layer in range(depth):
        j = layer // 2
        xpn = rmsnorm_bf16(hp, norm_mix[layer])
        xsn = rmsnorm_bf16(hs, norm_mix[layer])
        if layer % 2 == 0:
            gb_blk = 4 * a_width // b_width
            x_blk = gb_blk + 1
            rg_args = (rg_conv_w[j], rg_conv_b[j], rg_wa[j], rg_ba[j], rg_wx[j], rg_bx[j], rg_lambda[j])
            n_in = w_in_ab.shape[2]

            proj, w_in = matmul(xpn, w_in_ab, layer=j, emit_bf16=True)
            mix_w = a_width + b_width
            mix, sa = hgrn(proj, hgrn_lb_logits, hgrn_out_norm[j], jnp.zeros((bp, nh_a, HEAD, HEAD), F32),
                           bsz=bp, t=tp, layer_j=j, r=min(HGRN_CHUNK_ROWS, tp), out_width=mix_w)
            mix, hl, cn = rglru(proj, mix, jnp.zeros((bp, CONV_W - 1, b_width), F32), jnp.zeros((bp, b_width), F32),
                                *rg_args, bsz=bp, t=tp, r=min(RGLRU_BLOCK_ROWS, tp), gb_blk=gb_blk, x_blk=x_blk,
                                out_blk=a_width // b_width, first_pos_zero=True)
            hp, w_out = matmul(mix, w_out_ab, layer=j, emit_bf16=True, epilogue="residual", res=hp)
            outs["hg_p"].append(sa)
            outs["rh_p"].append(hl.reshape(bp, b_width))
            outs["rc_p"].append(cn)

            proj = pad_s(matmul(xsn, w_in, n_out=n_in))
            mix, sa = hgrn(proj, hgrn_lb_logits, hgrn_out_norm[j], state_hgrn[j],
                           bsz=bs, t=ts_pad, layer_j=j, r=ts_pad, out_width=mix_w, t_valid=ts)
            mix, hl, cn = rglru(proj, mix, state_rglru_conv[j], state_rglru_h[j], *rg_args, bsz=bs, t=ts_pad,
                                r=ts_pad, gb_blk=gb_blk, x_blk=x_blk, out_blk=a_width // b_width,
                                first_pos_zero=False, t_valid=ts)
            hs = matmul(unpad_s(mix), w_out, n_out=d, epilogue="residual", res=hs)
            outs["hg_s"].append(sa)
            outs["rh_s"].append(hl.reshape(bs, b_width))
            outs["rc_s"].append(cn)
        else:
            norms = (c_q_norm[j], c_k_norm[j], idx_k_ln_g[j], idx_k_ln_b[j])
            pm, w_main = matmul(xpn, w_in_c_t, layer=j, w_t=True, n_out=c_main, emit_bf16=True)
            ptl, w_tail = matmul(xpn, w_in_c_t, layer=j, w_t=True, n_out=2 * HEAD, col0=c_main, emit_bf16=True)
            post = dsa_post(pm, ptl, tab_p, *norms, nq=nq, nkv=nkv, nidx=nidx, t_tab=tp)
            q_bf, qi_bf, kk, vv, ki, wi, k_bf, vt_bf, ki_bf = post
            o = dsa_prompt_attend(q_bf, qi_bf, wi.T, ki_bf, k_bf, vt_bf, bsz=bp, t=tp, nq=nq, nkv=nkv, nidx=nidx)
            hp, w_out = matmul(o, w_out_c, layer=j, emit_bf16=True, epilogue="residual", res=hp)
            outs["k_p"].append(kk.reshape(bp, tp, nkv, HEAD))
            outs["v_p"].append(vv.reshape(bp, tp, nkv, HEAD))
            outs["ki_p"].append(ki.reshape(bp, tp, HEAD))

            post = dsa_post(matmul(xsn, w_main, w_t=True, n_out=c_main), matmul(xsn, w_tail, w_t=True, n_out=2 * HEAD),
                            tab_s, *norms, nq=nq, nkv=nkv, nidx=nidx, t_tab=bs * ts)
            q_bf, qi_bf, kk, vv, ki, wi, _, _, _ = post
            o = dsa_sample_attend(q_bf, qi_bf, wi, ki, kk, vv, cache_k, cache_v, cache_kidx,
                                  page_table, layer_j=j, bsz=bs, ts=ts, nq=nq, nkv=nkv, nidx=nidx)
            hs = matmul(o, w_out, n_out=d, epilogue="residual", res=hs)
            outs["k_s"].append(kk.reshape(bs, ts, nkv, HEAD))
            outs["v_s"].append(vv.reshape(bs, ts, nkv, HEAD))
            outs["ki_s"].append(ki.reshape(bs, ts, HEAD))

        d_ff = w_up.shape[2]
        mid, wu = matmul(rmsnorm_bf16(hp, norm_mlp[layer]), w_up, layer=layer, emit_bf16=True,
                         epilogue="relu2", out_dtype=BF16)
        hp = matmul(mid, wd_bf, layer=layer, epilogue="residual", res=hp)
        hp, wpg = matmul(rmsnorm_bf16(hp, norm_pe[layer]), w_pg, layer=layer, emit_bf16=True, epilogue="pe_gate",
                         res=hp, p=pp_all[layer], w_pe=w_pe, pe_layer=layer)
        mid = matmul(rmsnorm_bf16(hs, norm_mlp[layer]), wu, n_out=d_ff, epilogue="relu2", out_dtype=BF16)
        hs = matmul(mid, wd_bf, layer=layer, epilogue="residual", res=hs)
        hs = matmul(rmsnorm_bf16(hs, norm_pe[layer]), wpg, n_out=d, epilogue="pe_gate", res=hs,
                    p=ps_all[layer], w_pe=w_pe, pe_layer=layer)

    st = lambda name: jnp.stack(outs[name])
    return (hp.reshape(bp, tp, d), hs.reshape(bs, ts, d),
            st("hg_p"), st("rh_p"), st("rc_p"), st("k_p"), st("v_p"), st("ki_p"),
            st("hg_s"), st("rh_s"), st("rc_s"), st("k_s"), st("v_s"), st("ki_s"))
```
